```python
import jax, jax.numpy as jnp
from jax import lax
import numpy as np

D_MODEL = 2048
BATCH = 4
SEQ = 2048
DEPTH = 2
DEC_BATCH = 32
DEC_SEQ = 1
PAST_LEN = 16384
PAGE_SIZE = 128

MIX_WIDTH = D_MODEL
ATTN_WIDTH = MIX_WIDTH // 2
GM_WIDTH = MIX_WIDTH - ATTN_WIDTH
HEAD_DIM = 64
N_HEADS = ATTN_WIDTH // HEAD_DIM
N_KV_HEADS = N_HEADS // 4
GQA_GROUP = N_HEADS // N_KV_HEADS
WINDOW = 128
ROPE_THETA = 10000.0
CHUNK = 128
GM_HEADS = 8
GM_HD = GM_WIDTH // GM_HEADS
D_FF = 11 * D_MODEL // 4
CONV_W = 3
KV_WIDTH = N_KV_HEADS * HEAD_DIM
IN_COLS = ATTN_WIDTH + 2 * KV_WIDTH + 2 * GM_WIDTH
EPS = 1e-6
NEG = -1e30

kernel_name = "hymba_swa_sink_chunk_gmlp_convffn_adaln_step"


def rms_norm(x, g):
    xf = x.astype(jnp.float32)
    y = xf * lax.rsqrt(jnp.mean(xf * xf, axis=-1, keepdims=True) + EPS)
    return (y * g.astype(jnp.float32)).astype(x.dtype)


def rope(x, pos):
    half = HEAD_DIM // 2
    inv = ROPE_THETA ** (-jnp.arange(half, dtype=jnp.float32) / half)
    ang = pos.astype(jnp.float32)[:, None] * inv[None, :]
    cos = jnp.cos(ang)[None, :, None, :]
    sin = jnp.sin(ang)[None, :, None, :]
    xf = x.astype(jnp.float32)
    x1, x2 = xf[..., :half], xf[..., half:]
    return jnp.concatenate([x1 * cos - x2 * sin, x2 * cos + x1 * sin], axis=-1).astype(x.dtype)


def sink_attention(q, k, v, mask, sinks):
    B, N, Tq = q.shape[:3]
    s = jnp.einsum('bnqkgd,bnjkd->bnkgqj', q.astype(jnp.float32), k.astype(jnp.float32)) * (HEAD_DIM ** -0.5)
    s = jnp.where(mask[None, :, None, None], s, NEG)
    sk = sinks.astype(jnp.float32).reshape(N_KV_HEADS, GQA_GROUP)[None, None, :, :, None]
    m = jnp.maximum(s.max(axis=-1), sk)
    p = jnp.exp(s - m[..., None])
    p = p / (p.sum(axis=-1) + jnp.exp(sk - m))[..., None]
    o = jnp.einsum('bnkgqj,bnjkd->bnqkgd', p, v.astype(jnp.float32))
    return o.reshape(B, N * Tq, N_HEADS * HEAD_DIM)


def swa_prompt(q, k, v, sinks):
    B, S = q.shape[:2]
    nb = S // WINDOW
    qb = q.reshape(B, nb, WINDOW, N_KV_HEADS, GQA_GROUP, HEAD_DIM)
    kb = k.reshape(B, nb, WINDOW, N_KV_HEADS, HEAD_DIM)
    vb = v.reshape(B, nb, WINDOW, N_KV_HEADS, HEAD_DIM)
    pad = jnp.zeros_like(kb[:, :1])
    kk = jnp.concatenate([jnp.concatenate([pad, kb[:, :-1]], axis=1), kb], axis=2)
    vv = jnp.concatenate([jnp.concatenate([pad, vb[:, :-1]], axis=1), vb], axis=2)
    n = jnp.arange(nb)[:, None, None]
    qpos = n * WINDOW + jnp.arange(WINDOW)[None, :, None]
    kpos = (n - 1) * WINDOW + jnp.arange(2 * WINDOW)[None, None, :]
    mask = (kpos <= qpos) & (kpos >= qpos - WINDOW) & (kpos >= 0)
    return sink_attention(qb, kk, vv, mask, sinks)


def swa_sample(q, k_new, v_new, k_cache, v_cache, sinks):
    T = q.shape[1]
    W = k_cache.shape[1]
    kk = jnp.concatenate([k_cache.astype(k_new.dtype), k_new], axis=1)
    vv = jnp.concatenate([v_cache.astype(v_new.dtype), v_new], axis=1)
    qpos = PAST_LEN + jnp.arange(T)[:, None]
    kpos = PAST_LEN - W + jnp.arange(W + T)[None, :]
    mask = (kpos <= qpos) & (kpos >= qpos - WINDOW) & (kpos >= 0)
    o = sink_attention(q[:, None], kk[:, None], vv[:, None], mask[None], sinks)
    return o, kk[:, -W:], vv[:, -W:]


def chunk_spatial_gate(u, v, ws, bs):
    B, T, _ = v.shape
    nc = -(-T // CHUNK)
    vp = jnp.pad(v, ((0, 0), (0, nc * CHUNK - T), (0, 0))).reshape(B, nc, CHUNK, GM_HEADS, GM_HD)
    causal = jnp.tril(jnp.ones((CHUNK, CHUNK), dtype=bool))
    wm = jnp.where(causal[None], ws, jnp.zeros_like(ws))
    s = jnp.einsum('hij,bcjhd->bcihd', wm, vp) + jnp.swapaxes(bs, 0, 1)[None, None, :, :, None]
    s = s.reshape(B, nc * CHUNK, GM_WIDTH)[:, :T]
    return u * s


def conv_ffn(h, prev, w_gate, w_up, conv_w, conv_b, w_down):
    a = h @ w_gate
    T = a.shape[1]
    ap = jnp.concatenate([prev.astype(a.dtype), a], axis=1)
    conv = conv_b
    for t in range(CONV_W):
        conv = conv + conv_w[t] * ap[:, t:t + T]
    y = (jax.nn.silu(conv) * (h @ w_up)) @ w_down
    return y, ap[:, -(CONV_W - 1):]


def block(x, c, pos, kv_cache, conv_prev, w_ada, b_ada, g1, g2, w_in, gm_gain, gm_ws, gm_bs,
          sinks, w_out, w_gate, w_up, conv_w, conv_b, w_down):
    B, T, _ = x.shape
    mod = jax.nn.silu(c) @ w_ada + b_ada
    sh1, sc1, gt1, sh2, sc2, gt2 = [m[:, None, :] for m in jnp.split(mod, 6, axis=-1)]
    h = rms_norm(x, g1) * (1 + sc1) + sh1
    z = h @ w_in
    q, k, v, gu, gv = jnp.split(z, [ATTN_WIDTH, ATTN_WIDTH + KV_WIDTH, ATTN_WIDTH + 2 * KV_WIDTH,
                                    ATTN_WIDTH + 2 * KV_WIDTH + GM_WIDTH], axis=-1)
    q = rope(q.reshape(B, T, N_HEADS, HEAD_DIM), pos).reshape(B, T, N_KV_HEADS, GQA_GROUP, HEAD_DIM)
    k = rope(k.reshape(B, T, N_KV_HEADS, HEAD_DIM), pos)
    v = v.reshape(B, T, N_KV_HEADS, HEAD_DIM)
    if kv_cache is None:
        attn = swa_prompt(q, k, v, sinks)
        k_state, v_state = k[:, -WINDOW:], v[:, -WINDOW:]
    else:
        attn, k_state, v_state = swa_sample(q, k, v, kv_cache[0], kv_cache[1], sinks)
    u = jax.nn.gelu(gu)
    gvn = rms_norm(jax.nn.gelu(gv), gm_gain)
    gm = chunk_spatial_gate(u, gvn, gm_ws, gm_bs)
    gm_state = gvn[:, ((T - 1) // CHUNK) * CHUNK:]
    mix = jnp.concatenate([attn.astype(x.dtype), gm], axis=-1) @ w_out
    x = x + gt1 * mix
    h2 = rms_norm(x, g2) * (1 + sc2) + sh2
    f, conv_state = conv_ffn(h2, conv_prev, w_gate, w_up, conv_w, conv_b, w_down)
    x = x + gt2 * f
    return x, k_state, v_state, gm_state, conv_state


def setup_inputs(seed: int = 0) -> dict:
    key = jax.random.key(seed)
    ks = jax.random.split(key, 24)
    nrm = lambda k, shape, s: jax.random.normal(k, shape, dtype=jnp.float32) * s
    D = D_MODEL
    return {
        "x_prompt": nrm(ks[0], (BATCH, SEQ, D), 1.0),
        "x_sample": nrm(ks[1], (DEC_BATCH, DEC_SEQ, D), 1.0),
        "cache_k": nrm(ks[2], (DEPTH, DEC_BATCH, WINDOW, N_KV_HEADS, HEAD_DIM), 1.0),
        "cache_v": nrm(ks[3], (DEPTH, DEC_BATCH, WINDOW, N_KV_HEADS, HEAD_DIM), 1.0),
        "state_conv": nrm(ks[4], (DEPTH, DEC_BATCH, CONV_W - 1, D_FF), 1.0),
        "c_prompt": nrm(ks[5], (BATCH, D), 1.0),
        "c_sample": nrm(ks[6], (DEC_BATCH, D), 1.0),
        "w_ada": nrm(ks[7], (DEPTH, D, 6 * D), 0.3 * D ** -0.5),
        "b_ada": nrm(ks[8], (DEPTH, 6 * D), 0.02),
        "g_norm1": 1.0 + nrm(ks[9], (DEPTH, D), 0.02),
        "g_norm2": 1.0 + nrm(ks[10], (DEPTH, D), 0.02),
        "w_in": nrm(ks[11], (DEPTH, D, IN_COLS), D ** -0.5),
        "gm_gain": 1.0 + nrm(ks[12], (DEPTH, GM_WIDTH), 0.02),
        "gm_ws": nrm(ks[13], (DEPTH, GM_HEADS, CHUNK, CHUNK), CHUNK ** -0.5),
        "gm_bs": 1.0 + nrm(ks[14], (DEPTH, GM_HEADS, CHUNK), 0.02),
        "sinks": nrm(ks[15], (DEPTH, N_HEADS), 1.0),
        "w_out": nrm(ks[16], (DEPTH, MIX_WIDTH, D), MIX_WIDTH ** -0.5),
        "w_gate": nrm(ks[17], (DEPTH, D, D_FF), D ** -0.5),
        "w_up": nrm(ks[18], (DEPTH, D, D_FF), D ** -0.5),
        "conv_w": nrm(ks[19], (DEPTH, CONV_W, D_FF), CONV_W ** -0.5),
        "conv_b": nrm(ks[20], (DEPTH, D_FF), 0.02),
        "w_down": nrm(ks[21], (DEPTH, D_FF, D), D_FF ** -0.5),
        "g_final": 1.0 + nrm(ks[22], (D,), 0.02),
    }


def reference(x_prompt, x_sample, cache_k, cache_v, state_conv, c_prompt, c_sample,
              w_ada, b_ada, g_norm1, g_norm2, w_in, gm_gain, gm_ws, gm_bs, sinks, w_out,
              w_gate, w_up, conv_w, conv_b, w_down, g_final):
    pos_p = jnp.arange(x_prompt.shape[1], dtype=jnp.int32)
    pos_s = PAST_LEN + jnp.arange(x_sample.shape[1], dtype=jnp.int32)
    hp, hs = x_prompt, x_sample
    kp, vp, gp, cp, ksl, vsl, gsl, csl = [], [], [], [], [], [], [], []
    for l in range(DEPTH):
        params = (w_ada[l], b_ada[l], g_norm1[l], g_norm2[l], w_in[l], gm_gain[l], gm_ws[l], gm_bs[l],
                  sinks[l], w_out[l], w_gate[l], w_up[l], conv_w[l], conv_b[l], w_down[l])
        conv0 = jnp.zeros((hp.shape[0], CONV_W - 1, D_FF), dtype=hp.dtype)
        hp, k1, v1, g1, c1 = block(hp, c_prompt, pos_p, None, conv0, *params)
        hs, k2, v2, g2, c2 = block(hs, c_sample, pos_s, (cache_k[l], cache_v[l]), state_conv[l], *params)
        kp.append(k1); vp.append(v1); gp.append(g1); cp.append(c1)
        ksl.append(k2); vsl.append(v2); gsl.append(g2); csl.append(c2)
    y_prompt = rms_norm(hp, g_final)
    y_sample = rms_norm(hs, g_final)
    return (y_prompt, y_sample,
            jnp.stack(kp), jnp.stack(vp), jnp.stack(gp), jnp.stack(cp),
            jnp.stack(ksl), jnp.stack(vsl), jnp.stack(gsl), jnp.stack(csl))
```

```python
import functools

import jax
import jax.numpy as jnp
from jax import lax
from jax.experimental import pallas as pl
from jax.experimental.pallas import tpu as pltpu

D_MODEL = 2048
DEPTH = 2
PAST_LEN = 16384
ATTN_WIDTH = 1024
GM_WIDTH = 1024
HEAD_DIM = 64
N_HEADS = 16
N_KV_HEADS = 4
WINDOW = 128
ROPE_THETA = 10000.0
CHUNK = 128
GM_HEADS = 8
D_FF = 5632
KV_WIDTH = N_KV_HEADS * HEAD_DIM
IN_COLS = ATTN_WIDTH + 2 * KV_WIDTH + 2 * GM_WIDTH
EPS = 1e-6
NEG = -1e30

Q0, K0, V0, U0, G0 = 0, ATTN_WIDTH, ATTN_WIDTH + KV_WIDTH, ATTN_WIDTH + 2 * KV_WIDTH, ATTN_WIDTH + 2 * KV_WIDTH + GM_WIDTH

V7X_LANES = 128
V7X_SUBLANES = 8
V7X_VMEM_LIMIT_BYTES = 56 * 1024 * 1024

TM = 512
TF = 512
MOD_TN = 1024
MOD_ROWS = 48

BF16 = jnp.bfloat16
F32 = jnp.float32


def _params(*semantics):
    return pltpu.CompilerParams(dimension_semantics=semantics, vmem_limit_bytes=V7X_VMEM_LIMIT_BYTES)


def _dot(a, b):
    return jnp.dot(a, b, preferred_element_type=F32)


def _dot_nt(a, b):
    return lax.dot_general(a, b, (((1,), (1,)), ((), ())), preferred_element_type=F32)


def _rms(x, g):
    return x * lax.rsqrt(jnp.mean(x * x, axis=-1, keepdims=True) + EPS) * g


def _resident(shape):
    zeros = (0,) * len(shape)
    return pl.BlockSpec(shape, lambda *_: zeros, pipeline_mode=pl.Buffered(1))


def _mod_kernel(c_ref, w_ref, b_ref, o_ref):
    s = jax.nn.silu(c_ref[...]).astype(BF16)
    o_ref[...] = _dot(s, w_ref[...].astype(BF16)) + b_ref[...]


def _modulation(c_all, w_ada, b_ada):
    n_tiles = (6 * D_MODEL) // MOD_TN
    per_chunk = D_MODEL // MOD_TN
    return pl.pallas_call(
        _mod_kernel,
        out_shape=jax.ShapeDtypeStruct((DEPTH, 6, MOD_ROWS, D_MODEL), F32),
        grid=(DEPTH, n_tiles),
        in_specs=[
            pl.BlockSpec((MOD_ROWS, D_MODEL), lambda l, j: (0, 0)),
            pl.BlockSpec((None, D_MODEL, MOD_TN), lambda l, j: (l, 0, j)),
            pl.BlockSpec((None, 1, MOD_TN), lambda l, j: (l, 0, j)),
        ],
        out_specs=pl.BlockSpec((None, None, MOD_ROWS, MOD_TN), lambda l, j: (l, j // per_chunk, 0, j % per_chunk)),
        compiler_params=_params("arbitrary", "arbitrary"),
        name="adaln_modulation",
    )(c_all, w_ada, b_ada.reshape(DEPTH, 1, 6 * D_MODEL))


def _rope_cols(z, cos, sin_signed):
    lane = lax.broadcasted_iota(jnp.int32, (1, V7X_LANES), 1)
    first_half = jnp.bitwise_and(lane, HEAD_DIM - 1) < HEAD_DIM // 2
    outs = []
    for c in range(z.shape[1] // V7X_LANES):
        zc = z[:, c * V7X_LANES:(c + 1) * V7X_LANES]
        partner = jnp.where(first_half,
                            pltpu.roll(zc, V7X_LANES - HEAD_DIM // 2, axis=1),
                            pltpu.roll(zc, HEAD_DIM // 2, axis=1))
        outs.append(zc * cos + partner * sin_signed)
    return outs


def _mix_in_core(x, sh, sc, g1, w_ref, cos, sin_signed, gain):
    h = (_rms(x, g1) * (1.0 + sc) + sh).astype(BF16)
    q_blocks = _rope_cols(_dot(h, w_ref[:, Q0:K0]), cos, sin_signed)
    q_blocks = [(qb * (HEAD_DIM ** -0.5)).astype(BF16) for qb in q_blocks]
    k_blocks = _rope_cols(_dot(h, w_ref[:, K0:V0]), cos, sin_signed)
    v = _dot(h, w_ref[:, V0:U0])
    u = jax.nn.gelu(_dot(h, w_ref[:, U0:G0]))
    gvn = _rms(jax.nn.gelu(_dot(h, w_ref[:, G0:IN_COLS])), gain)
    return q_blocks, k_blocks, v, u, gvn


def _store_cols(ref, blocks):
    for c, blk in enumerate(blocks):
        ref[:, c * V7X_LANES:(c + 1) * V7X_LANES] = blk


def _mix_in_kernel(x_ref, mod_ref, g1_ref, w_ref, cos_ref, sin_ref, gain_ref, ws_ref, bst_ref,
                   q_ref, k_ref, v_ref, gm_ref, gvl_ref):
    tm = x_ref.shape[0]
    q_blocks, k_blocks, v, u, gvn = _mix_in_core(
        x_ref[...], mod_ref[0:1], mod_ref[1:2], g1_ref[...], w_ref, cos_ref[...], sin_ref[...], gain_ref[...])
    _store_cols(q_ref, q_blocks)
    _store_cols(k_ref, k_blocks)
    v_ref[...] = v
    gvl_ref[...] = gvn[tm - CHUNK:, :]
    gvb = gvn.astype(BF16)
    row = lax.broadcasted_iota(jnp.int32, (CHUNK, CHUNK), 0)
    col = lax.broadcasted_iota(jnp.int32, (CHUNK, CHUNK), 1)
    causal = col <= row
    for hh in range(GM_HEADS):
        wm = jnp.where(causal, ws_ref[hh], 0.0).astype(BF16)
        bias = bst_ref[:, hh:hh + 1]
        cs = slice(hh * CHUNK, (hh + 1) * CHUNK)
        for c in range(tm // CHUNK):
            rs = slice(c * CHUNK, (c + 1) * CHUNK)
            s = _dot(wm, gvb[rs, cs]) + bias
            gm_ref[rs, cs] = (u[rs, cs] * s).astype(BF16)


def _mix_in(x, mod, g1, w_in, cos, sin_signed, gain, ws, bst, seq):
    tokens = x.shape[0]
    tiles_per_seq = seq // TM
    batch = tokens // seq
    row_tile = lambda width: pl.BlockSpec((TM, width), lambda i: (i, 0))
    return pl.pallas_call(
        _mix_in_kernel,
        out_shape=(
            jax.ShapeDtypeStruct((tokens, ATTN_WIDTH), BF16),
            jax.ShapeDtypeStruct((tokens, KV_WIDTH), F32),
            jax.ShapeDtypeStruct((tokens, KV_WIDTH), F32),
            jax.ShapeDtypeStruct((tokens, GM_WIDTH), BF16),
            jax.ShapeDtypeStruct((batch, CHUNK, GM_WIDTH), F32),
        ),
        grid=(tokens // TM,),
        in_specs=[
            row_tile(D_MODEL),
            pl.BlockSpec((None, 6, D_MODEL), lambda i: (i // tiles_per_seq, 0, 0)),
            _resident((1, D_MODEL)),
            _resident((D_MODEL, IN_COLS)),
            pl.BlockSpec((TM, V7X_LANES), lambda i: (i % tiles_per_seq, 0)),
            pl.BlockSpec((TM, V7X_LANES), lambda i: (i % tiles_per_seq, 0)),
            _resident((1, GM_WIDTH)),
            _resident((GM_HEADS, CHUNK, CHUNK)),
            _resident((CHUNK, GM_HEADS)),
        ],
        out_specs=(
            row_tile(ATTN_WIDTH), row_tile(KV_WIDTH), row_tile(KV_WIDTH), row_tile(GM_WIDTH),
            pl.BlockSpec((None, CHUNK, GM_WIDTH), lambda i: (i // tiles_per_seq, 0, 0)),
        ),
        compiler_params=_params("arbitrary"),
        name="mix_in",
    )(x, mod, g1, w_in, cos, sin_signed, gain, ws, bst)


def _attn_kernel(sink_ref, q_ref, kp_ref, kc_ref, vp_ref, vc_ref, o_ref):
    n = pl.program_id(1)
    row = lax.broadcasted_iota(jnp.int32, (WINDOW, 2 * WINDOW), 0)
    key = lax.broadcasted_iota(jnp.int32, (WINDOW, 2 * WINDOW), 1) - WINDOW
    valid = (key <= row) & (key >= jnp.maximum(row - WINDOW, -n * WINDOW))
    lane = lax.broadcasted_iota(jnp.int32, (WINDOW, V7X_LANES), 1)
    low = lane < HEAD_DIM
    cols_per_pair = ATTN_WIDTH // V7X_LANES // 2
    for p in range(2):
        ls = slice(p * V7X_LANES, (p + 1) * V7X_LANES)
        kcat = jnp.concatenate([kp_ref[:, ls], kc_ref[:, ls]], axis=0).astype(BF16)
        vcat = jnp.concatenate([vp_ref[:, ls], vc_ref[:, ls]], axis=0).astype(BF16)
        blocks = []
        for cc in range(cols_per_pair):
            c = p * cols_per_pair + cc
            qcol = q_ref[:, c * V7X_LANES:(c + 1) * V7X_LANES]
            zero = jnp.zeros_like(qcol)
            blocks += [jnp.where(low, qcol, zero), jnp.where(low, zero, qcol)]
        s = _dot_nt(jnp.concatenate(blocks, axis=0), kcat)
        probs, inv = [], []
        for j in range(2 * cols_per_pair):
            sj = jnp.where(valid, s[j * WINDOW:(j + 1) * WINDOW], NEG)
            sk = sink_ref[p * 2 * cols_per_pair + j]
            m = jnp.maximum(jnp.max(sj, axis=-1, keepdims=True), sk)
            pj = jnp.exp(sj - m)
            inv.append(1.0 / (jnp.sum(pj, axis=-1, keepdims=True) + jnp.exp(sk - m)))
            probs.append(pj.astype(BF16))
        o = _dot(jnp.concatenate(probs, axis=0), vcat)
        for cc in range(cols_per_pair):
            c = p * cols_per_pair + cc
            even = o[(2 * cc) * WINDOW:(2 * cc + 1) * WINDOW] * inv[2 * cc]
            odd = o[(2 * cc + 1) * WINDOW:(2 * cc + 2) * WINDOW] * inv[2 * cc + 1]
            o_ref[:, c * V7X_LANES:(c + 1) * V7X_LANES] = jnp.where(low, even, odd).astype(BF16)


def _attention(sinks, q, k, v, seq):
    tokens = q.shape[0]
    nb = seq // WINDOW
    cur = lambda b, n: (b * nb + n, 0)
    prev = lambda b, n: (jnp.maximum(b * nb + n - 1, 0), 0)
    return pl.pallas_call(
        _attn_kernel,
        out_shape=jax.ShapeDtypeStruct((tokens, ATTN_WIDTH), BF16),
        grid=(tokens // seq, nb),
        in_specs=[
            pl.BlockSpec(memory_space=pltpu.SMEM),
            pl.BlockSpec((WINDOW, ATTN_WIDTH), cur),
            pl.BlockSpec((WINDOW, KV_WIDTH), prev),
            pl.BlockSpec((WINDOW, KV_WIDTH), cur),
            pl.BlockSpec((WINDOW, KV_WIDTH), prev),
            pl.BlockSpec((WINDOW, KV_WIDTH), cur),
        ],
        out_specs=pl.BlockSpec((WINDOW, ATTN_WIDTH), cur),
        compiler_params=_params("arbitrary", "arbitrary"),
        name="swa_attention",
    )(sinks, q, k, k, v, v)


def _mix_out_kernel(x_ref, mod_ref, a_ref, gm_ref, wa_ref, wb_ref, o_ref):
    mix = _dot(a_ref[...], wa_ref[...]) + _dot(gm_ref[...], wb_ref[...])
    o_ref[...] = x_ref[...] + mod_ref[2:3] * mix


def _mix_out(x, mod, attn, gm, wa, wb, seq):
    tokens = x.shape[0]
    tiles_per_seq = seq // TM
    row_tile = lambda width: pl.BlockSpec((TM, width), lambda i: (i, 0))
    return pl.pallas_call(
        _mix_out_kernel,
        out_shape=jax.ShapeDtypeStruct((tokens, D_MODEL), F32),
        grid=(tokens // TM,),
        in_specs=[
            row_tile(D_MODEL),
            pl.BlockSpec((None, 6, D_MODEL), lambda i: (i // tiles_per_seq, 0, 0)),
            row_tile(ATTN_WIDTH), row_tile(GM_WIDTH),
            _resident((ATTN_WIDTH, D_MODEL)), _resident((GM_WIDTH, D_MODEL)),
        ],
        out_specs=row_tile(D_MODEL),
        compiler_params=_params("arbitrary"),
        name="mix_out",
    )(x, mod, attn, gm, wa, wb)


def _ffn_kernel(x_ref, mod_ref, g2_ref, wg_ref, wu_ref, wd_ref, cw_ref, cb_ref, gf_ref,
                o_ref, tail_ref, h2_ref, acc_ref, ext_ref, carry_ref, *, tiles_per_seq, final_norm):
    i = pl.program_id(0)
    f = pl.program_id(1)
    tm = x_ref.shape[0]

    @pl.when(f == 0)
    def _():
        h2_ref[...] = (_rms(x_ref[...], g2_ref[...]) * (1.0 + mod_ref[4:5]) + mod_ref[3:4]).astype(BF16)
        acc_ref[...] = jnp.zeros_like(acc_ref)

    h2 = h2_ref[...]
    a = _dot(h2, wg_ref[...])
    up = _dot(h2, wu_ref[...])
    first = (i % tiles_per_seq) == 0

    @pl.when(first)
    def _():
        ext_ref[0:V7X_SUBLANES] = jnp.zeros((V7X_SUBLANES, ext_ref.shape[1]), F32)

    @pl.when(jnp.logical_not(first))
    def _():
        ext_ref[0:V7X_SUBLANES] = carry_ref[f]

    ext_ref[V7X_SUBLANES:] = a
    conv = (cb_ref[...] + cw_ref[0:1] * ext_ref[V7X_SUBLANES - 2:V7X_SUBLANES - 2 + tm]
            + cw_ref[1:2] * ext_ref[V7X_SUBLANES - 1:V7X_SUBLANES - 1 + tm] + cw_ref[2:3] * a)
    y = (jax.nn.silu(conv) * up).astype(BF16)
    acc_ref[...] += _dot(y, wd_ref[...])
    tail = ext_ref[tm:tm + V7X_SUBLANES]
    carry_ref[f] = tail
    tail_ref[...] = tail

    @pl.when(f == pl.num_programs(1) - 1)
    def _():
        x2 = x_ref[...] + mod_ref[5:6] * acc_ref[...]
        o_ref[...] = _rms(x2, gf_ref[...]) if final_norm else x2


def _ffn(x, mod, g2, wg, wu, wd, cw, cb, g_final, seq, final_norm):
    tokens = x.shape[0]
    tiles_per_seq = seq // TM
    nf = D_FF // TF
    kern = functools.partial(_ffn_kernel, tiles_per_seq=tiles_per_seq, final_norm=final_norm)
    return pl.pallas_call(
        kern,
        out_shape=(
            jax.ShapeDtypeStruct((tokens, D_MODEL), F32),
            jax.ShapeDtypeStruct((tokens // TM, V7X_SUBLANES, D_FF), F32),
        ),
        grid=(tokens // TM, nf),
        in_specs=[
            pl.BlockSpec((TM, D_MODEL), lambda i, f: (i, 0)),
            pl.BlockSpec((None, 6, D_MODEL), lambda i, f: (i // tiles_per_seq, 0, 0)),
            pl.BlockSpec((1, D_MODEL), lambda i, f: (0, 0)),
            pl.BlockSpec((D_MODEL, TF), lambda i, f: (0, f)),
            pl.BlockSpec((D_MODEL, TF), lambda i, f: (0, f)),
            pl.BlockSpec((TF, D_MODEL), lambda i, f: (f, 0)),
            pl.BlockSpec((3, TF), lambda i, f: (0, f)),
            pl.BlockSpec((1, TF), lambda i, f: (0, f)),
            pl.BlockSpec((1, D_MODEL), lambda i, f: (0, 0)),
        ],
        out_specs=(
            pl.BlockSpec((TM, D_MODEL), lambda i, f: (i, 0)),
            pl.BlockSpec((None, V7X_SUBLANES, TF), lambda i, f: (i, 0, f)),
        ),
        scratch_shapes=[
            pltpu.VMEM((TM, D_MODEL), BF16),
            pltpu.VMEM((TM, D_MODEL), F32),
            pltpu.VMEM((TM + V7X_SUBLANES, TF), F32),
            pltpu.VMEM((nf, V7X_SUBLANES, TF), F32),
        ],
        compiler_params=_params("arbitrary", "arbitrary"),
        name="conv_ffn",
    )(x, mod, g2, wg, wu, wd, cw, cb, g_final)


def _s_mix_in_kernel(x_ref, mod_ref, g1_ref, w_ref, cos_ref, sin_ref, gain_ref, ws0_ref, bs0_ref,
                     q_ref, k_ref, v_ref, gm_ref, gvn_ref):
    q_blocks, k_blocks, v, u, gvn = _mix_in_core(
        x_ref[...], mod_ref[0], mod_ref[1], g1_ref[...], w_ref, cos_ref[...], sin_ref[...], gain_ref[...])
    _store_cols(q_ref, q_blocks)
    _store_cols(k_ref, k_blocks)
    v_ref[...] = v
    gvn_ref[...] = gvn
    gm_ref[...] = (u * (ws0_ref[...].astype(BF16).astype(F32) * gvn.astype(BF16).astype(F32) + bs0_ref[...])).astype(BF16)


def _s_mix_in(x, mod, g1, w_in, cos, sin_signed, gain, ws0, bs0):
    rows = x.shape[0]
    return pl.pallas_call(
        _s_mix_in_kernel,
        out_shape=(
            jax.ShapeDtypeStruct((rows, ATTN_WIDTH), BF16),
            jax.ShapeDtypeStruct((rows, KV_WIDTH), F32),
            jax.ShapeDtypeStruct((rows, KV_WIDTH), F32),
            jax.ShapeDtypeStruct((rows, GM_WIDTH), BF16),
            jax.ShapeDtypeStruct((rows, GM_WIDTH), F32),
        ),
        compiler_params=_params(),
        name="sample_mix_in",
    )(x, mod, g1, w_in, cos, sin_signed, gain, ws0, bs0)


def _s_attn_kernel(q_ref, kn_ref, vn_ref, ck_ref, cv_ref, sink_ref, o_ref):
    rows = q_ref.shape[0]
    keys = ck_ref.shape[0]
    nseq = kn_ref.shape[0]
    heads_log2 = 4
    window_log2 = 7
    hd_log2 = 6

    def iota(shape, d):
        return lax.broadcasted_iota(jnp.int32, shape, d)

    spread = jnp.where(jnp.bitwise_and(iota((HEAD_DIM, KV_WIDTH), 1), HEAD_DIM - 1) == iota((HEAD_DIM, KV_WIDTH), 0),
                       1.0, 0.0).astype(BF16)
    qx = _dot(q_ref[...], spread)
    slot = jnp.bitwise_and(iota((rows, KV_WIDTH), 0), N_HEADS - 1)
    group = 2 * lax.shift_right_logical(slot, 3) + jnp.bitwise_and(slot, 1)
    own = lax.shift_right_logical(iota((rows, KV_WIDTH), 1), hd_log2) == group
    qpad = jnp.where(own, qx, 0.0).astype(BF16)

    s = _dot_nt(qpad, ck_ref[...].astype(BF16))
    same = (lax.shift_right_logical(iota((rows, keys), 0), heads_log2)
            == lax.shift_right_logical(iota((rows, keys), 1), window_log2))
    s = jnp.where(same, s, NEG)
    sn = _dot_nt(qpad, kn_ref[...].astype(BF16))
    sn = jnp.where(lax.shift_right_logical(iota((rows, nseq), 0), heads_log2) == iota((rows, nseq), 1), sn, NEG)
    sk = sink_ref[...]
    m = jnp.maximum(jnp.maximum(jnp.max(s, axis=-1, keepdims=True), jnp.max(sn, axis=-1, keepdims=True)), sk)
    p = jnp.exp(s - m)
    pn = jnp.exp(sn - m)
    denom = jnp.sum(p, axis=-1, keepdims=True) + jnp.sum(pn, axis=-1, keepdims=True) + jnp.exp(sk - m)
    o = _dot(p.astype(BF16), cv_ref[...].astype(BF16)) + _dot(pn.astype(BF16), vn_ref[...].astype(BF16))
    o = jnp.where(own, o / denom, 0.0).astype(BF16)
    gather = jnp.where(jnp.bitwise_and(iota((KV_WIDTH, HEAD_DIM), 0), HEAD_DIM - 1) == iota((KV_WIDTH, HEAD_DIM), 1),
                       1.0, 0.0).astype(BF16)
    o_ref[...] = _dot(o, gather).astype(BF16)


def _s_attention(q_rows, k_new, v_new, cache_k, cache_v, sink_rows):
    return pl.pallas_call(
        _s_attn_kernel,
        out_shape=jax.ShapeDtypeStruct(q_rows.shape, BF16),
        compiler_params=_params(),
        name="sample_attention",
    )(q_rows, k_new, v_new, cache_k, cache_v, sink_rows)


def _s_mix_out_kernel(x_ref, gate_ref, a_ref, gm_ref, wa_ref, wb_ref, o_ref):
    mix = _dot(a_ref[...], wa_ref[...]) + _dot(gm_ref[...], wb_ref[...])
    o_ref[...] = x_ref[...] + gate_ref[...] * mix


def _s_mix_out(x, gate, attn, gm, wa, wb):
    return pl.pallas_call(
        _s_mix_out_kernel,
        out_shape=jax.ShapeDtypeStruct(x.shape, F32),
        compiler_params=_params(),
        name="sample_mix_out",
    )(x, gate, attn, gm, wa, wb)


def _s_ffn_kernel(x_ref, mod_ref, g2_ref, wg_ref, wu_ref, wd_ref, cw_ref, cb_ref, p0_ref, p1_ref, gf_ref,
                  o_ref, a_ref, h2_ref, acc_ref, *, final_norm):
    f = pl.program_id(0)

    @pl.when(f == 0)
    def _():
        h2_ref[...] = (_rms(x_ref[...], g2_ref[...]) * (1.0 + mod_ref[4]) + mod_ref[3]).astype(BF16)
        acc_ref[...] = jnp.zeros_like(acc_ref)

    h2 = h2_ref[...]
    a = _dot(h2, wg_ref[...])
    up = _dot(h2, wu_ref[...])
    a_ref[...] = a
    conv = cb_ref[...] + cw_ref[0:1] * p0_ref[...] + cw_ref[1:2] * p1_ref[...] + cw_ref[2:3] * a
    acc_ref[...] += _dot((jax.nn.silu(conv) * up).astype(BF16), wd_ref[...])

    @pl.when(f == pl.num_programs(0) - 1)
    def _():
        x2 = x_ref[...] + mod_ref[5] * acc_ref[...]
        o_ref[...] = _rms(x2, gf_ref[...]) if final_norm else x2


def _s_ffn(x, mod, g2, wg, wu, wd, cw, cb, p0, p1, g_final, final_norm):
    rows = x.shape[0]
    nf = D_FF // TF
    full = lambda shape: pl.BlockSpec(shape, lambda f: (0,) * len(shape))
    return pl.pallas_call(
        functools.partial(_s_ffn_kernel, final_norm=final_norm),
        out_shape=(jax.ShapeDtypeStruct((rows, D_MODEL), F32), jax.ShapeDtypeStruct((rows, D_FF), F32)),
        grid=(nf,),
        in_specs=[
            full((rows, D_MODEL)), full((6, rows, D_MODEL)), full((1, D_MODEL)),
            pl.BlockSpec((D_MODEL, TF), lambda f: (0, f)),
            pl.BlockSpec((D_MODEL, TF), lambda f: (0, f)),
            pl.BlockSpec((TF, D_MODEL), lambda f: (f, 0)),
            pl.BlockSpec((3, TF), lambda f: (0, f)),
            pl.BlockSpec((1, TF), lambda f: (0, f)),
            pl.BlockSpec((rows, TF), lambda f: (0, f)),
            pl.BlockSpec((rows, TF), lambda f: (0, f)),
            full((1, D_MODEL)),
        ],
        out_specs=(full((rows, D_MODEL)), pl.BlockSpec((rows, TF), lambda f: (0, f))),
        scratch_shapes=[pltpu.VMEM((rows, D_MODEL), BF16), pltpu.VMEM((rows, D_MODEL), F32)],
        compiler_params=_params("arbitrary"),
        name="sample_conv_ffn",
    )(x, mod, g2, wg, wu, wd, cw, cb, p0, p1, g_final)


def _pair_heads(a, axis, width=HEAD_DIM):
    shape = a.shape
    a = a.reshape(shape[:axis] + (2, 2, 4, width) + shape[axis + 1:])
    a = jnp.swapaxes(a, axis + 1, axis + 2)
    return a.reshape(shape)


def _rope_tables(pos):
    half = HEAD_DIM // 2
    inv = ROPE_THETA ** (-jnp.arange(half, dtype=jnp.float32) / half)
    ang = pos.astype(jnp.float32)[:, None] * inv[None, :]
    reps = V7X_LANES // half
    sign = jnp.where((jnp.arange(V7X_LANES) % HEAD_DIM) < half, -1.0, 1.0).astype(F32)
    return jnp.tile(jnp.cos(ang), (1, reps)), jnp.tile(jnp.sin(ang), (1, reps)) * sign[None, :]


def kernel(x_prompt, x_sample, cache_k, cache_v, state_conv, c_prompt, c_sample, w_ada, b_ada, g_norm1, g_norm2,
           w_in, gm_gain, gm_ws, gm_bs, sinks, w_out, w_gate, w_up, conv_w, conv_b, w_down, g_final):
    batch, seq, _ = x_prompt.shape
    dec = x_sample.shape[0]
    assert x_sample.shape[1] == 1 and seq % TM == 0 and TM % CHUNK == 0 and D_FF % TF == 0

    w_in_b = jnp.concatenate([_pair_heads(w_in[:, :, Q0:K0], 2), w_in[:, :, K0:]], axis=2).astype(BF16)
    wo_a = _pair_heads(w_out[:, :ATTN_WIDTH], 1).astype(BF16)
    wo_b = w_out[:, ATTN_WIDTH:].astype(BF16)
    wg_b, wu_b, wd_b = w_gate.astype(BF16), w_up.astype(BF16), w_down.astype(BF16)
    sinks_p = _pair_heads(sinks, 1, width=1)
    bst = jnp.swapaxes(gm_bs, 1, 2)
    ws0 = jnp.repeat(gm_ws[:, :, 0, 0], GM_WIDTH // GM_HEADS, axis=1)[:, None, :]
    bs0 = jnp.repeat(gm_bs[:, :, 0], GM_WIDTH // GM_HEADS, axis=1)[:, None, :]

    cos_p, sin_p = _rope_tables(jnp.arange(seq, dtype=jnp.int32))
    cos_s, sin_s = _rope_tables(PAST_LEN + jnp.arange(1, dtype=jnp.int32))

    c_all = jnp.concatenate([c_prompt, c_sample, jnp.zeros((MOD_ROWS - batch - dec, D_MODEL), F32)], axis=0)
    mod = _modulation(c_all, w_ada, b_ada)

    hp = x_prompt.reshape(batch * seq, D_MODEL)
    hs = x_sample.reshape(dec, D_MODEL)
    g_fin = g_final.reshape(1, D_MODEL)
    outs = [[] for _ in range(8)]
    for l in range(DEPTH):
        last = l == DEPTH - 1
        g1 = g_norm1[l].reshape(1, D_MODEL)
        g2 = g_norm2[l].reshape(1, D_MODEL)
        gain = gm_gain[l].reshape(1, GM_WIDTH)
        cw, cb = conv_w[l], conv_b[l].reshape(1, D_FF)
        mod_p = jnp.swapaxes(mod[l, :, :batch], 0, 1)
        mod_s = mod[l, :, batch:batch + dec]

        q, k, v, gm, gvl = _mix_in(hp, mod_p, g1, w_in_b[l], cos_p, sin_p, gain, gm_ws[l], bst[l], seq)
        attn = _attention(sinks_p[l], q, k, v, seq)
        hp = _mix_out(hp, mod_p, attn, gm, wo_a[l], wo_b[l], seq)
        hp, tail = _ffn(hp, mod_p, g2, wg_b[l], wu_b[l], wd_b[l], cw, cb, g_fin, seq, last)
        outs[0].append(k.reshape(batch, seq, N_KV_HEADS, HEAD_DIM)[:, seq - WINDOW:])
        outs[1].append(v.reshape(batch, seq, N_KV_HEADS, HEAD_DIM)[:, seq - WINDOW:])
        outs[2].append(gvl)
        outs[3].append(tail[seq // TM - 1::seq // TM, V7X_SUBLANES - 2:])

        qs, ks, vs, gms, gvs = _s_mix_in(hs, mod_s, g1, w_in_b[l], cos_s, sin_s, gain, ws0[l], bs0[l])
        ck = cache_k[l].reshape(dec * WINDOW, KV_WIDTH)
        cv = cache_v[l].reshape(dec * WINDOW, KV_WIDTH)
        sink_rows = jnp.tile(sinks_p[l], dec)[:, None]
        attn_s = _s_attention(qs.reshape(dec * N_HEADS, HEAD_DIM), ks, vs, ck, cv, sink_rows)
        hs = _s_mix_out(hs, mod_s[2], attn_s.reshape(dec, ATTN_WIDTH), gms, wo_a[l], wo_b[l])
        hs, a_new = _s_ffn(hs, mod_s, g2, wg_b[l], wu_b[l], wd_b[l], cw, cb,
                           state_conv[l, :, 0], state_conv[l, :, 1], g_fin, last)
        outs[4].append(jnp.concatenate([cache_k[l, :, 1:], ks.reshape(dec, 1, N_KV_HEADS, HEAD_DIM)], axis=1))
        outs[5].append(jnp.concatenate([cache_v[l, :, 1:], vs.reshape(dec, 1, N_KV_HEADS, HEAD_DIM)], axis=1))
        outs[6].append(gvs[:, None, :])
        outs[7].append(jnp.stack([state_conv[l, :, 1], a_new], axis=1))

    return (hp.reshape(batch, seq, D_MODEL), hs.reshape(dec, 1, D_MODEL)) + tuple(jnp.stack(o) for o in outs)
```

```python
import functools

import jax
import jax.numpy as jnp
from jax import lax
from jax.experimental import pallas as pl
from jax.experimental.pallas import tpu as pltpu

D_MODEL = 2048
DEPTH = 2
PAST_LEN = 16384
ATTN_WIDTH = 1024
GM_WIDTH = 1024
HEAD_DIM = 64
N_HEADS = 16
N_KV_HEADS = 4
WINDOW = 128
ROPE_THETA = 10000.0
CHUNK = 128
GM_HEADS = 8
D_FF = 5632
KV_WIDTH = N_KV_HEADS * HEAD_DIM
IN_COLS = ATTN_WIDTH + 2 * KV_WIDTH + 2 * GM_WIDTH
EPS = 1e-6
NEG = -1e30

Q0, K0, V0, U0, G0 = 0, ATTN_WIDTH, ATTN_WIDTH + KV_WIDTH, ATTN_WIDTH + 2 * KV_WIDTH, ATTN_WIDTH + 2 * KV_WIDTH + GM_WIDTH

V7X_LANES = 128
V7X_SUBLANES = 8
V7X_VMEM_LIMIT_BYTES = 56 * 1024 * 1024

TM = 512
TF = 512
MOD_TN = 1024
FFN_SUBBLOCKS = 2
MIX_OUT_SUBBLOCKS = 2
MOD_ROWS = 48

BF16 = jnp.bfloat16
F32 = jnp.float32


def _params(*semantics):
    return pltpu.CompilerParams(dimension_semantics=semantics, vmem_limit_bytes=V7X_VMEM_LIMIT_BYTES)


def _dot(a, b):
    return jnp.dot(a, b, preferred_element_type=F32)


def _dot_nt(a, b):
    return lax.dot_general(a, b, (((1,), (1,)), ((), ())), preferred_element_type=F32)


def _rms(x, g):
    return x * lax.rsqrt(jnp.mean(x * x, axis=-1, keepdims=True) + EPS) * g


def _resident(shape):
    zeros = (0,) * len(shape)
    return pl.BlockSpec(shape, lambda *_: zeros, pipeline_mode=pl.Buffered(1))


def _mod_kernel(c_ref, w_ref, b_ref, o_ref):
    s = jax.nn.silu(c_ref[...]).astype(BF16)
    o_ref[...] = _dot(s, w_ref[...].astype(BF16)) + b_ref[...]


def _modulation(c_all, w_ada, b_ada):
    n_tiles = (6 * D_MODEL) // MOD_TN
    per_chunk = D_MODEL // MOD_TN
    return pl.pallas_call(
        _mod_kernel,
        out_shape=jax.ShapeDtypeStruct((DEPTH, 6, MOD_ROWS, D_MODEL), F32),
        grid=(DEPTH, n_tiles),
        in_specs=[
            pl.BlockSpec((MOD_ROWS, D_MODEL), lambda l, j: (0, 0)),
            pl.BlockSpec((None, D_MODEL, MOD_TN), lambda l, j: (l, 0, j)),
            pl.BlockSpec((None, 1, MOD_TN), lambda l, j: (l, 0, j)),
        ],
        out_specs=pl.BlockSpec((None, None, MOD_ROWS, MOD_TN), lambda l, j: (l, j // per_chunk, 0, j % per_chunk)),
        compiler_params=_params("arbitrary", "arbitrary"),
        name="adaln_modulation",
    )(c_all, w_ada, b_ada.reshape(DEPTH, 1, 6 * D_MODEL))


def _rope_cols(z, cos, sin_signed):
    lane = lax.broadcasted_iota(jnp.int32, (1, V7X_LANES), 1)
    first_half = jnp.bitwise_and(lane, HEAD_DIM - 1) < HEAD_DIM // 2
    outs = []
    for c in range(z.shape[1] // V7X_LANES):
        zc = z[:, c * V7X_LANES:(c + 1) * V7X_LANES]
        partner = jnp.where(first_half,
                            pltpu.roll(zc, V7X_LANES - HEAD_DIM // 2, axis=1),
                            pltpu.roll(zc, HEAD_DIM // 2, axis=1))
        outs.append(zc * cos + partner * sin_signed)
    return outs


def _mix_in_core(x, sh, sc, g1, w_ref, cos, sin_signed, gain):
    h = (_rms(x, g1) * (1.0 + sc) + sh).astype(BF16)
    q_blocks = _rope_cols(_dot(h, w_ref[:, Q0:K0]), cos, sin_signed)
    q_blocks = [(qb * (HEAD_DIM ** -0.5)).astype(BF16) for qb in q_blocks]
    k_blocks = _rope_cols(_dot(h, w_ref[:, K0:V0]), cos, sin_signed)
    v = _dot(h, w_ref[:, V0:U0])
    u = jax.nn.gelu(_dot(h, w_ref[:, U0:G0]))
    gvn = _rms(jax.nn.gelu(_dot(h, w_ref[:, G0:IN_COLS])), gain)
    return q_blocks, k_blocks, v, u, gvn


def _store_cols(ref, blocks):
    for c, blk in enumerate(blocks):
        ref[:, c * V7X_LANES:(c + 1) * V7X_LANES] = blk


def _mix_in_kernel(x_ref, mod_ref, g1_ref, w_ref, cos_ref, sin_ref, gain_ref, ws_ref, bst_ref,
                   q_ref, k_ref, v_ref, gm_ref, gvl_ref):
    tm = x_ref.shape[0]
    q_blocks, k_blocks, v, u, gvn = _mix_in_core(
        x_ref[...], mod_ref[0:1], mod_ref[1:2], g1_ref[...], w_ref, cos_ref[...], sin_ref[...], gain_ref[...])
    _store_cols(q_ref, q_blocks)
    _store_cols(k_ref, k_blocks)
    v_ref[...] = v
    gvl_ref[...] = gvn[tm - CHUNK:, :]
    gvb = gvn.astype(BF16)
    row = lax.broadcasted_iota(jnp.int32, (CHUNK, CHUNK), 0)
    col = lax.broadcasted_iota(jnp.int32, (CHUNK, CHUNK), 1)
    causal = col <= row
    for hh in range(GM_HEADS):
        wm = jnp.where(causal, ws_ref[hh], 0.0).astype(BF16)
        bias = bst_ref[:, hh:hh + 1]
        cs = slice(hh * CHUNK, (hh + 1) * CHUNK)
        for c in range(tm // CHUNK):
            rs = slice(c * CHUNK, (c + 1) * CHUNK)
            s = _dot(wm, gvb[rs, cs]) + bias
            gm_ref[rs, cs] = (u[rs, cs] * s).astype(BF16)


def _mix_in(x, mod, g1, w_in, cos, sin_signed, gain, ws, bst, seq):
    tokens = x.shape[0]
    tiles_per_seq = seq // TM
    batch = tokens // seq
    row_tile = lambda width: pl.BlockSpec((TM, width), lambda i: (i, 0))
    return pl.pallas_call(
        _mix_in_kernel,
        out_shape=(
            jax.ShapeDtypeStruct((tokens, ATTN_WIDTH), BF16),
            jax.ShapeDtypeStruct((tokens, KV_WIDTH), F32),
            jax.ShapeDtypeStruct((tokens, KV_WIDTH), F32),
            jax.ShapeDtypeStruct((tokens, GM_WIDTH), BF16),
            jax.ShapeDtypeStruct((batch, CHUNK, GM_WIDTH), F32),
        ),
        grid=(tokens // TM,),
        in_specs=[
            row_tile(D_MODEL),
            pl.BlockSpec((None, 6, D_MODEL), lambda i: (i // tiles_per_seq, 0, 0)),
            _resident((1, D_MODEL)),
            _resident((D_MODEL, IN_COLS)),
            pl.BlockSpec((TM, V7X_LANES), lambda i: (i % tiles_per_seq, 0)),
            pl.BlockSpec((TM, V7X_LANES), lambda i: (i % tiles_per_seq, 0)),
            _resident((1, GM_WIDTH)),
            _resident((GM_HEADS, CHUNK, CHUNK)),
            _resident((CHUNK, GM_HEADS)),
        ],
        out_specs=(
            row_tile(ATTN_WIDTH), row_tile(KV_WIDTH), row_tile(KV_WIDTH), row_tile(GM_WIDTH),
            pl.BlockSpec((None, CHUNK, GM_WIDTH), lambda i: (i // tiles_per_seq, 0, 0)),
        ),
        compiler_params=_params("arbitrary"),
        name="mix_in",
    )(x, mod, g1, w_in, cos, sin_signed, gain, ws, bst)


def _attn_kernel(sink_ref, q_ref, kp_ref, kc_ref, vp_ref, vc_ref, o_ref):
    n = pl.program_id(1)
    row = lax.broadcasted_iota(jnp.int32, (WINDOW, 2 * WINDOW), 0)
    key = lax.broadcasted_iota(jnp.int32, (WINDOW, 2 * WINDOW), 1) - WINDOW
    valid = (key <= row) & (key >= jnp.maximum(row - WINDOW, -n * WINDOW))
    lane = lax.broadcasted_iota(jnp.int32, (WINDOW, V7X_LANES), 1)
    low = lane < HEAD_DIM
    cols_per_pair = ATTN_WIDTH // V7X_LANES // 2
    for p in range(2):
        ls = slice(p * V7X_LANES, (p + 1) * V7X_LANES)
        kcat = jnp.concatenate([kp_ref[:, ls], kc_ref[:, ls]], axis=0).astype(BF16)
        vcat = jnp.concatenate([vp_ref[:, ls], vc_ref[:, ls]], axis=0).astype(BF16)
        blocks = []
        for cc in range(cols_per_pair):
            c = p * cols_per_pair + cc
            qcol = q_ref[:, c * V7X_LANES:(c + 1) * V7X_LANES]
            zero = jnp.zeros_like(qcol)
            blocks += [jnp.where(low, qcol, zero), jnp.where(low, zero, qcol)]
        s = _dot_nt(jnp.concatenate(blocks, axis=0), kcat)
        probs, inv = [], []
        for j in range(2 * cols_per_pair):
            sj = jnp.where(valid, s[j * WINDOW:(j + 1) * WINDOW], NEG)
            sk = sink_ref[p * 2 * cols_per_pair + j]
            m = jnp.maximum(jnp.max(sj, axis=-1, keepdims=True), sk)
            pj = jnp.exp(sj - m)
            inv.append(1.0 / (jnp.sum(pj, axis=-1, keepdims=True) + jnp.exp(sk - m)))
            probs.append(pj.astype(BF16))
        o = _dot(jnp.concatenate(probs, axis=0), vcat)
        for cc in range(cols_per_pair):
            c = p * cols_per_pair + cc
            even = o[(2 * cc) * WINDOW:(2 * cc + 1) * WINDOW] * inv[2 * cc]
            odd = o[(2 * cc + 1) * WINDOW:(2 * cc + 2) * WINDOW] * inv[2 * cc + 1]
            o_ref[:, c * V7X_LANES:(c + 1) * V7X_LANES] = jnp.where(low, even, odd).astype(BF16)


def _attention(sinks, q, k, v, seq):
    tokens = q.shape[0]
    nb = seq // WINDOW
    cur = lambda b, n: (b * nb + n, 0)
    prev = lambda b, n: (jnp.maximum(b * nb + n - 1, 0), 0)
    return pl.pallas_call(
        _attn_kernel,
        out_shape=jax.ShapeDtypeStruct((tokens, ATTN_WIDTH), BF16),
        grid=(tokens // seq, nb),
        in_specs=[
            pl.BlockSpec(memory_space=pltpu.SMEM),
            pl.BlockSpec((WINDOW, ATTN_WIDTH), cur),
            pl.BlockSpec((WINDOW, KV_WIDTH), prev),
            pl.BlockSpec((WINDOW, KV_WIDTH), cur),
            pl.BlockSpec((WINDOW, KV_WIDTH), prev),
            pl.BlockSpec((WINDOW, KV_WIDTH), cur),
        ],
        out_specs=pl.BlockSpec((WINDOW, ATTN_WIDTH), cur),
        compiler_params=_params("arbitrary", "arbitrary"),
        name="swa_attention",
    )(sinks, q, k, k, v, v)


def _mix_out_kernel(x_ref, mod_ref, g2_ref, a_ref, gm_ref, wa_ref, wb_ref, o_ref, h2_ref):
    sub = x_ref.shape[0] // MIX_OUT_SUBBLOCKS
    for r in range(MIX_OUT_SUBBLOCKS):
        rs = slice(r * sub, (r + 1) * sub)
        mix = _dot(a_ref[rs], wa_ref[...]) + _dot(gm_ref[rs], wb_ref[...])
        x1 = x_ref[rs] + mod_ref[2:3] * mix
        o_ref[rs] = x1
        h2_ref[rs] = (_rms(x1, g2_ref[...]) * (1.0 + mod_ref[4:5]) + mod_ref[3:4]).astype(BF16)


def _mix_out(x, mod, g2, attn, gm, wa, wb, seq):
    tokens = x.shape[0]
    tiles_per_seq = seq // TM
    row_tile = lambda width: pl.BlockSpec((TM, width), lambda i: (i, 0))
    return pl.pallas_call(
        _mix_out_kernel,
        out_shape=(jax.ShapeDtypeStruct((tokens, D_MODEL), F32), jax.ShapeDtypeStruct((tokens, D_MODEL), BF16)),
        grid=(tokens // TM,),
        in_specs=[
            row_tile(D_MODEL),
            pl.BlockSpec((None, 6, D_MODEL), lambda i: (i // tiles_per_seq, 0, 0)),
            _resident((1, D_MODEL)),
            row_tile(ATTN_WIDTH), row_tile(GM_WIDTH),
            _resident((ATTN_WIDTH, D_MODEL)), _resident((GM_WIDTH, D_MODEL)),
        ],
        out_specs=(row_tile(D_MODEL), row_tile(D_MODEL)),
        compiler_params=_params("arbitrary"),
        name="mix_out",
    )(x, mod, g2, attn, gm, wa, wb)


def _ffn_kernel(x_ref, h2_ref, mod_ref, wg_ref, wu_ref, wd_ref, cw_ref, cb_ref, gf_ref,
                o_ref, tail_ref, acc_ref, ext_ref, carry_ref, *, tiles_per_seq, final_norm):
    i = pl.program_id(0)
    f = pl.program_id(1)
    tm = x_ref.shape[0]

    @pl.when(f == 0)
    def _():
        acc_ref[...] = jnp.zeros_like(acc_ref)

    first = (i % tiles_per_seq) == 0

    @pl.when(first)
    def _():
        ext_ref[0:V7X_SUBLANES] = jnp.zeros((V7X_SUBLANES, ext_ref.shape[1]), F32)

    @pl.when(jnp.logical_not(first))
    def _():
        ext_ref[0:V7X_SUBLANES] = carry_ref[f]

    sub = tm // FFN_SUBBLOCKS
    ups = []
    for r in range(FFN_SUBBLOCKS):
        h2 = h2_ref[r * sub:(r + 1) * sub]
        ext_ref[V7X_SUBLANES + r * sub:V7X_SUBLANES + (r + 1) * sub] = _dot(h2, wg_ref[...])
        ups.append(_dot(h2, wu_ref[...]))
    for r in range(FFN_SUBBLOCKS):
        lo = V7X_SUBLANES + r * sub
        conv = (cb_ref[...] + cw_ref[0:1] * ext_ref[lo - 2:lo - 2 + sub]
                + cw_ref[1:2] * ext_ref[lo - 1:lo - 1 + sub] + cw_ref[2:3] * ext_ref[lo:lo + sub])
        y = (jax.nn.silu(conv) * ups[r]).astype(BF16)
        acc_ref[r * sub:(r + 1) * sub] += _dot(y, wd_ref[...])
    tail = ext_ref[tm:tm + V7X_SUBLANES]
    carry_ref[f] = tail
    tail_ref[...] = tail

    @pl.when(f == pl.num_programs(1) - 1)
    def _():
        x2 = x_ref[...] + mod_ref[5:6] * acc_ref[...]
        o_ref[...] = _rms(x2, gf_ref[...]) if final_norm else x2


def _ffn(x, h2, mod, wg, wu, wd, cw, cb, g_final, seq, final_norm):
    tokens = x.shape[0]
    tiles_per_seq = seq // TM
    nf = D_FF // TF
    kern = functools.partial(_ffn_kernel, tiles_per_seq=tiles_per_seq, final_norm=final_norm)
    return pl.pallas_call(
        kern,
        out_shape=(
            jax.ShapeDtypeStruct((tokens, D_MODEL), F32),
            jax.ShapeDtypeStruct((tokens // TM, V7X_SUBLANES, D_FF), F32),
        ),
        grid=(tokens // TM, nf),
        in_specs=[
            pl.BlockSpec((TM, D_MODEL), lambda i, f: (i, 0)),
            pl.BlockSpec((TM, D_MODEL), lambda i, f: (i, 0)),
            pl.BlockSpec((None, 6, D_MODEL), lambda i, f: (i // tiles_per_seq, 0, 0)),
            pl.BlockSpec((D_MODEL, TF), lambda i, f: (0, f)),
            pl.BlockSpec((D_MODEL, TF), lambda i, f: (0, f)),
            pl.BlockSpec((TF, D_MODEL), lambda i, f: (f, 0)),
            pl.BlockSpec((3, TF), lambda i, f: (0, f)),
            pl.BlockSpec((1, TF), lambda i, f: (0, f)),
            pl.BlockSpec((1, D_MODEL), lambda i, f: (0, 0)),
        ],
        out_specs=(
            pl.BlockSpec((TM, D_MODEL), lambda i, f: (i, 0)),
            pl.BlockSpec((None, V7X_SUBLANES, TF), lambda i, f: (i, 0, f)),
        ),
        scratch_shapes=[
            pltpu.VMEM((TM, D_MODEL), F32),
            pltpu.VMEM((TM + V7X_SUBLANES, TF), F32),
            pltpu.VMEM((nf, V7X_SUBLANES, TF), F32),
        ],
        compiler_params=_params("arbitrary", "arbitrary"),
        name="conv_ffn",
    )(x, h2, mod, wg, wu, wd, cw, cb, g_final)


def _s_mix_in_kernel(x_ref, mod_ref, g1_ref, w_ref, cos_ref, sin_ref, gain_ref, ws0_ref, bs0_ref,
                     q_ref, k_ref, v_ref, gm_ref, gvn_ref):
    q_blocks, k_blocks, v, u, gvn = _mix_in_core(
        x_ref[...], mod_ref[0], mod_ref[1], g1_ref[...], w_ref, cos_ref[...], sin_ref[...], gain_ref[...])
    _store_cols(q_ref, q_blocks)
    _store_cols(k_ref, k_blocks)
    v_ref[...] = v
    gvn_ref[...] = gvn
    gm_ref[...] = (u * (ws0_ref[...].astype(BF16).astype(F32) * gvn.astype(BF16).astype(F32) + bs0_ref[...])).astype(BF16)


def _s_mix_in(x, mod, g1, w_in, cos, sin_signed, gain, ws0, bs0):
    rows = x.shape[0]
    return pl.pallas_call(
        _s_mix_in_kernel,
        out_shape=(
            jax.ShapeDtypeStruct((rows, ATTN_WIDTH), BF16),
            jax.ShapeDtypeStruct((rows, KV_WIDTH), F32),
            jax.ShapeDtypeStruct((rows, KV_WIDTH), F32),
            jax.ShapeDtypeStruct((rows, GM_WIDTH), BF16),
            jax.ShapeDtypeStruct((rows, GM_WIDTH), F32),
        ),
        compiler_params=_params(),
        name="sample_mix_in",
    )(x, mod, g1, w_in, cos, sin_signed, gain, ws0, bs0)


def _s_attn_kernel(q_ref, kn_ref, vn_ref, ck_ref, cv_ref, sink_ref, o_ref):
    rows = q_ref.shape[0]
    keys = ck_ref.shape[0]
    nseq = kn_ref.shape[0]
    heads_log2 = 4
    window_log2 = 7
    hd_log2 = 6

    def iota(shape, d):
        return lax.broadcasted_iota(jnp.int32, shape, d)

    spread = jnp.where(jnp.bitwise_and(iota((HEAD_DIM, KV_WIDTH), 1), HEAD_DIM - 1) == iota((HEAD_DIM, KV_WIDTH), 0),
                       1.0, 0.0).astype(BF16)
    qx = _dot(q_ref[...], spread)
    slot = jnp.bitwise_and(iota((rows, KV_WIDTH), 0), N_HEADS - 1)
    group = 2 * lax.shift_right_logical(slot, 3) + jnp.bitwise_and(slot, 1)
    own = lax.shift_right_logical(iota((rows, KV_WIDTH), 1), hd_log2) == group
    qpad = jnp.where(own, qx, 0.0).astype(BF16)

    s = _dot_nt(qpad, ck_ref[...].astype(BF16))
    same = (lax.shift_right_logical(iota((rows, keys), 0), heads_log2)
            == lax.shift_right_logical(iota((rows, keys), 1), window_log2))
    s = jnp.where(same, s, NEG)
    sn = _dot_nt(qpad, kn_ref[...].astype(BF16))
    sn = jnp.where(lax.shift_right_logical(iota((rows, nseq), 0), heads_log2) == iota((rows, nseq), 1), sn, NEG)
    sk = sink_ref[...]
    m = jnp.maximum(jnp.maximum(jnp.max(s, axis=-1, keepdims=True), jnp.max(sn, axis=-1, keepdims=True)), sk)
    p = jnp.exp(s - m)
    pn = jnp.exp(sn - m)
    denom = jnp.sum(p, axis=-1, keepdims=True) + jnp.sum(pn, axis=-1, keepdims=True) + jnp.exp(sk - m)
    o = _dot(p.astype(BF16), cv_ref[...].astype(BF16)) + _dot(pn.astype(BF16), vn_ref[...].astype(BF16))
    o = jnp.where(own, o / denom, 0.0).astype(BF16)
    gather = jnp.where(jnp.bitwise_and(iota((KV_WIDTH, HEAD_DIM), 0), HEAD_DIM - 1) == iota((KV_WIDTH, HEAD_DIM), 1),
                       1.0, 0.0).astype(BF16)
    o_ref[...] = _dot(o, gather).astype(BF16)


def _s_attention(q_rows, k_new, v_new, cache_k, cache_v, sink_rows):
    return pl.pallas_call(
        _s_attn_kernel,
        out_shape=jax.ShapeDtypeStruct(q_rows.shape, BF16),
        compiler_params=_params(),
        name="sample_attention",
    )(q_rows, k_new, v_new, cache_k, cache_v, sink_rows)


def _s_mix_out_kernel(x_ref, gate_ref, a_ref, gm_ref, wa_ref, wb_ref, o_ref):
    mix = _dot(a_ref[...], wa_ref[...]) + _dot(gm_ref[...], wb_ref[...])
    o_ref[...] = x_ref[...] + gate_ref[...] * mix


def _s_mix_out(x, gate, attn, gm, wa, wb):
    return pl.pallas_call(
        _s_mix_out_kernel,
        out_shape=jax.ShapeDtypeStruct(x.shape, F32),
        compiler_params=_params(),
        name="sample_mix_out",
    )(x, gate, attn, gm, wa, wb)


def _s_ffn_kernel(x_ref, mod_ref, g2_ref, wg_ref, wu_ref, wd_ref, cw_ref, cb_ref, p0_ref, p1_ref, gf_ref,
                  o_ref, a_ref, h2_ref, acc_ref, *, final_norm):
    f = pl.program_id(0)

    @pl.when(f == 0)
    def _():
        h2_ref[...] = (_rms(x_ref[...], g2_ref[...]) * (1.0 + mod_ref[4]) + mod_ref[3]).astype(BF16)
        acc_ref[...] = jnp.zeros_like(acc_ref)

    h2 = h2_ref[...]
    a = _dot(h2, wg_ref[...])
    up = _dot(h2, wu_ref[...])
    a_ref[...] = a
    conv = cb_ref[...] + cw_ref[0:1] * p0_ref[...] + cw_ref[1:2] * p1_ref[...] + cw_ref[2:3] * a
    acc_ref[...] += _dot((jax.nn.silu(conv) * up).astype(BF16), wd_ref[...])

    @pl.when(f == pl.num_programs(0) - 1)
    def _():
        x2 = x_ref[...] + mod_ref[5] * acc_ref[...]
        o_ref[...] = _rms(x2, gf_ref[...]) if final_norm else x2


def _s_ffn(x, mod, g2, wg, wu, wd, cw, cb, p0, p1, g_final, final_norm):
    rows = x.shape[0]
    nf = D_FF // TF
    full = lambda shape: pl.BlockSpec(shape, lambda f: (0,) * len(shape))
    return pl.pallas_call(
        functools.partial(_s_ffn_kernel, final_norm=final_norm),
        out_shape=(jax.ShapeDtypeStruct((rows, D_MODEL), F32), jax.ShapeDtypeStruct((rows, D_FF), F32)),
        grid=(nf,),
        in_specs=[
            full((rows, D_MODEL)), full((6, rows, D_MODEL)), full((1, D_MODEL)),
            pl.BlockSpec((D_MODEL, TF), lambda f: (0, f)),
            pl.BlockSpec((D_MODEL, TF), lambda f: (0, f)),
            pl.BlockSpec((TF, D_MODEL), lambda f: (f, 0)),
            pl.BlockSpec((3, TF), lambda f: (0, f)),
            pl.BlockSpec((1, TF), lambda f: (0, f)),
            pl.BlockSpec((rows, TF), lambda f: (0, f)),
            pl.BlockSpec((rows, TF), lambda f: (0, f)),
            full((1, D_MODEL)),
        ],
        out_specs=(full((rows, D_MODEL)), pl.BlockSpec((rows, TF), lambda f: (0, f))),
        scratch_shapes=[pltpu.VMEM((rows, D_MODEL), BF16), pltpu.VMEM((rows, D_MODEL), F32)],
        compiler_params=_params("arbitrary"),
        name="sample_conv_ffn",
    )(x, mod, g2, wg, wu, wd, cw, cb, p0, p1, g_final)


def _pair_heads(a, axis, width=HEAD_DIM):
    shape = a.shape
    a = a.reshape(shape[:axis] + (2, 2, 4, width) + shape[axis + 1:])
    a = jnp.swapaxes(a, axis + 1, axis + 2)
    return a.reshape(shape)


def _rope_tables(pos):
    half = HEAD_DIM // 2
    inv = ROPE_THETA ** (-jnp.arange(half, dtype=jnp.float32) / half)
    ang = pos.astype(jnp.float32)[:, None] * inv[None, :]
    reps = V7X_LANES // half
    sign = jnp.where((jnp.arange(V7X_LANES) % HEAD_DIM) < half, -1.0, 1.0).astype(F32)
    return jnp.tile(jnp.cos(ang), (1, reps)), jnp.tile(jnp.sin(ang), (1, reps)) * sign[None, :]


def kernel(x_prompt, x_sample, cache_k, cache_v, state_conv, c_prompt, c_sample, w_ada, b_ada, g_norm1, g_norm2,
           w_in, gm_gain, gm_ws, gm_bs, sinks, w_out, w_gate, w_up, conv_w, conv_b, w_down, g_final):
    batch, seq, _ = x_prompt.shape
    dec = x_sample.shape[0]
    assert x_sample.shape[1] == 1 and seq % TM == 0 and TM % CHUNK == 0 and D_FF % TF == 0

    sinks_p = _pair_heads(sinks, 1, width=1)
    bst = jnp.swapaxes(gm_bs, 1, 2)
    ws0 = jnp.repeat(gm_ws[:, :, 0, 0], GM_WIDTH // GM_HEADS, axis=1)[:, None, :]
    bs0 = jnp.repeat(gm_bs[:, :, 0], GM_WIDTH // GM_HEADS, axis=1)[:, None, :]

    cos_p, sin_p = _rope_tables(jnp.arange(seq, dtype=jnp.int32))
    cos_s, sin_s = _rope_tables(PAST_LEN + jnp.arange(1, dtype=jnp.int32))

    c_all = jnp.concatenate([c_prompt, c_sample, jnp.zeros((MOD_ROWS - batch - dec, D_MODEL), F32)], axis=0)
    mod = _modulation(c_all, w_ada, b_ada)

    hp = x_prompt.reshape(batch * seq, D_MODEL)
    hs = x_sample.reshape(dec, D_MODEL)
    g_fin = g_final.reshape(1, D_MODEL)
    ck_all = cache_k.reshape(DEPTH, dec * WINDOW, KV_WIDTH)
    cv_all = cache_v.reshape(DEPTH, dec * WINDOW, KV_WIDTH)
    outs = [[] for _ in range(8)]
    for l in range(DEPTH):
        last = l == DEPTH - 1
        w_in_l = jnp.concatenate([_pair_heads(w_in[l][:, Q0:K0], 1), w_in[l][:, K0:]], axis=1).astype(BF16)
        wo_a = _pair_heads(w_out[l][:ATTN_WIDTH], 0).astype(BF16)
        wo_b = w_out[l][ATTN_WIDTH:].astype(BF16)
        wg_l, wu_l, wd_l = w_gate[l].astype(BF16), w_up[l].astype(BF16), w_down[l].astype(BF16)
        g1 = g_norm1[l].reshape(1, D_MODEL)
        g2 = g_norm2[l].reshape(1, D_MODEL)
        gain = gm_gain[l].reshape(1, GM_WIDTH)
        cw, cb = conv_w[l], conv_b[l].reshape(1, D_FF)
        mod_p = jnp.swapaxes(mod[l, :, :batch], 0, 1)
        mod_s = mod[l, :, batch:batch + dec]

        q, k, v, gm, gvl = _mix_in(hp, mod_p, g1, w_in_l, cos_p, sin_p, gain, gm_ws[l], bst[l], seq)
        attn = _attention(sinks_p[l], q, k, v, seq)
        hp, h2 = _mix_out(hp, mod_p, g2, attn, gm, wo_a, wo_b, seq)
        hp, tail = _ffn(hp, h2, mod_p, wg_l, wu_l, wd_l, cw, cb, g_fin, seq, last)
        window = lambda t: t.reshape(batch, seq, KV_WIDTH)[:, seq - WINDOW:].reshape(batch, WINDOW, N_KV_HEADS, HEAD_DIM)
        outs[0].append(window(k))
        outs[1].append(window(v))
        outs[2].append(gvl)
        outs[3].append(tail[seq // TM - 1::seq // TM, V7X_SUBLANES - 2:])

        qs, ks, vs, gms, gvs = _s_mix_in(hs, mod_s, g1, w_in_l, cos_s, sin_s, gain, ws0[l], bs0[l])
        sink_rows = jnp.tile(sinks_p[l], dec)[:, None]
        attn_s = _s_attention(qs.reshape(dec * N_HEADS, HEAD_DIM), ks, vs, ck_all[l], cv_all[l], sink_rows)
        hs = _s_mix_out(hs, mod_s[2], attn_s.reshape(dec, ATTN_WIDTH), gms, wo_a, wo_b)
        hs, a_new = _s_ffn(hs, mod_s, g2, wg_l, wu_l, wd_l, cw, cb,
                           state_conv[l, :, 0], state_conv[l, :, 1], g_fin, last)
        outs[4].append(ks)
        outs[5].append(vs)
        outs[6].append(gvs[:, None, :])
        outs[7].append(jnp.stack([state_conv[l, :, 1], a_new], axis=1))

    new_rows = lambda rows: jnp.stack(rows).reshape(DEPTH, dec, 1, N_KV_HEADS, HEAD_DIM)
    k_win_s = jnp.concatenate([cache_k[:, :, 1:], new_rows(outs[4])], axis=2)
    v_win_s = jnp.concatenate([cache_v[:, :, 1:], new_rows(outs[5])], axis=2)
    stacked = [jnp.stack(o) for o in outs]
    return (hp.reshape(batch, seq, D_MODEL), hs.reshape(dec, 1, D_MODEL), stacked[0], stacked[1], stacked[2],
            stacked[3], k_win_s, v_win_s, stacked[6], stacked[7])
```

```python
import functools

import jax
import jax.numpy as jnp
from jax import lax
from jax.experimental import pallas as pl
from jax.experimental.pallas import tpu as pltpu

D_MODEL = 2048
DEPTH = 2
PAST_LEN = 16384
ATTN_WIDTH = 1024
GM_WIDTH = 1024
HEAD_DIM = 64
N_HEADS = 16
N_KV_HEADS = 4
GQA_GROUP = N_HEADS // N_KV_HEADS
WINDOW = 128
ROPE_THETA = 10000.0
CHUNK = 128
GM_HEADS = 8
D_FF = 5632
KV_WIDTH = N_KV_HEADS * HEAD_DIM
IN_COLS = ATTN_WIDTH + 2 * KV_WIDTH + 2 * GM_WIDTH
EPS = 1e-6
NEG = -1e30

Q0, K0, V0, U0, G0 = 0, ATTN_WIDTH, ATTN_WIDTH + KV_WIDTH, ATTN_WIDTH + 2 * KV_WIDTH, ATTN_WIDTH + 2 * KV_WIDTH + GM_WIDTH

V7X_LANES = 128
V7X_SUBLANES = 8
V7X_VMEM_LIMIT_BYTES = 56 * 1024 * 1024

TM = 512
TF = 512
TQ = 512
TN_IN = 512
TN_OUT = 512
MOD_TN = 1024
MOD_ROWS = 48
FFN_SUBBLOCKS = 2
MIX_OUT_SUBBLOCKS = 2

BF16 = jnp.bfloat16
F32 = jnp.float32


def _params(*semantics):
    return pltpu.CompilerParams(dimension_semantics=semantics, vmem_limit_bytes=V7X_VMEM_LIMIT_BYTES)


def _dot(a, b):
    return jnp.dot(a, b, preferred_element_type=F32)


def _dot_nt(a, b):
    return lax.dot_general(a, b, (((1,), (1,)), ((), ())), preferred_element_type=F32)


def _rms(x, g):
    return x * lax.rsqrt(jnp.mean(x * x, axis=-1, keepdims=True) + EPS) * g


def _resident(shape):
    zeros = (0,) * len(shape)
    return pl.BlockSpec(shape, lambda *_: zeros, pipeline_mode=pl.Buffered(1))


def _iota(shape, dim):
    return lax.broadcasted_iota(jnp.int32, shape, dim)


def _mod_kernel(c_ref, w_ref, b_ref, o_ref):
    s = jax.nn.silu(c_ref[...]).astype(BF16)
    o_ref[...] = _dot(s, w_ref[...].astype(BF16)) + b_ref[...]


def _modulation(c_all, w_ada, b_ada):
    n_tiles = (6 * D_MODEL) // MOD_TN
    per_chunk = D_MODEL // MOD_TN
    return pl.pallas_call(
        _mod_kernel,
        out_shape=jax.ShapeDtypeStruct((DEPTH, 6, MOD_ROWS, D_MODEL), F32),
        grid=(DEPTH, n_tiles),
        in_specs=[
            pl.BlockSpec((MOD_ROWS, D_MODEL), lambda l, j: (0, 0)),
            pl.BlockSpec((None, D_MODEL, MOD_TN), lambda l, j: (l, 0, j)),
            pl.BlockSpec((None, 1, MOD_TN), lambda l, j: (l, 0, j)),
        ],
        out_specs=pl.BlockSpec((None, None, MOD_ROWS, MOD_TN), lambda l, j: (l, j // per_chunk, 0, j % per_chunk)),
        compiler_params=_params("arbitrary", "arbitrary"),
        name="adaln_modulation",
    )(c_all, w_ada, b_ada.reshape(DEPTH, 1, 6 * D_MODEL))


def _rope_cols(z, cos, sin_signed):
    lane = _iota((1, V7X_LANES), 1)
    first_half = jnp.bitwise_and(lane, HEAD_DIM - 1) < HEAD_DIM // 2
    outs = []
    for c in range(z.shape[1] // V7X_LANES):
        zc = z[:, c * V7X_LANES:(c + 1) * V7X_LANES]
        partner = jnp.where(first_half,
                            pltpu.roll(zc, V7X_LANES - HEAD_DIM // 2, axis=1),
                            pltpu.roll(zc, HEAD_DIM // 2, axis=1))
        outs.append(zc * cos + partner * sin_signed)
    return outs


def _swap_halves(x):
    return pltpu.roll(x, HEAD_DIM, axis=1)


def _pair_q_blocks(nat):
    low = _iota((1, V7X_LANES), 1) < HEAD_DIM
    out = []
    for c in range(len(nat)):
        p, j = divmod(c, 4)
        a, b = nat[p * 4 + j // 2], nat[p * 4 + 2 + j // 2]
        out.append(jnp.where(low, a, _swap_halves(b)) if j % 2 == 0 else jnp.where(low, _swap_halves(a), b))
    return out


def _post_project(zq, zk, zu, zg, cos, sin_signed, gain, pair_q):
    q_blocks = _rope_cols(zq, cos, sin_signed)
    if pair_q:
        q_blocks = _pair_q_blocks(q_blocks)
    q_blocks = [qb * (HEAD_DIM ** -0.5) for qb in q_blocks]
    k_blocks = _rope_cols(zk, cos, sin_signed)
    u = jax.nn.gelu(zu)
    gvn = _rms(jax.nn.gelu(zg), gain)
    return q_blocks, k_blocks, u, gvn


def _store_cols(ref, blocks):
    for c, blk in enumerate(blocks):
        ref[:, c * V7X_LANES:(c + 1) * V7X_LANES] = blk.astype(ref.dtype)


def _mix_in_kernel(x_ref, mod_ref, g1_ref, w_ref, cos_ref, sin_ref, gain_ref, ws_ref, bst_ref,
                   q_ref, k_ref, v_ref, gm_ref, gvl_ref):
    tm = x_ref.shape[0]
    h = (_rms(x_ref[...], g1_ref[...]) * (1.0 + mod_ref[1:2]) + mod_ref[0:1]).astype(BF16)
    q_blocks, k_blocks, u, gvn = _post_project(
        _dot(h, w_ref[:, Q0:K0]), _dot(h, w_ref[:, K0:V0]), _dot(h, w_ref[:, U0:G0]), _dot(h, w_ref[:, G0:IN_COLS]),
        cos_ref[...], sin_ref[...], gain_ref[...], pair_q=True)
    _store_cols(q_ref, q_blocks)
    _store_cols(k_ref, k_blocks)
    v_ref[...] = _dot(h, w_ref[:, V0:U0])
    gvl_ref[...] = gvn[tm - CHUNK:, :]
    gvb = gvn.astype(BF16)
    causal = _iota((CHUNK, CHUNK), 1) <= _iota((CHUNK, CHUNK), 0)
    for hh in range(GM_HEADS):
        wm = jnp.where(causal, ws_ref[hh], 0.0).astype(BF16)
        bias = bst_ref[:, hh:hh + 1]
        cs = slice(hh * CHUNK, (hh + 1) * CHUNK)
        for c in range(tm // CHUNK):
            rs = slice(c * CHUNK, (c + 1) * CHUNK)
            s = _dot(wm, gvb[rs, cs]) + bias
            gm_ref[rs, cs] = (u[rs, cs] * s).astype(BF16)


def _mix_in(x, mod, g1, w_in, cos, sin_signed, gain, ws, bst, seq):
    tokens = x.shape[0]
    tiles_per_seq = seq // TM
    batch = tokens // seq
    row_tile = lambda width: pl.BlockSpec((TM, width), lambda i: (i, 0))
    return pl.pallas_call(
        _mix_in_kernel,
        out_shape=(
            jax.ShapeDtypeStruct((tokens, ATTN_WIDTH), BF16),
            jax.ShapeDtypeStruct((tokens, KV_WIDTH), F32),
            jax.ShapeDtypeStruct((tokens, KV_WIDTH), F32),
            jax.ShapeDtypeStruct((tokens, GM_WIDTH), BF16),
            jax.ShapeDtypeStruct((batch, CHUNK, GM_WIDTH), F32),
        ),
        grid=(tokens // TM,),
        in_specs=[
            row_tile(D_MODEL),
            pl.BlockSpec((None, 6, D_MODEL), lambda i: (i // tiles_per_seq, 0, 0)),
            _resident((1, D_MODEL)),
            _resident((D_MODEL, IN_COLS)),
            pl.BlockSpec((TM, V7X_LANES), lambda i: (i % tiles_per_seq, 0)),
            pl.BlockSpec((TM, V7X_LANES), lambda i: (i % tiles_per_seq, 0)),
            _resident((1, GM_WIDTH)),
            _resident((GM_HEADS, CHUNK, CHUNK)),
            _resident((CHUNK, GM_HEADS)),
        ],
        out_specs=(
            row_tile(ATTN_WIDTH), row_tile(KV_WIDTH), row_tile(KV_WIDTH), row_tile(GM_WIDTH),
            pl.BlockSpec((None, CHUNK, GM_WIDTH), lambda i: (i // tiles_per_seq, 0, 0)),
        ),
        compiler_params=_params("arbitrary"),
        name="mix_in",
    )(x, mod, g1, w_in, cos, sin_signed, gain, ws, bst)


def _attn_block(n, sink_ref, q_blk, k_prev, k_cur, v_prev, v_cur, out_blk):
    row = _iota((WINDOW, 2 * WINDOW), 0)
    key = _iota((WINDOW, 2 * WINDOW), 1) - WINDOW
    valid = (key <= row) & (key >= jnp.maximum(row - WINDOW, -n * WINDOW))
    low = _iota((WINDOW, V7X_LANES), 1) < HEAD_DIM
    cols_per_pair = ATTN_WIDTH // V7X_LANES // 2
    for p in range(2):
        ls = slice(p * V7X_LANES, (p + 1) * V7X_LANES)
        kcat = jnp.concatenate([k_prev[:, ls], k_cur[:, ls]], axis=0).astype(BF16)
        vcat = jnp.concatenate([v_prev[:, ls], v_cur[:, ls]], axis=0).astype(BF16)
        blocks = []
        for cc in range(cols_per_pair):
            c = p * cols_per_pair + cc
            qcol = q_blk[:, c * V7X_LANES:(c + 1) * V7X_LANES]
            zero = jnp.zeros_like(qcol)
            blocks += [jnp.where(low, qcol, zero), jnp.where(low, zero, qcol)]
        s = _dot_nt(jnp.concatenate(blocks, axis=0), kcat)
        probs, inv = [], []
        for j in range(2 * cols_per_pair):
            cc, e = divmod(j, 2)
            sj = jnp.where(valid, s[j * WINDOW:(j + 1) * WINDOW], NEG)
            sk = sink_ref[p * 2 * GQA_GROUP + e * GQA_GROUP + cc]
            m = jnp.maximum(jnp.max(sj, axis=-1, keepdims=True), sk)
            pj = jnp.exp(sj - m)
            inv.append(1.0 / (jnp.sum(pj, axis=-1, keepdims=True) + jnp.exp(sk - m)))
            probs.append(pj.astype(BF16))
        o = _dot(jnp.concatenate(probs, axis=0), vcat)
        for cc in range(cols_per_pair):
            c = p * cols_per_pair + cc
            even = o[(2 * cc) * WINDOW:(2 * cc + 1) * WINDOW] * inv[2 * cc]
            odd = o[(2 * cc + 1) * WINDOW:(2 * cc + 2) * WINDOW] * inv[2 * cc + 1]
            out_blk[:, c * V7X_LANES:(c + 1) * V7X_LANES] = jnp.where(low, even, odd).astype(BF16)


def _attn_kernel(sink_ref, q_ref, kp_ref, kc_ref, vp_ref, vc_ref, o_ref, kt_ref, vt_ref):
    step = pl.program_id(1)
    blocks = q_ref.shape[0] // WINDOW
    for t in range(blocks):
        rows = slice(t * WINDOW, (t + 1) * WINDOW)
        prev = slice((t - 1) * WINDOW, t * WINDOW)
        k_prev = kp_ref[...] if t == 0 else kc_ref[prev]
        v_prev = vp_ref[...] if t == 0 else vc_ref[prev]
        _attn_block(step * blocks + t, sink_ref, q_ref.at[rows], k_prev, kc_ref[rows], v_prev, vc_ref[rows], o_ref.at[rows])

    @pl.when(step == pl.num_programs(1) - 1)
    def _():
        last = slice((blocks - 1) * WINDOW, blocks * WINDOW)
        kt_ref[...] = kc_ref[last].T
        vt_ref[...] = vc_ref[last].T


def _attention(sinks, q, k, v, seq):
    tokens = q.shape[0]
    batch = tokens // seq
    steps = seq // TQ
    per_step = TQ // WINDOW
    cur = lambda b, n: (b * steps + n, 0)
    prev = lambda b, n: (jnp.maximum((b * steps + n) * per_step - 1, 0), 0)
    state = pl.BlockSpec((None, KV_WIDTH, WINDOW), lambda b, n: (b, 0, 0))
    return pl.pallas_call(
        _attn_kernel,
        out_shape=(
            jax.ShapeDtypeStruct((tokens, ATTN_WIDTH), BF16),
            jax.ShapeDtypeStruct((batch, KV_WIDTH, WINDOW), F32),
            jax.ShapeDtypeStruct((batch, KV_WIDTH, WINDOW), F32),
        ),
        grid=(batch, steps),
        in_specs=[
            pl.BlockSpec(memory_space=pltpu.SMEM),
            pl.BlockSpec((TQ, ATTN_WIDTH), cur),
            pl.BlockSpec((WINDOW, KV_WIDTH), prev),
            pl.BlockSpec((TQ, KV_WIDTH), cur),
            pl.BlockSpec((WINDOW, KV_WIDTH), prev),
            pl.BlockSpec((TQ, KV_WIDTH), cur),
        ],
        out_specs=(pl.BlockSpec((TQ, ATTN_WIDTH), cur), state, state),
        compiler_params=_params("arbitrary", "arbitrary"),
        name="swa_attention",
    )(sinks, q, k, k, v, v)


def _mix_out_kernel(x_ref, mod_ref, g2_ref, a_ref, gm_ref, wa_ref, wb_ref, o_ref, h2_ref):
    sub = x_ref.shape[0] // MIX_OUT_SUBBLOCKS
    for r in range(MIX_OUT_SUBBLOCKS):
        rs = slice(r * sub, (r + 1) * sub)
        mix = _dot(a_ref[rs], wa_ref[...]) + _dot(gm_ref[rs], wb_ref[...])
        x1 = x_ref[rs] + mod_ref[2:3] * mix
        o_ref[rs] = x1
        h2_ref[rs] = (_rms(x1, g2_ref[...]) * (1.0 + mod_ref[4:5]) + mod_ref[3:4]).astype(BF16)


def _mix_out(x, mod, g2, attn, gm, w_out, seq):
    tokens = x.shape[0]
    tiles_per_seq = seq // TM
    row_tile = lambda width: pl.BlockSpec((TM, width), lambda i: (i, 0))
    half = lambda k: pl.BlockSpec((ATTN_WIDTH, D_MODEL), lambda i: (k, 0), pipeline_mode=pl.Buffered(1))
    return pl.pallas_call(
        _mix_out_kernel,
        out_shape=(jax.ShapeDtypeStruct((tokens, D_MODEL), F32), jax.ShapeDtypeStruct((tokens, D_MODEL), BF16)),
        grid=(tokens // TM,),
        in_specs=[
            row_tile(D_MODEL),
            pl.BlockSpec((None, 6, D_MODEL), lambda i: (i // tiles_per_seq, 0, 0)),
            _resident((1, D_MODEL)),
            row_tile(ATTN_WIDTH), row_tile(GM_WIDTH),
            half(0), half(1),
        ],
        out_specs=(row_tile(D_MODEL), row_tile(D_MODEL)),
        compiler_params=_params("arbitrary"),
        name="mix_out",
    )(x, mod, g2, attn, gm, w_out, w_out)


def _ffn_kernel(x_ref, h2_ref, mod_ref, wg_ref, wu_ref, wd_ref, cw_ref, cb_ref, gf_ref,
                o_ref, tail_ref, acc_ref, ext_ref, carry_ref, *, tiles_per_seq, final_norm):
    i = pl.program_id(0)
    f = pl.program_id(1)
    tm = x_ref.shape[0]

    @pl.when(f == 0)
    def _():
        acc_ref[...] = jnp.zeros_like(acc_ref)

    first = (i % tiles_per_seq) == 0

    @pl.when(first)
    def _():
        ext_ref[0:V7X_SUBLANES] = jnp.zeros((V7X_SUBLANES, ext_ref.shape[1]), F32)

    @pl.when(jnp.logical_not(first))
    def _():
        ext_ref[0:V7X_SUBLANES] = carry_ref[f]

    sub = tm // FFN_SUBBLOCKS
    cw = cw_ref[f]
    cb = cb_ref[f]
    ups = []
    for r in range(FFN_SUBBLOCKS):
        h2 = h2_ref[r * sub:(r + 1) * sub]
        ext_ref[V7X_SUBLANES + r * sub:V7X_SUBLANES + (r + 1) * sub] = _dot(h2, wg_ref[...])
        ups.append(_dot(h2, wu_ref[...]))
    for r in range(FFN_SUBBLOCKS):
        lo = V7X_SUBLANES + r * sub
        conv = (cb + cw[0:1] * ext_ref[lo - 2:lo - 2 + sub]
                + cw[1:2] * ext_ref[lo - 1:lo - 1 + sub] + cw[2:3] * ext_ref[lo:lo + sub])
        y = (jax.nn.silu(conv) * ups[r]).astype(BF16)
        acc_ref[r * sub:(r + 1) * sub] += _dot(y, wd_ref[...])
    tail = ext_ref[tm:tm + V7X_SUBLANES]
    carry_ref[f] = tail
    tail_ref[f] = tail

    @pl.when(f == pl.num_programs(1) - 1)
    def _():
        x2 = x_ref[...] + mod_ref[5:6] * acc_ref[...]
        o_ref[...] = _rms(x2, gf_ref[...]) if final_norm else x2


def _ffn(x, h2, mod, wg, wu, wd, cw, cb, g_final, seq, final_norm):
    tokens = x.shape[0]
    tiles_per_seq = seq // TM
    nf = D_FF // TF
    kern = functools.partial(_ffn_kernel, tiles_per_seq=tiles_per_seq, final_norm=final_norm)
    return pl.pallas_call(
        kern,
        out_shape=(
            jax.ShapeDtypeStruct((tokens, D_MODEL), F32),
            jax.ShapeDtypeStruct((tokens // TM, nf, V7X_SUBLANES, TF), F32),
        ),
        grid=(tokens // TM, nf),
        in_specs=[
            pl.BlockSpec((TM, D_MODEL), lambda i, f: (i, 0)),
            pl.BlockSpec((TM, D_MODEL), lambda i, f: (i, 0)),
            pl.BlockSpec((None, 6, D_MODEL), lambda i, f: (i // tiles_per_seq, 0, 0)),
            pl.BlockSpec((D_MODEL, TF), lambda i, f: (0, f)),
            pl.BlockSpec((D_MODEL, TF), lambda i, f: (0, f)),
            pl.BlockSpec((TF, D_MODEL), lambda i, f: (f, 0)),
            _resident((nf, 3, TF)),
            _resident((nf, 1, TF)),
            _resident((1, D_MODEL)),
        ],
        out_specs=(
            pl.BlockSpec((TM, D_MODEL), lambda i, f: (i, 0)),
            pl.BlockSpec((None, nf, V7X_SUBLANES, TF), lambda i, f: (i, 0, 0, 0)),
        ),
        scratch_shapes=[
            pltpu.VMEM((TM, D_MODEL), F32),
            pltpu.VMEM((TM + V7X_SUBLANES, TF), F32),
            pltpu.VMEM((nf, V7X_SUBLANES, TF), F32),
        ],
        compiler_params=_params("arbitrary", "arbitrary"),
        name="conv_ffn",
    )(x, h2, mod, wg, wu, wd, cw, cb, g_final)


def _s_in_proj_kernel(x_ref, mod_ref, g1_ref, w_ref, z_ref, wb_ref, h_ref):
    @pl.when(pl.program_id(0) == 0)
    def _():
        h_ref[...] = (_rms(x_ref[...], g1_ref[...]) * (1.0 + mod_ref[1]) + mod_ref[0]).astype(BF16)

    w = w_ref[...].astype(BF16)
    wb_ref[...] = w
    z_ref[...] = _dot(h_ref[...], w)


def _s_in_proj(layer, x, mod, g1, w_in):
    rows = x.shape[0]
    full = lambda shape: pl.BlockSpec(shape, lambda j: (0,) * len(shape))
    return pl.pallas_call(
        _s_in_proj_kernel,
        out_shape=(jax.ShapeDtypeStruct((rows, IN_COLS), F32), jax.ShapeDtypeStruct((D_MODEL, IN_COLS), BF16)),
        grid=(IN_COLS // TN_IN,),
        in_specs=[
            full((rows, D_MODEL)), full((6, rows, D_MODEL)), full((1, D_MODEL)),
            pl.BlockSpec((None, D_MODEL, TN_IN), lambda j: (layer, 0, j)),
        ],
        out_specs=(pl.BlockSpec((rows, TN_IN), lambda j: (0, j)), pl.BlockSpec((D_MODEL, TN_IN), lambda j: (0, j))),
        scratch_shapes=[pltpu.VMEM((rows, D_MODEL), BF16)],
        compiler_params=_params("arbitrary"),
        name="sample_in_proj",
    )(x, mod, g1, w_in)


def _s_attn_kernel(sink_ref, z_ref, cos_ref, sin_ref, gain_ref, ws0_ref, bs0_ref, kt_ref, vt_ref,
                   attn_ref, gm_ref, gvn_ref, ktn_ref, vtn_ref):
    nseq = z_ref.shape[0]
    q_blocks, k_blocks, u, gvn = _post_project(
        z_ref[:, Q0:K0], z_ref[:, K0:V0], z_ref[:, U0:G0], z_ref[:, G0:IN_COLS],
        cos_ref[...], sin_ref[...], gain_ref[...], pair_q=False)
    v_new = z_ref[:, V0:U0]
    k_new = jnp.concatenate(k_blocks, axis=1)
    gvn_ref[...] = gvn
    gm_ref[...] = (u * (ws0_ref[...].astype(BF16).astype(F32) * gvn.astype(BF16).astype(F32) + bs0_ref[...])).astype(BF16)

    rows = GQA_GROUP * nseq
    keys = nseq * WINDOW
    seq_of_row = jnp.bitwise_and(_iota((rows, 1), 0), nseq - 1)
    head_of_row = lax.shift_right_logical(_iota((rows, 1), 0), nseq.bit_length() - 1)
    own_keys = lax.shift_right_logical(_iota((rows, keys), 1), WINDOW.bit_length() - 1) == seq_of_row
    own_new = _iota((rows, nseq), 1) == seq_of_row
    pieces = []
    for g in range(N_KV_HEADS):
        gs = slice(g * HEAD_DIM, (g + 1) * HEAD_DIM)
        heads = [g * GQA_GROUP + r for r in range(GQA_GROUP)]
        qg = jnp.concatenate(
            [q_blocks[h // 2][:, (h % 2) * HEAD_DIM:(h % 2 + 1) * HEAD_DIM] for h in heads], axis=0).astype(BF16)
        ktg = jnp.concatenate([kt_ref[b, gs, :] for b in range(nseq)], axis=1).astype(BF16)
        vtg = jnp.concatenate([vt_ref[b, gs, :] for b in range(nseq)], axis=1).astype(BF16)
        s = jnp.where(own_keys, _dot(qg, ktg), NEG)
        sn = jnp.where(own_new, _dot_nt(qg, k_new[:, gs].astype(BF16)), NEG)
        sk = jnp.full((rows, 1), sink_ref[heads[0]], F32)
        for r in range(1, GQA_GROUP):
            sk = jnp.where(head_of_row == r, sink_ref[heads[r]], sk)
        m = jnp.maximum(jnp.maximum(jnp.max(s, axis=-1, keepdims=True), jnp.max(sn, axis=-1, keepdims=True)), sk)
        p = jnp.exp(s - m)
        pn = jnp.exp(sn - m)
        denom = jnp.sum(p, axis=-1, keepdims=True) + jnp.sum(pn, axis=-1, keepdims=True) + jnp.exp(sk - m)
        o = (_dot_nt(p.astype(BF16), vtg) + _dot(pn.astype(BF16), v_new[:, gs].astype(BF16))) / denom
        pieces += [o[r * nseq:(r + 1) * nseq] for r in range(GQA_GROUP)]
    attn_ref[...] = jnp.concatenate(pieces, axis=1).astype(BF16)

    pad = jnp.zeros((V7X_LANES - nseq, KV_WIDTH), F32)
    knt = jnp.concatenate([k_new, pad], axis=0).T
    vnt = jnp.concatenate([v_new, pad], axis=0).T
    newest = _iota((1, WINDOW), 1) == WINDOW - 1
    for b in range(nseq):
        ktn_ref[b] = jnp.where(newest, knt[:, b:b + 1], pltpu.roll(kt_ref[b], WINDOW - 1, axis=1))
        vtn_ref[b] = jnp.where(newest, vnt[:, b:b + 1], pltpu.roll(vt_ref[b], WINDOW - 1, axis=1))


def _s_attention(layer, sinks, z, cos, sin_signed, gain, ws0, bs0, kt_all, vt_all):
    rows = z.shape[0]
    full = lambda shape: pl.BlockSpec(shape, lambda i: (0,) * len(shape))
    cache = pl.BlockSpec((None, rows, KV_WIDTH, WINDOW), lambda i: (layer, 0, 0, 0))
    return pl.pallas_call(
        _s_attn_kernel,
        out_shape=(
            jax.ShapeDtypeStruct((rows, ATTN_WIDTH), BF16),
            jax.ShapeDtypeStruct((rows, GM_WIDTH), BF16),
            jax.ShapeDtypeStruct((rows, GM_WIDTH), F32),
            jax.ShapeDtypeStruct((rows, KV_WIDTH, WINDOW), F32),
            jax.ShapeDtypeStruct((rows, KV_WIDTH, WINDOW), F32),
        ),
        grid=(1,),
        in_specs=[
            pl.BlockSpec(memory_space=pltpu.SMEM),
            full((rows, IN_COLS)), full((1, V7X_LANES)), full((1, V7X_LANES)),
            full((1, GM_WIDTH)), full((1, GM_WIDTH)), full((1, GM_WIDTH)),
            cache, cache,
        ],
        out_specs=(full((rows, ATTN_WIDTH)), full((rows, GM_WIDTH)), full((rows, GM_WIDTH)),
                   full((rows, KV_WIDTH, WINDOW)), full((rows, KV_WIDTH, WINDOW))),
        compiler_params=_params("arbitrary"),
        name="sample_attention",
    )(sinks, z, cos, sin_signed, gain, ws0, bs0, kt_all, vt_all)


def _paired_head(slot):
    p, rest = divmod(slot, 2 * GQA_GROUP)
    j, i = divmod(rest, 2)
    return p * 2 * GQA_GROUP + i * GQA_GROUP + j


def _s_mix_out_kernel(x_ref, gate_ref, a_ref, gm_ref, w_ref, o_ref, wb_ref):
    w = w_ref[...].astype(BF16)
    for slot in range(N_HEADS):
        head = _paired_head(slot)
        wb_ref[slot * HEAD_DIM:(slot + 1) * HEAD_DIM] = w[head * HEAD_DIM:(head + 1) * HEAD_DIM]
    wb_ref[ATTN_WIDTH:] = w[ATTN_WIDTH:]
    mix = _dot(a_ref[...], w[:ATTN_WIDTH]) + _dot(gm_ref[...], w[ATTN_WIDTH:])
    o_ref[...] = x_ref[...] + gate_ref[...] * mix


def _s_mix_out(layer, x, gate, attn, gm, w_out):
    rows = x.shape[0]
    full = lambda shape: pl.BlockSpec(shape, lambda j: (0,) * len(shape))
    col_tile = lambda r: pl.BlockSpec((r, TN_OUT), lambda j: (0, j))
    return pl.pallas_call(
        _s_mix_out_kernel,
        out_shape=(jax.ShapeDtypeStruct((rows, D_MODEL), F32), jax.ShapeDtypeStruct((D_MODEL, D_MODEL), BF16)),
        grid=(D_MODEL // TN_OUT,),
        in_specs=[
            col_tile(rows), col_tile(rows), full((rows, ATTN_WIDTH)), full((rows, GM_WIDTH)),
            pl.BlockSpec((None, D_MODEL, TN_OUT), lambda j: (layer, 0, j)),
        ],
        out_specs=(col_tile(rows), col_tile(D_MODEL)),
        compiler_params=_params("arbitrary"),
        name="sample_mix_out",
    )(x, gate, attn, gm, w_out)


def _s_ffn_kernel(x_ref, mod_ref, g2_ref, wg_ref, wu_ref, wd_ref, cw_ref, cb_ref, p0_ref, p1_ref, gf_ref,
                  o_ref, a_ref, wgb_ref, wub_ref, wdb_ref, h2_ref, acc_ref, *, final_norm):
    f = pl.program_id(0)

    @pl.when(f == 0)
    def _():
        h2_ref[...] = (_rms(x_ref[...], g2_ref[...]) * (1.0 + mod_ref[4]) + mod_ref[3]).astype(BF16)
        acc_ref[...] = jnp.zeros_like(acc_ref)

    wg, wu, wd = wg_ref[...].astype(BF16), wu_ref[...].astype(BF16), wd_ref[...].astype(BF16)
    wgb_ref[...] = wg
    wub_ref[...] = wu
    wdb_ref[...] = wd
    h2 = h2_ref[...]
    a = _dot(h2, wg)
    up = _dot(h2, wu)
    a_ref[...] = a
    cw = cw_ref[f]
    conv = cb_ref[f] + cw[0:1] * p0_ref[...] + cw[1:2] * p1_ref[...] + cw[2:3] * a
    acc_ref[...] += _dot((jax.nn.silu(conv) * up).astype(BF16), wd)

    @pl.when(f == pl.num_programs(0) - 1)
    def _():
        x2 = x_ref[...] + mod_ref[5] * acc_ref[...]
        o_ref[...] = _rms(x2, gf_ref[...]) if final_norm else x2


def _s_ffn(layer, x, mod, g2, w_gate, w_up, w_down, cw, cb, p0, p1, g_final, final_norm):
    rows = x.shape[0]
    nf = D_FF // TF
    full = lambda shape: pl.BlockSpec(shape, lambda f: (0,) * len(shape))
    return pl.pallas_call(
        functools.partial(_s_ffn_kernel, final_norm=final_norm),
        out_shape=(
            jax.ShapeDtypeStruct((rows, D_MODEL), F32), jax.ShapeDtypeStruct((rows, D_FF), F32),
            jax.ShapeDtypeStruct((D_MODEL, D_FF), BF16), jax.ShapeDtypeStruct((D_MODEL, D_FF), BF16),
            jax.ShapeDtypeStruct((D_FF, D_MODEL), BF16),
        ),
        grid=(nf,),
        in_specs=[
            full((rows, D_MODEL)), full((6, rows, D_MODEL)), full((1, D_MODEL)),
            pl.BlockSpec((None, D_MODEL, TF), lambda f: (layer, 0, f)),
            pl.BlockSpec((None, D_MODEL, TF), lambda f: (layer, 0, f)),
            pl.BlockSpec((None, TF, D_MODEL), lambda f: (layer, f, 0)),
            full((nf, 3, TF)), full((nf, 1, TF)),
            pl.BlockSpec((rows, TF), lambda f: (0, f)),
            pl.BlockSpec((rows, TF), lambda f: (0, f)),
            full((1, D_MODEL)),
        ],
        out_specs=(
            full((rows, D_MODEL)), pl.BlockSpec((rows, TF), lambda f: (0, f)),
            pl.BlockSpec((D_MODEL, TF), lambda f: (0, f)), pl.BlockSpec((D_MODEL, TF), lambda f: (0, f)),
            pl.BlockSpec((TF, D_MODEL), lambda f: (f, 0)),
        ),
        scratch_shapes=[pltpu.VMEM((rows, D_MODEL), BF16), pltpu.VMEM((rows, D_MODEL), F32)],
        compiler_params=_params("arbitrary"),
        name="sample_conv_ffn",
    )(x, mod, g2, w_gate, w_up, w_down, cw, cb, p0, p1, g_final)


def _rope_tables(pos):
    half = HEAD_DIM // 2
    inv = ROPE_THETA ** (-jnp.arange(half, dtype=jnp.float32) / half)
    ang = pos.astype(jnp.float32)[:, None] * inv[None, :]
    reps = V7X_LANES // half
    sign = jnp.where((jnp.arange(V7X_LANES) % HEAD_DIM) < half, -1.0, 1.0).astype(F32)
    return jnp.tile(jnp.cos(ang), (1, reps)), jnp.tile(jnp.sin(ang), (1, reps)) * sign[None, :]


def _to_planes(window_buf):
    lead = window_buf.shape[:-3]
    return jnp.moveaxis(window_buf, -3, -1).reshape(lead + (KV_WIDTH, WINDOW))


def _from_planes(planes):
    lead = planes.shape[:-2]
    return jnp.moveaxis(planes.reshape(lead + (N_KV_HEADS, HEAD_DIM, WINDOW)), -1, -3)


def kernel(x_prompt, x_sample, cache_k, cache_v, state_conv, c_prompt, c_sample, w_ada, b_ada, g_norm1, g_norm2,
           w_in, gm_gain, gm_ws, gm_bs, sinks, w_out, w_gate, w_up, conv_w, conv_b, w_down, g_final):
    batch, seq, _ = x_prompt.shape
    dec = x_sample.shape[0]
    nf = D_FF // TF
    assert x_sample.shape[1] == 1 and seq % TM == 0 and seq % TQ == 0 and TM % CHUNK == 0 and D_FF % TF == 0
    assert dec & (dec - 1) == 0 and dec <= V7X_LANES

    bst = jnp.swapaxes(gm_bs, 1, 2)
    ws0 = jnp.repeat(gm_ws[:, :, 0, 0], GM_WIDTH // GM_HEADS, axis=1)[:, None, :]
    bs0 = jnp.repeat(gm_bs[:, :, 0], GM_WIDTH // GM_HEADS, axis=1)[:, None, :]
    cw_tiles = jnp.swapaxes(conv_w.reshape(DEPTH, 3, nf, TF), 1, 2)
    cb_tiles = conv_b.reshape(DEPTH, nf, 1, TF)

    cos_p, sin_p = _rope_tables(jnp.arange(seq, dtype=jnp.int32))
    cos_s, sin_s = _rope_tables(PAST_LEN + jnp.arange(1, dtype=jnp.int32))

    c_all = jnp.concatenate([c_prompt, c_sample, jnp.zeros((MOD_ROWS - batch - dec, D_MODEL), F32)], axis=0)
    mod = _modulation(c_all, w_ada, b_ada)

    kt_all, vt_all = _to_planes(cache_k), _to_planes(cache_v)

    hp = x_prompt.reshape(batch * seq, D_MODEL)
    hs = x_sample.reshape(dec, D_MODEL)
    g_fin = g_final.reshape(1, D_MODEL)
    outs = [[] for _ in range(8)]
    for l in range(DEPTH):
        last = l == DEPTH - 1
        g1 = g_norm1[l].reshape(1, D_MODEL)
        g2 = g_norm2[l].reshape(1, D_MODEL)
        gain = gm_gain[l].reshape(1, GM_WIDTH)
        mod_p = jnp.swapaxes(mod[l, :, :batch], 0, 1)
        mod_s = mod[l, :, batch:batch + dec]

        z, w_in_b = _s_in_proj(l, hs, mod_s, g1, w_in)
        attn_s, gm_s, gvn_s, kt_new, vt_new = _s_attention(
            l, sinks[l], z, cos_s, sin_s, gain, ws0[l], bs0[l], kt_all, vt_all)
        hs, w_out_b = _s_mix_out(l, hs, mod_s[2], attn_s, gm_s, w_out)
        hs, a_new, wg_b, wu_b, wd_b = _s_ffn(l, hs, mod_s, g2, w_gate, w_up, w_down, cw_tiles[l], cb_tiles[l],
                                             state_conv[l, :, 0], state_conv[l, :, 1], g_fin, last)
        outs[4].append(kt_new)
        outs[5].append(vt_new)
        outs[6].append(gvn_s[:, None, :])
        outs[7].append(jnp.stack([state_conv[l, :, 1], a_new], axis=1))

        q, k, v, gm, gvl = _mix_in(hp, mod_p, g1, w_in_b, cos_p, sin_p, gain, gm_ws[l], bst[l], seq)
        attn, kt_p, vt_p = _attention(sinks[l], q, k, v, seq)
        hp, h2 = _mix_out(hp, mod_p, g2, attn, gm, w_out_b, seq)
        hp, tail = _ffn(hp, h2, mod_p, wg_b, wu_b, wd_b, cw_tiles[l], cb_tiles[l], g_fin, seq, last)
        outs[0].append(kt_p)
        outs[1].append(vt_p)
        outs[2].append(gvl)
        tails = tail[seq // TM - 1::seq // TM, :, V7X_SUBLANES - 2:, :]
        outs[3].append(jnp.swapaxes(tails, 1, 2).reshape(batch, 2, D_FF))

    st = [jnp.stack(o) for o in outs]
    return (hp.reshape(batch, seq, D_MODEL), hs.reshape(dec, 1, D_MODEL),
            _from_planes(st[0]), _from_planes(st[1]), st[2], st[3],
            _from_planes(st[4]), _from_planes(st[5]), st[6], st[7])
```

```python
import functools

import jax
import jax.numpy as jnp
from jax import lax
from jax.experimental import pallas as pl
from jax.experimental.pallas import tpu as pltpu

D_MODEL = 2048
DEPTH = 2
PAST_LEN = 16384
ATTN_WIDTH = 1024
GM_WIDTH = 1024
HEAD_DIM = 64
N_HEADS = 16
N_KV_HEADS = 4
GQA_GROUP = N_HEADS // N_KV_HEADS
WINDOW = 128
ROPE_THETA = 10000.0
CHUNK = 128
GM_HEADS = 8
D_FF = 5632
KV_WIDTH = N_KV_HEADS * HEAD_DIM
IN_COLS = ATTN_WIDTH + 2 * KV_WIDTH + 2 * GM_WIDTH
EPS = 1e-6
NEG = -1e30

Q0, K0, V0, U0, G0 = 0, ATTN_WIDTH, ATTN_WIDTH + KV_WIDTH, ATTN_WIDTH + 2 * KV_WIDTH, ATTN_WIDTH + 2 * KV_WIDTH + GM_WIDTH

V7X_LANES = 128
V7X_SUBLANES = 8
V7X_VMEM_LIMIT_BYTES = 56 * 1024 * 1024

TM = 512
TF = 512
TM_FF = 1024
TQ = 512
TN_IN = 512
TN_OUT = 512
MOD_TN = 1024
MOD_ROWS = 48
FFN_SUBBLOCKS = 4
MIX_OUT_SUBBLOCKS = 2

BF16 = jnp.bfloat16
F32 = jnp.float32


def _params(*semantics):
    return pltpu.CompilerParams(dimension_semantics=semantics, vmem_limit_bytes=V7X_VMEM_LIMIT_BYTES)


def _dot(a, b):
    return jnp.dot(a, b, preferred_element_type=F32)


def _dot_nt(a, b):
    return lax.dot_general(a, b, (((1,), (1,)), ((), ())), preferred_element_type=F32)


def _rms(x, g):
    return x * lax.rsqrt(jnp.mean(x * x, axis=-1, keepdims=True) + EPS) * g


def _resident(shape):
    zeros = (0,) * len(shape)
    return pl.BlockSpec(shape, lambda *_: zeros, pipeline_mode=pl.Buffered(1))


def _iota(shape, dim):
    return lax.broadcasted_iota(jnp.int32, shape, dim)


def _mod_kernel(c_ref, w_ref, b_ref, o_ref):
    s = jax.nn.silu(c_ref[...]).astype(BF16)
    o_ref[...] = _dot(s, w_ref[...].astype(BF16)) + b_ref[...]


def _modulation(c_all, w_ada, b_ada):
    n_tiles = (6 * D_MODEL) // MOD_TN
    per_chunk = D_MODEL // MOD_TN
    return pl.pallas_call(
        _mod_kernel,
        out_shape=jax.ShapeDtypeStruct((DEPTH, 6, MOD_ROWS, D_MODEL), F32),
        grid=(DEPTH, n_tiles),
        in_specs=[
            pl.BlockSpec((MOD_ROWS, D_MODEL), lambda l, j: (0, 0)),
            pl.BlockSpec((None, D_MODEL, MOD_TN), lambda l, j: (l, 0, j)),
            pl.BlockSpec((None, 1, MOD_TN), lambda l, j: (l, 0, j)),
        ],
        out_specs=pl.BlockSpec((None, None, MOD_ROWS, MOD_TN), lambda l, j: (l, j // per_chunk, 0, j % per_chunk)),
        compiler_params=_params("arbitrary", "arbitrary"),
        name="adaln_modulation",
    )(c_all, w_ada, b_ada.reshape(DEPTH, 1, 6 * D_MODEL))


def _rope_cols(z, cos, sin_signed):
    lane = _iota((1, V7X_LANES), 1)
    first_half = jnp.bitwise_and(lane, HEAD_DIM - 1) < HEAD_DIM // 2
    outs = []
    for c in range(z.shape[1] // V7X_LANES):
        zc = z[:, c * V7X_LANES:(c + 1) * V7X_LANES]
        partner = jnp.where(first_half,
                            pltpu.roll(zc, V7X_LANES - HEAD_DIM // 2, axis=1),
                            pltpu.roll(zc, HEAD_DIM // 2, axis=1))
        outs.append(zc * cos + partner * sin_signed)
    return outs


def _swap_halves(x):
    return pltpu.roll(x, HEAD_DIM, axis=1)


def _pair_q_blocks(nat):
    low = _iota((1, V7X_LANES), 1) < HEAD_DIM
    out = []
    for c in range(len(nat)):
        p, j = divmod(c, 4)
        a, b = nat[p * 4 + j // 2], nat[p * 4 + 2 + j // 2]
        out.append(jnp.where(low, a, _swap_halves(b)) if j % 2 == 0 else jnp.where(low, _swap_halves(a), b))
    return out


def _post_project(zq, zk, zu, zg, cos, sin_signed, gain, pair_q):
    q_blocks = _rope_cols(zq, cos, sin_signed)
    if pair_q:
        q_blocks = _pair_q_blocks(q_blocks)
    q_blocks = [qb * (HEAD_DIM ** -0.5) for qb in q_blocks]
    k_blocks = _rope_cols(zk, cos, sin_signed)
    u = jax.nn.gelu(zu)
    gvn = _rms(jax.nn.gelu(zg), gain)
    return q_blocks, k_blocks, u, gvn


def _store_cols(ref, blocks):
    for c, blk in enumerate(blocks):
        ref[:, c * V7X_LANES:(c + 1) * V7X_LANES] = blk.astype(ref.dtype)


def _mix_in_kernel(x_ref, mod_ref, g1_ref, w_ref, cos_ref, sin_ref, gain_ref, ws_ref, bst_ref,
                   q_ref, k_ref, v_ref, gm_ref, gvl_ref):
    tm = x_ref.shape[0]
    h = (_rms(x_ref[...], g1_ref[...]) * (1.0 + mod_ref[1:2]) + mod_ref[0:1]).astype(BF16)
    q_blocks, k_blocks, u, gvn = _post_project(
        _dot(h, w_ref[:, Q0:K0]), _dot(h, w_ref[:, K0:V0]), _dot(h, w_ref[:, U0:G0]), _dot(h, w_ref[:, G0:IN_COLS]),
        cos_ref[...], sin_ref[...], gain_ref[...], pair_q=True)
    _store_cols(q_ref, q_blocks)
    _store_cols(k_ref, k_blocks)
    v_ref[...] = _dot(h, w_ref[:, V0:U0])
    gvl_ref[...] = gvn[tm - CHUNK:, :]
    gvb = gvn.astype(BF16)
    causal = _iota((CHUNK, CHUNK), 1) <= _iota((CHUNK, CHUNK), 0)
    for hh in range(GM_HEADS):
        wm = jnp.where(causal, ws_ref[hh], 0.0).astype(BF16)
        bias = bst_ref[:, hh:hh + 1]
        cs = slice(hh * CHUNK, (hh + 1) * CHUNK)
        for c in range(tm // CHUNK):
            rs = slice(c * CHUNK, (c + 1) * CHUNK)
            s = _dot(wm, gvb[rs, cs]) + bias
            gm_ref[rs, cs] = (u[rs, cs] * s).astype(BF16)


def _mix_in(x, mod, g1, w_in, cos, sin_signed, gain, ws, bst, seq):
    tokens = x.shape[0]
    tiles_per_seq = seq // TM
    batch = tokens // seq
    row_tile = lambda width: pl.BlockSpec((TM, width), lambda i: (i, 0))
    return pl.pallas_call(
        _mix_in_kernel,
        out_shape=(
            jax.ShapeDtypeStruct((tokens, ATTN_WIDTH), BF16),
            jax.ShapeDtypeStruct((tokens, KV_WIDTH), F32),
            jax.ShapeDtypeStruct((tokens, KV_WIDTH), F32),
            jax.ShapeDtypeStruct((tokens, GM_WIDTH), BF16),
            jax.ShapeDtypeStruct((batch, CHUNK, GM_WIDTH), F32),
        ),
        grid=(tokens // TM,),
        in_specs=[
            row_tile(D_MODEL),
            pl.BlockSpec((None, 6, D_MODEL), lambda i: (i // tiles_per_seq, 0, 0)),
            _resident((1, D_MODEL)),
            _resident((D_MODEL, IN_COLS)),
            pl.BlockSpec((TM, V7X_LANES), lambda i: (i % tiles_per_seq, 0)),
            pl.BlockSpec((TM, V7X_LANES), lambda i: (i % tiles_per_seq, 0)),
            _resident((1, GM_WIDTH)),
            _resident((GM_HEADS, CHUNK, CHUNK)),
            _resident((CHUNK, GM_HEADS)),
        ],
        out_specs=(
            row_tile(ATTN_WIDTH), row_tile(KV_WIDTH), row_tile(KV_WIDTH), row_tile(GM_WIDTH),
            pl.BlockSpec((None, CHUNK, GM_WIDTH), lambda i: (i // tiles_per_seq, 0, 0)),
        ),
        compiler_params=_params("arbitrary"),
        name="mix_in",
    )(x, mod, g1, w_in, cos, sin_signed, gain, ws, bst)


def _attn_block(n, sink_ref, q_blk, k_prev, k_cur, v_prev, v_cur, out_blk):
    row = _iota((WINDOW, 2 * WINDOW), 0)
    key = _iota((WINDOW, 2 * WINDOW), 1) - WINDOW
    valid = (key <= row) & (key >= jnp.maximum(row - WINDOW, -n * WINDOW))
    low = _iota((WINDOW, V7X_LANES), 1) < HEAD_DIM
    cols_per_pair = ATTN_WIDTH // V7X_LANES // 2
    for p in range(2):
        ls = slice(p * V7X_LANES, (p + 1) * V7X_LANES)
        kcat = jnp.concatenate([k_prev[:, ls], k_cur[:, ls]], axis=0).astype(BF16)
        vcat = jnp.concatenate([v_prev[:, ls], v_cur[:, ls]], axis=0).astype(BF16)
        blocks = []
        for cc in range(cols_per_pair):
            c = p * cols_per_pair + cc
            qcol = q_blk[:, c * V7X_LANES:(c + 1) * V7X_LANES]
            zero = jnp.zeros_like(qcol)
            blocks += [jnp.where(low, qcol, zero), jnp.where(low, zero, qcol)]
        s = _dot_nt(jnp.concatenate(blocks, axis=0), kcat)
        probs, inv = [], []
        for j in range(2 * cols_per_pair):
            cc, e = divmod(j, 2)
            sj = jnp.where(valid, s[j * WINDOW:(j + 1) * WINDOW], NEG)
            sk = sink_ref[p * 2 * GQA_GROUP + e * GQA_GROUP + cc]
            m = jnp.maximum(jnp.max(sj, axis=-1, keepdims=True), sk)
            pj = jnp.exp(sj - m)
            inv.append(1.0 / (jnp.sum(pj, axis=-1, keepdims=True) + jnp.exp(sk - m)))
            probs.append(pj.astype(BF16))
        o = _dot(jnp.concatenate(probs, axis=0), vcat)
        for cc in range(cols_per_pair):
            c = p * cols_per_pair + cc
            even = o[(2 * cc) * WINDOW:(2 * cc + 1) * WINDOW] * inv[2 * cc]
            odd = o[(2 * cc + 1) * WINDOW:(2 * cc + 2) * WINDOW] * inv[2 * cc + 1]
            out_blk[:, c * V7X_LANES:(c + 1) * V7X_LANES] = jnp.where(low, even, odd).astype(BF16)


def _attn_kernel(sink_ref, q_ref, kp_ref, kc_ref, vp_ref, vc_ref, o_ref, kt_ref, vt_ref):
    step = pl.program_id(1)
    blocks = q_ref.shape[0] // WINDOW
    for t in range(blocks):
        rows = slice(t * WINDOW, (t + 1) * WINDOW)
        prev = slice((t - 1) * WINDOW, t * WINDOW)
        k_prev = kp_ref[...] if t == 0 else kc_ref[prev]
        v_prev = vp_ref[...] if t == 0 else vc_ref[prev]
        _attn_block(step * blocks + t, sink_ref, q_ref.at[rows], k_prev, kc_ref[rows], v_prev, vc_ref[rows], o_ref.at[rows])

    @pl.when(step == pl.num_programs(1) - 1)
    def _():
        last = slice((blocks - 1) * WINDOW, blocks * WINDOW)
        kt_ref[...] = kc_ref[last].T
        vt_ref[...] = vc_ref[last].T


def _attention(sinks, q, k, v, seq):
    tokens = q.shape[0]
    batch = tokens // seq
    steps = seq // TQ
    per_step = TQ // WINDOW
    cur = lambda b, n: (b * steps + n, 0)
    prev = lambda b, n: (jnp.maximum((b * steps + n) * per_step - 1, 0), 0)
    state = pl.BlockSpec((None, KV_WIDTH, WINDOW), lambda b, n: (b, 0, 0))
    return pl.pallas_call(
        _attn_kernel,
        out_shape=(
            jax.ShapeDtypeStruct((tokens, ATTN_WIDTH), BF16),
            jax.ShapeDtypeStruct((batch, KV_WIDTH, WINDOW), F32),
            jax.ShapeDtypeStruct((batch, KV_WIDTH, WINDOW), F32),
        ),
        grid=(batch, steps),
        in_specs=[
            pl.BlockSpec(memory_space=pltpu.SMEM),
            pl.BlockSpec((TQ, ATTN_WIDTH), cur),
            pl.BlockSpec((WINDOW, KV_WIDTH), prev),
            pl.BlockSpec((TQ, KV_WIDTH), cur),
            pl.BlockSpec((WINDOW, KV_WIDTH), prev),
            pl.BlockSpec((TQ, KV_WIDTH), cur),
        ],
        out_specs=(pl.BlockSpec((TQ, ATTN_WIDTH), cur), state, state),
        compiler_params=_params("arbitrary", "arbitrary"),
        name="swa_attention",
    )(sinks, q, k, k, v, v)


def _mix_out_kernel(x_ref, mod_ref, g2_ref, a_ref, gm_ref, wa_ref, wb_ref, o_ref, h2_ref):
    sub = x_ref.shape[0] // MIX_OUT_SUBBLOCKS
    for r in range(MIX_OUT_SUBBLOCKS):
        rs = slice(r * sub, (r + 1) * sub)
        mix = _dot(a_ref[rs], wa_ref[...]) + _dot(gm_ref[rs], wb_ref[...])
        x1 = x_ref[rs] + mod_ref[2:3] * mix
        o_ref[rs] = x1
        h2_ref[rs] = (_rms(x1, g2_ref[...]) * (1.0 + mod_ref[4:5]) + mod_ref[3:4]).astype(BF16)


def _mix_out(x, mod, g2, attn, gm, w_out, seq):
    tokens = x.shape[0]
    tiles_per_seq = seq // TM
    row_tile = lambda width: pl.BlockSpec((TM, width), lambda i: (i, 0))
    half = lambda k: pl.BlockSpec((ATTN_WIDTH, D_MODEL), lambda i: (k, 0), pipeline_mode=pl.Buffered(1))
    return pl.pallas_call(
        _mix_out_kernel,
        out_shape=(jax.ShapeDtypeStruct((tokens, D_MODEL), F32), jax.ShapeDtypeStruct((tokens, D_MODEL), BF16)),
        grid=(tokens // TM,),
        in_specs=[
            row_tile(D_MODEL),
            pl.BlockSpec((None, 6, D_MODEL), lambda i: (i // tiles_per_seq, 0, 0)),
            _resident((1, D_MODEL)),
            row_tile(ATTN_WIDTH), row_tile(GM_WIDTH),
            half(0), half(1),
        ],
        out_specs=(row_tile(D_MODEL), row_tile(D_MODEL)),
        compiler_params=_params("arbitrary"),
        name="mix_out",
    )(x, mod, g2, attn, gm, w_out, w_out)


def _ffn_kernel(x_hbm, h2_ref, mod_ref, wg_ref, wu_ref, wd_ref, cw_ref, cb_ref, gf_ref,
                o_ref, tail_ref, acc_ref, ext_ref, carry_ref, x_sem, *, tiles_per_seq, final_norm):
    i = pl.program_id(0)
    f = pl.program_id(1)
    tm = h2_ref.shape[0]
    last = f == pl.num_programs(1) - 1
    x_copy = pltpu.make_async_copy(x_hbm.at[pl.ds(pl.multiple_of(i * tm, tm), tm)], o_ref, x_sem)

    @pl.when(last)
    def _():
        x_copy.start()

    @pl.when(f == 0)
    def _():
        acc_ref[...] = jnp.zeros_like(acc_ref)

    first = (i % tiles_per_seq) == 0

    @pl.when(first)
    def _():
        ext_ref[0:V7X_SUBLANES] = jnp.zeros((V7X_SUBLANES, ext_ref.shape[1]), F32)

    @pl.when(jnp.logical_not(first))
    def _():
        ext_ref[0:V7X_SUBLANES] = carry_ref[f]

    sub = tm // FFN_SUBBLOCKS
    cw = cw_ref[f]
    cb = cb_ref[f]
    ups = []
    for r in range(FFN_SUBBLOCKS):
        h2 = h2_ref[r * sub:(r + 1) * sub]
        ext_ref[V7X_SUBLANES + r * sub:V7X_SUBLANES + (r + 1) * sub] = _dot(h2, wg_ref[...])
        ups.append(_dot(h2, wu_ref[...]))
    for r in range(FFN_SUBBLOCKS):
        lo = V7X_SUBLANES + r * sub
        conv = (cb + cw[0:1] * ext_ref[lo - 2:lo - 2 + sub]
                + cw[1:2] * ext_ref[lo - 1:lo - 1 + sub] + cw[2:3] * ext_ref[lo:lo + sub])
        y = (jax.nn.silu(conv) * ups[r]).astype(BF16)
        acc_ref[r * sub:(r + 1) * sub] += _dot(y, wd_ref[...])
    tail = ext_ref[tm:tm + V7X_SUBLANES]
    carry_ref[f] = tail
    tail_ref[f] = tail

    @pl.when(last)
    def _():
        x_copy.wait()

        def finish(r, carry):
            rs = pl.ds(pl.multiple_of(r * sub, sub), sub)
            x2 = o_ref[rs, :] + mod_ref[5:6] * acc_ref[rs, :]
            o_ref[rs, :] = _rms(x2, gf_ref[...]) if final_norm else x2
            return carry

        lax.fori_loop(0, FFN_SUBBLOCKS, finish, 0)


def _ffn(x, h2, mod, wg, wu, wd, cw, cb, g_final, seq, final_norm):
    tokens = x.shape[0]
    tiles_per_seq = seq // TM_FF
    nf = D_FF // TF
    kern = functools.partial(_ffn_kernel, tiles_per_seq=tiles_per_seq, final_norm=final_norm)
    return pl.pallas_call(
        kern,
        out_shape=(
            jax.ShapeDtypeStruct((tokens, D_MODEL), F32),
            jax.ShapeDtypeStruct((tokens // TM_FF, nf, V7X_SUBLANES, TF), F32),
        ),
        grid=(tokens // TM_FF, nf),
        in_specs=[
            pl.BlockSpec(memory_space=pl.ANY),
            pl.BlockSpec((TM_FF, D_MODEL), lambda i, f: (i, 0)),
            pl.BlockSpec((None, 6, D_MODEL), lambda i, f: (i // tiles_per_seq, 0, 0)),
            pl.BlockSpec((None, D_MODEL, TF), lambda i, f: (f, 0, 0)),
            pl.BlockSpec((None, D_MODEL, TF), lambda i, f: (f, 0, 0)),
            pl.BlockSpec((TF, D_MODEL), lambda i, f: (f, 0)),
            _resident((nf, 3, TF)),
            _resident((nf, 1, TF)),
            _resident((1, D_MODEL)),
        ],
        out_specs=(
            pl.BlockSpec((TM_FF, D_MODEL), lambda i, f: (i, 0)),
            pl.BlockSpec((None, nf, V7X_SUBLANES, TF), lambda i, f: (i, 0, 0, 0)),
        ),
        scratch_shapes=[
            pltpu.VMEM((TM_FF, D_MODEL), F32),
            pltpu.VMEM((TM_FF + V7X_SUBLANES, TF), F32),
            pltpu.VMEM((nf, V7X_SUBLANES, TF), F32),
            pltpu.SemaphoreType.DMA(()),
        ],
        compiler_params=_params("arbitrary", "arbitrary"),
        name="conv_ffn",
    )(x, h2, mod, wg, wu, wd, cw, cb, g_final)


def _s_in_proj_kernel(x_ref, mod_ref, g1_ref, w_ref, z_ref, wb_ref, h_ref):
    @pl.when(pl.program_id(0) == 0)
    def _():
        h_ref[...] = (_rms(x_ref[...], g1_ref[...]) * (1.0 + mod_ref[1]) + mod_ref[0]).astype(BF16)

    w = w_ref[...].astype(BF16)
    wb_ref[...] = w
    z_ref[...] = _dot(h_ref[...], w)


def _s_in_proj(layer, x, mod, g1, w_in):
    rows = x.shape[0]
    full = lambda shape: pl.BlockSpec(shape, lambda j: (0,) * len(shape))
    return pl.pallas_call(
        _s_in_proj_kernel,
        out_shape=(jax.ShapeDtypeStruct((rows, IN_COLS), F32), jax.ShapeDtypeStruct((D_MODEL, IN_COLS), BF16)),
        grid=(IN_COLS // TN_IN,),
        in_specs=[
            full((rows, D_MODEL)), full((6, rows, D_MODEL)), full((1, D_MODEL)),
            pl.BlockSpec((None, D_MODEL, TN_IN), lambda j: (layer, 0, j)),
        ],
        out_specs=(pl.BlockSpec((rows, TN_IN), lambda j: (0, j)), pl.BlockSpec((D_MODEL, TN_IN), lambda j: (0, j))),
        scratch_shapes=[pltpu.VMEM((rows, D_MODEL), BF16)],
        compiler_params=_params("arbitrary"),
        name="sample_in_proj",
    )(x, mod, g1, w_in)


def _s_attn_kernel(sink_ref, z_ref, cos_ref, sin_ref, gain_ref, ws0_ref, bs0_ref, kt_ref, vt_ref,
                   attn_ref, gm_ref, gvn_ref, ktn_ref, vtn_ref):
    nseq = z_ref.shape[0]
    q_blocks, k_blocks, u, gvn = _post_project(
        z_ref[:, Q0:K0], z_ref[:, K0:V0], z_ref[:, U0:G0], z_ref[:, G0:IN_COLS],
        cos_ref[...], sin_ref[...], gain_ref[...], pair_q=False)
    v_new = z_ref[:, V0:U0]
    k_new = jnp.concatenate(k_blocks, axis=1)
    gvn_ref[...] = gvn
    gm_ref[...] = (u * (ws0_ref[...].astype(BF16).astype(F32) * gvn.astype(BF16).astype(F32) + bs0_ref[...])).astype(BF16)

    rows = GQA_GROUP * nseq
    keys = nseq * WINDOW
    seq_of_row = jnp.bitwise_and(_iota((rows, 1), 0), nseq - 1)
    head_of_row = lax.shift_right_logical(_iota((rows, 1), 0), nseq.bit_length() - 1)
    own_keys = lax.shift_right_logical(_iota((rows, keys), 1), WINDOW.bit_length() - 1) == seq_of_row
    own_new = _iota((rows, nseq), 1) == seq_of_row
    pieces = []
    for g in range(N_KV_HEADS):
        gs = slice(g * HEAD_DIM, (g + 1) * HEAD_DIM)
        heads = [g * GQA_GROUP + r for r in range(GQA_GROUP)]
        qg = jnp.concatenate(
            [q_blocks[h // 2][:, (h % 2) * HEAD_DIM:(h % 2 + 1) * HEAD_DIM] for h in heads], axis=0).astype(BF16)
        ktg = jnp.concatenate([kt_ref[b, gs, :] for b in range(nseq)], axis=1).astype(BF16)
        vtg = jnp.concatenate([vt_ref[b, gs, :] for b in range(nseq)], axis=1).astype(BF16)
        s = jnp.where(own_keys, _dot(qg, ktg), NEG)
        sn = jnp.where(own_new, _dot_nt(qg, k_new[:, gs].astype(BF16)), NEG)
        sk = jnp.full((rows, 1), sink_ref[heads[0]], F32)
        for r in range(1, GQA_GROUP):
            sk = jnp.where(head_of_row == r, sink_ref[heads[r]], sk)
        m = jnp.maximum(jnp.maximum(jnp.max(s, axis=-1, keepdims=True), jnp.max(sn, axis=-1, keepdims=True)), sk)
        p = jnp.exp(s - m)
        pn = jnp.exp(sn - m)
        denom = jnp.sum(p, axis=-1, keepdims=True) + jnp.sum(pn, axis=-1, keepdims=True) + jnp.exp(sk - m)
        o = (_dot_nt(p.astype(BF16), vtg) + _dot(pn.astype(BF16), v_new[:, gs].astype(BF16))) / denom
        pieces += [o[r * nseq:(r + 1) * nseq] for r in range(GQA_GROUP)]
    attn_ref[...] = jnp.concatenate(pieces, axis=1).astype(BF16)

    pad = jnp.zeros((V7X_LANES - nseq, KV_WIDTH), F32)
    knt = jnp.concatenate([k_new, pad], axis=0).T
    vnt = jnp.concatenate([v_new, pad], axis=0).T
    newest = _iota((1, WINDOW), 1) == WINDOW - 1
    for b in range(nseq):
        ktn_ref[b] = jnp.where(newest, knt[:, b:b + 1], pltpu.roll(kt_ref[b], WINDOW - 1, axis=1))
        vtn_ref[b] = jnp.where(newest, vnt[:, b:b + 1], pltpu.roll(vt_ref[b], WINDOW - 1, axis=1))


def _s_attention(layer, sinks, z, cos, sin_signed, gain, ws0, bs0, kt_all, vt_all):
    rows = z.shape[0]
    full = lambda shape: pl.BlockSpec(shape, lambda i: (0,) * len(shape))
    cache = pl.BlockSpec((None, rows, KV_WIDTH, WINDOW), lambda i: (layer, 0, 0, 0))
    return pl.pallas_call(
        _s_attn_kernel,
        out_shape=(
            jax.ShapeDtypeStruct((rows, ATTN_WIDTH), BF16),
            jax.ShapeDtypeStruct((rows, GM_WIDTH), BF16),
            jax.ShapeDtypeStruct((rows, GM_WIDTH), F32),
            jax.ShapeDtypeStruct((rows, KV_WIDTH, WINDOW), F32),
            jax.ShapeDtypeStruct((rows, KV_WIDTH, WINDOW), F32),
        ),
        grid=(1,),
        in_specs=[
            pl.BlockSpec(memory_space=pltpu.SMEM),
            full((rows, IN_COLS)), full((1, V7X_LANES)), full((1, V7X_LANES)),
            full((1, GM_WIDTH)), full((1, GM_WIDTH)), full((1, GM_WIDTH)),
            cache, cache,
        ],
        out_specs=(full((rows, ATTN_WIDTH)), full((rows, GM_WIDTH)), full((rows, GM_WIDTH)),
                   full((rows, KV_WIDTH, WINDOW)), full((rows, KV_WIDTH, WINDOW))),
        compiler_params=_params("arbitrary"),
        name="sample_attention",
    )(sinks, z, cos, sin_signed, gain, ws0, bs0, kt_all, vt_all)


def _paired_head(slot):
    p, rest = divmod(slot, 2 * GQA_GROUP)
    j, i = divmod(rest, 2)
    return p * 2 * GQA_GROUP + i * GQA_GROUP + j


def _s_mix_out_kernel(x_ref, gate_ref, a_ref, gm_ref, w_ref, o_ref, wb_ref):
    w = w_ref[...].astype(BF16)
    for slot in range(N_HEADS):
        head = _paired_head(slot)
        wb_ref[slot * HEAD_DIM:(slot + 1) * HEAD_DIM] = w[head * HEAD_DIM:(head + 1) * HEAD_DIM]
    wb_ref[ATTN_WIDTH:] = w[ATTN_WIDTH:]
    mix = _dot(a_ref[...], w[:ATTN_WIDTH]) + _dot(gm_ref[...], w[ATTN_WIDTH:])
    o_ref[...] = x_ref[...] + gate_ref[...] * mix


def _s_mix_out(layer, x, gate, attn, gm, w_out):
    rows = x.shape[0]
    full = lambda shape: pl.BlockSpec(shape, lambda j: (0,) * len(shape))
    col_tile = lambda r: pl.BlockSpec((r, TN_OUT), lambda j: (0, j))
    return pl.pallas_call(
        _s_mix_out_kernel,
        out_shape=(jax.ShapeDtypeStruct((rows, D_MODEL), F32), jax.ShapeDtypeStruct((D_MODEL, D_MODEL), BF16)),
        grid=(D_MODEL // TN_OUT,),
        in_specs=[
            col_tile(rows), col_tile(rows), full((rows, ATTN_WIDTH)), full((rows, GM_WIDTH)),
            pl.BlockSpec((None, D_MODEL, TN_OUT), lambda j: (layer, 0, j)),
        ],
        out_specs=(col_tile(rows), col_tile(D_MODEL)),
        compiler_params=_params("arbitrary"),
        name="sample_mix_out",
    )(x, gate, attn, gm, w_out)


def _s_ffn_kernel(x_ref, mod_ref, g2_ref, wg_ref, wu_ref, wd_ref, cw_ref, cb_ref, p0_ref, p1_ref, gf_ref,
                  o_ref, a_ref, wgb_ref, wub_ref, wdb_ref, h2_ref, acc_ref, *, final_norm):
    f = pl.program_id(0)

    @pl.when(f == 0)
    def _():
        h2_ref[...] = (_rms(x_ref[...], g2_ref[...]) * (1.0 + mod_ref[4]) + mod_ref[3]).astype(BF16)
        acc_ref[...] = jnp.zeros_like(acc_ref)

    wg, wu, wd = wg_ref[...].astype(BF16), wu_ref[...].astype(BF16), wd_ref[...].astype(BF16)
    wgb_ref[...] = wg
    wub_ref[...] = wu
    wdb_ref[...] = wd
    h2 = h2_ref[...]
    a = _dot(h2, wg)
    up = _dot(h2, wu)
    a_ref[...] = a
    cw = cw_ref[f]
    conv = cb_ref[f] + cw[0:1] * p0_ref[...] + cw[1:2] * p1_ref[...] + cw[2:3] * a
    acc_ref[...] += _dot((jax.nn.silu(conv) * up).astype(BF16), wd)

    @pl.when(f == pl.num_programs(0) - 1)
    def _():
        x2 = x_ref[...] + mod_ref[5] * acc_ref[...]
        o_ref[...] = _rms(x2, gf_ref[...]) if final_norm else x2


def _s_ffn(layer, x, mod, g2, w_gate, w_up, w_down, cw, cb, p0, p1, g_final, final_norm):
    rows = x.shape[0]
    nf = D_FF // TF
    full = lambda shape: pl.BlockSpec(shape, lambda f: (0,) * len(shape))
    return pl.pallas_call(
        functools.partial(_s_ffn_kernel, final_norm=final_norm),
        out_shape=(
            jax.ShapeDtypeStruct((rows, D_MODEL), F32), jax.ShapeDtypeStruct((rows, D_FF), F32),
            jax.ShapeDtypeStruct((nf, D_MODEL, TF), BF16), jax.ShapeDtypeStruct((nf, D_MODEL, TF), BF16),
            jax.ShapeDtypeStruct((D_FF, D_MODEL), BF16),
        ),
        grid=(nf,),
        in_specs=[
            full((rows, D_MODEL)), full((6, rows, D_MODEL)), full((1, D_MODEL)),
            pl.BlockSpec((None, D_MODEL, TF), lambda f: (layer, 0, f)),
            pl.BlockSpec((None, D_MODEL, TF), lambda f: (layer, 0, f)),
            pl.BlockSpec((None, TF, D_MODEL), lambda f: (layer, f, 0)),
            full((nf, 3, TF)), full((nf, 1, TF)),
            pl.BlockSpec((rows, TF), lambda f: (0, f)),
            pl.BlockSpec((rows, TF), lambda f: (0, f)),
            full((1, D_MODEL)),
        ],
        out_specs=(
            full((rows, D_MODEL)), pl.BlockSpec((rows, TF), lambda f: (0, f)),
            pl.BlockSpec((None, D_MODEL, TF), lambda f: (f, 0, 0)), pl.BlockSpec((None, D_MODEL, TF), lambda f: (f, 0, 0)),
            pl.BlockSpec((TF, D_MODEL), lambda f: (f, 0)),
        ),
        scratch_shapes=[pltpu.VMEM((rows, D_MODEL), BF16), pltpu.VMEM((rows, D_MODEL), F32)],
        compiler_params=_params("arbitrary"),
        name="sample_conv_ffn",
    )(x, mod, g2, w_gate, w_up, w_down, cw, cb, p0, p1, g_final)


def _rope_tables(pos):
    half = HEAD_DIM // 2
    inv = ROPE_THETA ** (-jnp.arange(half, dtype=jnp.float32) / half)
    ang = pos.astype(jnp.float32)[:, None] * inv[None, :]
    reps = V7X_LANES // half
    sign = jnp.where((jnp.arange(V7X_LANES) % HEAD_DIM) < half, -1.0, 1.0).astype(F32)
    return jnp.tile(jnp.cos(ang), (1, reps)), jnp.tile(jnp.sin(ang), (1, reps)) * sign[None, :]


def _to_planes(window_buf):
    lead = window_buf.shape[:-3]
    return jnp.moveaxis(window_buf, -3, -1).reshape(lead + (KV_WIDTH, WINDOW))


def _from_planes(planes):
    lead = planes.shape[:-2]
    return jnp.moveaxis(planes.reshape(lead + (N_KV_HEADS, HEAD_DIM, WINDOW)), -1, -3)


def kernel(x_prompt, x_sample, cache_k, cache_v, state_conv, c_prompt, c_sample, w_ada, b_ada, g_norm1, g_norm2,
           w_in, gm_gain, gm_ws, gm_bs, sinks, w_out, w_gate, w_up, conv_w, conv_b, w_down, g_final):
    batch, seq, _ = x_prompt.shape
    dec = x_sample.shape[0]
    nf = D_FF // TF
    assert x_sample.shape[1] == 1 and seq % TM == 0 and seq % TQ == 0 and TM % CHUNK == 0 and D_FF % TF == 0
    assert seq % TM_FF == 0 and TM_FF % FFN_SUBBLOCKS == 0
    assert dec & (dec - 1) == 0 and dec <= V7X_LANES

    bst = jnp.swapaxes(gm_bs, 1, 2)
    ws0 = jnp.repeat(gm_ws[:, :, 0, 0], GM_WIDTH // GM_HEADS, axis=1)[:, None, :]
    bs0 = jnp.repeat(gm_bs[:, :, 0], GM_WIDTH // GM_HEADS, axis=1)[:, None, :]
    cw_tiles = jnp.swapaxes(conv_w.reshape(DEPTH, 3, nf, TF), 1, 2)
    cb_tiles = conv_b.reshape(DEPTH, nf, 1, TF)

    cos_p, sin_p = _rope_tables(jnp.arange(seq, dtype=jnp.int32))
    cos_s, sin_s = _rope_tables(PAST_LEN + jnp.arange(1, dtype=jnp.int32))

    c_all = jnp.concatenate([c_prompt, c_sample, jnp.zeros((MOD_ROWS - batch - dec, D_MODEL), F32)], axis=0)
    mod = _modulation(c_all, w_ada, b_ada)

    kt_all, vt_all = _to_planes(cache_k), _to_planes(cache_v)

    hp = x_prompt.reshape(batch * seq, D_MODEL)
    hs = x_sample.reshape(dec, D_MODEL)
    g_fin = g_final.reshape(1, D_MODEL)
    outs = [[] for _ in range(8)]
    for l in range(DEPTH):
        last = l == DEPTH - 1
        g1 = g_norm1[l].reshape(1, D_MODEL)
        g2 = g_norm2[l].reshape(1, D_MODEL)
        gain = gm_gain[l].reshape(1, GM_WIDTH)
        mod_p = jnp.swapaxes(mod[l, :, :batch], 0, 1)
        mod_s = mod[l, :, batch:batch + dec]

        z, w_in_b = _s_in_proj(l, hs, mod_s, g1, w_in)
        attn_s, gm_s, gvn_s, kt_new, vt_new = _s_attention(
            l, sinks[l], z, cos_s, sin_s, gain, ws0[l], bs0[l], kt_all, vt_all)
        hs, w_out_b = _s_mix_out(l, hs, mod_s[2], attn_s, gm_s, w_out)
        hs, a_new, wg_b, wu_b, wd_b = _s_ffn(l, hs, mod_s, g2, w_gate, w_up, w_down, cw_tiles[l], cb_tiles[l],
                                             state_conv[l, :, 0], state_conv[l, :, 1], g_fin, last)
        outs[4].append(kt_new)
        outs[5].append(vt_new)
        outs[6].append(gvn_s[:, None, :])
        outs[7].append(jnp.stack([state_conv[l, :, 1], a_new], axis=1))

        q, k, v, gm, gvl = _mix_in(hp, mod_p, g1, w_in_b, cos_p, sin_p, gain, gm_ws[l], bst[l], seq)
        attn, kt_p, vt_p = _attention(sinks[l], q, k, v, seq)
        hp, h2 = _mix_out(hp, mod_p, g2, attn, gm, w_out_b, seq)
        hp, tail = _ffn(hp, h2, mod_p, wg_b, wu_b, wd_b, cw_tiles[l], cb_tiles[l], g_fin, seq, last)
        outs[0].append(kt_p)
        outs[1].append(vt_p)
        outs[2].append(gvl)
        tails = tail[seq // TM_FF - 1::seq // TM_FF, :, V7X_SUBLANES - 2:, :]
        outs[3].append(jnp.swapaxes(tails, 1, 2).reshape(batch, 2, D_FF))

    st = [jnp.stack(o) for o in outs]
    return (hp.reshape(batch, seq, D_MODEL), hs.reshape(dec, 1, D_MODEL),
            _from_planes(st[0]), _from_planes(st[1]), st[2], st[3],
            _from_planes(st[4]), _from_planes(st[5]), st[6], st[7])
```

```python
import functools

import jax
import jax.numpy as jnp
from jax import lax
from jax.experimental import pallas as pl
from jax.experimental.pallas import tpu as pltpu

D_MODEL = 2048
DEPTH = 2
PAST_LEN = 16384
ATTN_WIDTH = 1024
GM_WIDTH = 1024
HEAD_DIM = 64
N_HEADS = 16
N_KV_HEADS = 4
GQA_GROUP = N_HEADS // N_KV_HEADS
WINDOW = 128
ROPE_THETA = 10000.0
CHUNK = 128
GM_HEADS = 8
D_FF = 5632
KV_WIDTH = N_KV_HEADS * HEAD_DIM
IN_COLS = ATTN_WIDTH + 2 * KV_WIDTH + 2 * GM_WIDTH
EPS = 1e-6
NEG = -1e30

Q0, K0, V0, U0, G0 = 0, ATTN_WIDTH, ATTN_WIDTH + KV_WIDTH, ATTN_WIDTH + 2 * KV_WIDTH, ATTN_WIDTH + 2 * KV_WIDTH + GM_WIDTH

V7X_LANES = 128
V7X_SUBLANES = 8
V7X_VMEM_LIMIT_BYTES = 56 * 1024 * 1024

TM = 512
TF = 512
TM_FF = 1024
TF_HEAD = 256
TQ = 512
TN_IN = 512
TN_OUT = 512
MOD_TN = 1024
MOD_ROWS = 48
FFN_SUBBLOCKS = 4
MIX_OUT_SUBBLOCKS = 2

BF16 = jnp.bfloat16
F32 = jnp.float32


def _params(*semantics):
    return pltpu.CompilerParams(dimension_semantics=semantics, vmem_limit_bytes=V7X_VMEM_LIMIT_BYTES)


def _dot(a, b):
    return jnp.dot(a, b, preferred_element_type=F32)


def _dot_nt(a, b):
    return lax.dot_general(a, b, (((1,), (1,)), ((), ())), preferred_element_type=F32)


def _rms(x, g):
    return x * lax.rsqrt(jnp.mean(x * x, axis=-1, keepdims=True) + EPS) * g


def _resident(shape):
    zeros = (0,) * len(shape)
    return pl.BlockSpec(shape, lambda *_: zeros, pipeline_mode=pl.Buffered(1))


def _iota(shape, dim):
    return lax.broadcasted_iota(jnp.int32, shape, dim)


def _mod_kernel(c_ref, w_ref, b_ref, o_ref):
    s = jax.nn.silu(c_ref[...]).astype(BF16)
    o_ref[...] = _dot(s, w_ref[...].astype(BF16)) + b_ref[...]


def _modulation(c_all, w_ada, b_ada):
    n_tiles = (6 * D_MODEL) // MOD_TN
    per_chunk = D_MODEL // MOD_TN
    return pl.pallas_call(
        _mod_kernel,
        out_shape=jax.ShapeDtypeStruct((DEPTH, 6, MOD_ROWS, D_MODEL), F32),
        grid=(DEPTH, n_tiles),
        in_specs=[
            pl.BlockSpec((MOD_ROWS, D_MODEL), lambda l, j: (0, 0)),
            pl.BlockSpec((None, D_MODEL, MOD_TN), lambda l, j: (l, 0, j)),
            pl.BlockSpec((None, 1, MOD_TN), lambda l, j: (l, 0, j)),
        ],
        out_specs=pl.BlockSpec((None, None, MOD_ROWS, MOD_TN), lambda l, j: (l, j // per_chunk, 0, j % per_chunk)),
        compiler_params=_params("arbitrary", "arbitrary"),
        name="adaln_modulation",
    )(c_all, w_ada, b_ada.reshape(DEPTH, 1, 6 * D_MODEL))


def _rope_cols(z, cos, sin_signed):
    lane = _iota((1, V7X_LANES), 1)
    first_half = jnp.bitwise_and(lane, HEAD_DIM - 1) < HEAD_DIM // 2
    outs = []
    for c in range(z.shape[1] // V7X_LANES):
        zc = z[:, c * V7X_LANES:(c + 1) * V7X_LANES]
        partner = jnp.where(first_half,
                            pltpu.roll(zc, V7X_LANES - HEAD_DIM // 2, axis=1),
                            pltpu.roll(zc, HEAD_DIM // 2, axis=1))
        outs.append(zc * cos + partner * sin_signed)
    return outs


def _swap_halves(x):
    return pltpu.roll(x, HEAD_DIM, axis=1)


def _pair_q_blocks(nat):
    low = _iota((1, V7X_LANES), 1) < HEAD_DIM
    out = []
    for c in range(len(nat)):
        p, j = divmod(c, 4)
        a, b = nat[p * 4 + j // 2], nat[p * 4 + 2 + j // 2]
        out.append(jnp.where(low, a, _swap_halves(b)) if j % 2 == 0 else jnp.where(low, _swap_halves(a), b))
    return out


def _post_project(zq, zk, zu, zg, cos, sin_signed, gain, pair_q):
    q_blocks = _rope_cols(zq, cos, sin_signed)
    if pair_q:
        q_blocks = _pair_q_blocks(q_blocks)
    q_blocks = [qb * (HEAD_DIM ** -0.5) for qb in q_blocks]
    k_blocks = _rope_cols(zk, cos, sin_signed)
    u = jax.nn.gelu(zu)
    gvn = _rms(jax.nn.gelu(zg), gain)
    return q_blocks, k_blocks, u, gvn


def _store_cols(ref, blocks):
    for c, blk in enumerate(blocks):
        ref[:, c * V7X_LANES:(c + 1) * V7X_LANES] = blk.astype(ref.dtype)


def _mix_in_kernel(x_ref, mod_ref, g1_ref, w_ref, cos_ref, sin_ref, gain_ref, ws_ref, bst_ref,
                   q_ref, k_ref, v_ref, gm_ref, gvl_ref):
    tm = x_ref.shape[0]
    h = (_rms(x_ref[...], g1_ref[...]) * (1.0 + mod_ref[1:2]) + mod_ref[0:1]).astype(BF16)
    q_blocks, k_blocks, u, gvn = _post_project(
        _dot(h, w_ref[:, Q0:K0]), _dot(h, w_ref[:, K0:V0]), _dot(h, w_ref[:, U0:G0]), _dot(h, w_ref[:, G0:IN_COLS]),
        cos_ref[...], sin_ref[...], gain_ref[...], pair_q=True)
    _store_cols(q_ref, q_blocks)
    _store_cols(k_ref, k_blocks)
    v_ref[...] = _dot(h, w_ref[:, V0:U0])
    gvl_ref[...] = gvn[tm - CHUNK:, :]
    gvb = gvn.astype(BF16)
    causal = _iota((CHUNK, CHUNK), 1) <= _iota((CHUNK, CHUNK), 0)
    for hh in range(GM_HEADS):
        wm = jnp.where(causal, ws_ref[hh], 0.0).astype(BF16)
        bias = bst_ref[:, hh:hh + 1]
        cs = slice(hh * CHUNK, (hh + 1) * CHUNK)
        for c in range(tm // CHUNK):
            rs = slice(c * CHUNK, (c + 1) * CHUNK)
            s = _dot(wm, gvb[rs, cs]) + bias
            gm_ref[rs, cs] = (u[rs, cs] * s).astype(BF16)


def _mix_in(x, mod, g1, w_in, cos, sin_signed, gain, ws, bst, seq):
    tokens = x.shape[0]
    tiles_per_seq = seq // TM
    batch = tokens // seq
    row_tile = lambda width: pl.BlockSpec((TM, width), lambda i: (i, 0))
    return pl.pallas_call(
        _mix_in_kernel,
        out_shape=(
            jax.ShapeDtypeStruct((tokens, ATTN_WIDTH), BF16),
            jax.ShapeDtypeStruct((tokens, KV_WIDTH), F32),
            jax.ShapeDtypeStruct((tokens, KV_WIDTH), F32),
            jax.ShapeDtypeStruct((tokens, GM_WIDTH), BF16),
            jax.ShapeDtypeStruct((batch, CHUNK, GM_WIDTH), F32),
        ),
        grid=(tokens // TM,),
        in_specs=[
            row_tile(D_MODEL),
            pl.BlockSpec((None, 6, D_MODEL), lambda i: (i // tiles_per_seq, 0, 0)),
            _resident((1, D_MODEL)),
            _resident((D_MODEL, IN_COLS)),
            pl.BlockSpec((TM, V7X_LANES), lambda i: (i % tiles_per_seq, 0)),
            pl.BlockSpec((TM, V7X_LANES), lambda i: (i % tiles_per_seq, 0)),
            _resident((1, GM_WIDTH)),
            _resident((GM_HEADS, CHUNK, CHUNK)),
            _resident((CHUNK, GM_HEADS)),
        ],
        out_specs=(
            row_tile(ATTN_WIDTH), row_tile(KV_WIDTH), row_tile(KV_WIDTH), row_tile(GM_WIDTH),
            pl.BlockSpec((None, CHUNK, GM_WIDTH), lambda i: (i // tiles_per_seq, 0, 0)),
        ),
        compiler_params=_params("arbitrary"),
        name="mix_in",
    )(x, mod, g1, w_in, cos, sin_signed, gain, ws, bst)


def _attn_block(n, sink_ref, q_blk, k_prev, k_cur, v_prev, v_cur, out_blk):
    row = _iota((WINDOW, 2 * WINDOW), 0)
    key = _iota((WINDOW, 2 * WINDOW), 1) - WINDOW
    valid = (key <= row) & (key >= jnp.maximum(row - WINDOW, -n * WINDOW))
    low = _iota((WINDOW, V7X_LANES), 1) < HEAD_DIM
    cols_per_pair = ATTN_WIDTH // V7X_LANES // 2
    for p in range(2):
        ls = slice(p * V7X_LANES, (p + 1) * V7X_LANES)
        kcat = jnp.concatenate([k_prev[:, ls], k_cur[:, ls]], axis=0).astype(BF16)
        vcat = jnp.concatenate([v_prev[:, ls], v_cur[:, ls]], axis=0).astype(BF16)
        blocks = []
        for cc in range(cols_per_pair):
            c = p * cols_per_pair + cc
            qcol = q_blk[:, c * V7X_LANES:(c + 1) * V7X_LANES]
            zero = jnp.zeros_like(qcol)
            blocks += [jnp.where(low, qcol, zero), jnp.where(low, zero, qcol)]
        s = _dot_nt(jnp.concatenate(blocks, axis=0), kcat)
        probs, inv = [], []
        for j in range(2 * cols_per_pair):
            cc, e = divmod(j, 2)
            sj = jnp.where(valid, s[j * WINDOW:(j + 1) * WINDOW], NEG)
            sk = sink_ref[p * 2 * GQA_GROUP + e * GQA_GROUP + cc]
            m = jnp.maximum(jnp.max(sj, axis=-1, keepdims=True), sk)
            pj = jnp.exp(sj - m)
            inv.append(1.0 / (jnp.sum(pj, axis=-1, keepdims=True) + jnp.exp(sk - m)))
            probs.append(pj.astype(BF16))
        o = _dot(jnp.concatenate(probs, axis=0), vcat)
        for cc in range(cols_per_pair):
            c = p * cols_per_pair + cc
            even = o[(2 * cc) * WINDOW:(2 * cc + 1) * WINDOW] * inv[2 * cc]
            odd = o[(2 * cc + 1) * WINDOW:(2 * cc + 2) * WINDOW] * inv[2 * cc + 1]
            out_blk[:, c * V7X_LANES:(c + 1) * V7X_LANES] = jnp.where(low, even, odd).astype(BF16)


def _attn_kernel(sink_ref, q_ref, kp_ref, kc_ref, vp_ref, vc_ref, o_ref, kt_ref, vt_ref):
    step = pl.program_id(1)
    blocks = q_ref.shape[0] // WINDOW
    for t in range(blocks):
        rows = slice(t * WINDOW, (t + 1) * WINDOW)
        prev = slice((t - 1) * WINDOW, t * WINDOW)
        k_prev = kp_ref[...] if t == 0 else kc_ref[prev]
        v_prev = vp_ref[...] if t == 0 else vc_ref[prev]
        _attn_block(step * blocks + t, sink_ref, q_ref.at[rows], k_prev, kc_ref[rows], v_prev, vc_ref[rows], o_ref.at[rows])

    @pl.when(step == pl.num_programs(1) - 1)
    def _():
        last = slice((blocks - 1) * WINDOW, blocks * WINDOW)
        kt_ref[...] = kc_ref[last].T
        vt_ref[...] = vc_ref[last].T


def _attention(sinks, q, k, v, seq):
    tokens = q.shape[0]
    batch = tokens // seq
    steps = seq // TQ
    per_step = TQ // WINDOW
    cur = lambda b, n: (b * steps + n, 0)
    prev = lambda b, n: (jnp.maximum((b * steps + n) * per_step - 1, 0), 0)
    state = pl.BlockSpec((None, KV_WIDTH, WINDOW), lambda b, n: (b, 0, 0))
    return pl.pallas_call(
        _attn_kernel,
        out_shape=(
            jax.ShapeDtypeStruct((tokens, ATTN_WIDTH), BF16),
            jax.ShapeDtypeStruct((batch, KV_WIDTH, WINDOW), F32),
            jax.ShapeDtypeStruct((batch, KV_WIDTH, WINDOW), F32),
        ),
        grid=(batch, steps),
        in_specs=[
            pl.BlockSpec(memory_space=pltpu.SMEM),
            pl.BlockSpec((TQ, ATTN_WIDTH), cur),
            pl.BlockSpec((WINDOW, KV_WIDTH), prev),
            pl.BlockSpec((TQ, KV_WIDTH), cur),
            pl.BlockSpec((WINDOW, KV_WIDTH), prev),
            pl.BlockSpec((TQ, KV_WIDTH), cur),
        ],
        out_specs=(pl.BlockSpec((TQ, ATTN_WIDTH), cur), state, state),
        compiler_params=_params("arbitrary", "arbitrary"),
        name="swa_attention",
    )(sinks, q, k, k, v, v)


def _mix_out_kernel(x_ref, mod_ref, g2_ref, a_ref, gm_ref, wa_ref, wb_ref, o_ref, h2_ref):
    sub = x_ref.shape[0] // MIX_OUT_SUBBLOCKS
    for r in range(MIX_OUT_SUBBLOCKS):
        rs = slice(r * sub, (r + 1) * sub)
        mix = _dot(a_ref[rs], wa_ref[...]) + _dot(gm_ref[rs], wb_ref[...])
        x1 = x_ref[rs] + mod_ref[2:3] * mix
        o_ref[rs] = x1
        h2_ref[rs] = (_rms(x1, g2_ref[...]) * (1.0 + mod_ref[4:5]) + mod_ref[3:4]).astype(BF16)


def _mix_out(x, mod, g2, attn, gm, w_out, seq):
    tokens = x.shape[0]
    tiles_per_seq = seq // TM
    row_tile = lambda width: pl.BlockSpec((TM, width), lambda i: (i, 0))
    half = lambda k: pl.BlockSpec((ATTN_WIDTH, D_MODEL), lambda i: (k, 0), pipeline_mode=pl.Buffered(1))
    return pl.pallas_call(
        _mix_out_kernel,
        out_shape=(jax.ShapeDtypeStruct((tokens, D_MODEL), F32), jax.ShapeDtypeStruct((tokens, D_MODEL), BF16)),
        grid=(tokens // TM,),
        in_specs=[
            row_tile(D_MODEL),
            pl.BlockSpec((None, 6, D_MODEL), lambda i: (i // tiles_per_seq, 0, 0)),
            _resident((1, D_MODEL)),
            row_tile(ATTN_WIDTH), row_tile(GM_WIDTH),
            half(0), half(1),
        ],
        out_specs=(row_tile(D_MODEL), row_tile(D_MODEL)),
        compiler_params=_params("arbitrary"),
        name="mix_out",
    )(x, mod, g2, attn, gm, w_out, w_out)


def _ffn_tile(h2_ref, ext_ref, acc_ref, wg, wu, wd, cw, cb):
    tm = h2_ref.shape[0]
    sub = tm // FFN_SUBBLOCKS
    ups = []
    for r in range(FFN_SUBBLOCKS):
        h2 = h2_ref[r * sub:(r + 1) * sub]
        ext_ref[V7X_SUBLANES + r * sub:V7X_SUBLANES + (r + 1) * sub] = _dot(h2, wg)
        ups.append(_dot(h2, wu))
    for r in range(FFN_SUBBLOCKS):
        lo = V7X_SUBLANES + r * sub
        conv = (cb + cw[0:1] * ext_ref[lo - 2:lo - 2 + sub]
                + cw[1:2] * ext_ref[lo - 1:lo - 1 + sub] + cw[2:3] * ext_ref[lo:lo + sub])
        y = (jax.nn.silu(conv) * ups[r]).astype(BF16)
        acc_ref[r * sub:(r + 1) * sub] += _dot(y, wd)


def _ffn_finish(o_ref, acc_ref, gate, gf_ref, final_norm):
    sub = o_ref.shape[0] // FFN_SUBBLOCKS

    def finish(r, carry):
        rs = pl.ds(pl.multiple_of(r * sub, sub), sub)
        x2 = o_ref[rs, :] + gate * acc_ref[rs, :]
        o_ref[rs, :] = _rms(x2, gf_ref[...]) if final_norm else x2
        return carry

    lax.fori_loop(0, FFN_SUBBLOCKS, finish, 0)


def _ffn_kernel(x_hbm, done_hbm, h2_ref, mod_ref, wg_ref, wu_ref, wd_ref, cw_ref, cb_ref, halo_ref, gf_ref,
                o_ref, tail_ref, acc_ref, ext_ref, carry_ref, x_sem, *, first_tile, tiles_per_seq, final_norm):
    del done_hbm
    i = pl.program_id(0)
    f = pl.program_id(1)
    tm = h2_ref.shape[0]
    tile = i + first_tile
    last = f == pl.num_programs(1) - 1
    x_copy = pltpu.make_async_copy(x_hbm.at[pl.ds(pl.multiple_of(tile * tm, tm), tm)], o_ref, x_sem)

    @pl.when(last)
    def _():
        x_copy.start()

    @pl.when(f == 0)
    def _():
        acc_ref[...] = jnp.zeros_like(acc_ref)

    seq_start = (tile % tiles_per_seq) == 0

    @pl.when(seq_start)
    def _():
        ext_ref[0:V7X_SUBLANES] = jnp.zeros((V7X_SUBLANES, ext_ref.shape[1]), F32)

    @pl.when(jnp.logical_not(seq_start) & (i == 0))
    def _():
        ext_ref[0:V7X_SUBLANES] = halo_ref[...]

    @pl.when(jnp.logical_not(seq_start) & (i > 0))
    def _():
        ext_ref[0:V7X_SUBLANES] = carry_ref[f]

    _ffn_tile(h2_ref, ext_ref, acc_ref, wg_ref[...], wu_ref[...], wd_ref[...], cw_ref[f], cb_ref[f])
    tail = ext_ref[tm:tm + V7X_SUBLANES]
    carry_ref[f] = tail
    tail_ref[f] = tail

    @pl.when(last)
    def _():
        x_copy.wait()
        _ffn_finish(o_ref, acc_ref, mod_ref[5:6], gf_ref, final_norm)


def _ffn(x, done, halo, h2, mod, wg, wu, wd, cw, cb, g_final, seq, final_norm):
    tokens = x.shape[0]
    tiles_per_seq = seq // TM_FF
    nf = D_FF // TF
    first_tile = 1
    kern = functools.partial(_ffn_kernel, first_tile=first_tile, tiles_per_seq=tiles_per_seq, final_norm=final_norm)
    return pl.pallas_call(
        kern,
        out_shape=(
            jax.ShapeDtypeStruct((tokens, D_MODEL), F32),
            jax.ShapeDtypeStruct((tokens // TM_FF - first_tile, nf, V7X_SUBLANES, TF), F32),
        ),
        grid=(tokens // TM_FF - first_tile, nf),
        in_specs=[
            pl.BlockSpec(memory_space=pl.ANY),
            pl.BlockSpec(memory_space=pl.ANY),
            pl.BlockSpec((TM_FF, D_MODEL), lambda i, f: (i + first_tile, 0)),
            pl.BlockSpec((None, 6, D_MODEL), lambda i, f: ((i + first_tile) // tiles_per_seq, 0, 0)),
            pl.BlockSpec((None, D_MODEL, TF), lambda i, f: (f, 0, 0)),
            pl.BlockSpec((None, D_MODEL, TF), lambda i, f: (f, 0, 0)),
            pl.BlockSpec((TF, D_MODEL), lambda i, f: (f, 0)),
            _resident((nf, 3, TF)),
            _resident((nf, 1, TF)),
            pl.BlockSpec((V7X_SUBLANES, TF), lambda i, f: (0, f)),
            _resident((1, D_MODEL)),
        ],
        out_specs=(
            pl.BlockSpec((TM_FF, D_MODEL), lambda i, f: (i + first_tile, 0)),
            pl.BlockSpec((None, nf, V7X_SUBLANES, TF), lambda i, f: (i, 0, 0, 0)),
        ),
        scratch_shapes=[
            pltpu.VMEM((TM_FF, D_MODEL), F32),
            pltpu.VMEM((TM_FF + V7X_SUBLANES, TF), F32),
            pltpu.VMEM((nf, V7X_SUBLANES, TF), F32),
            pltpu.SemaphoreType.DMA(()),
        ],
        input_output_aliases={1: 0},
        compiler_params=_params("arbitrary", "arbitrary"),
        name="conv_ffn",
    )(x, done, h2, mod, wg, wu, wd, cw, cb, halo, g_final)


def _ffn_head_kernel(xs_ref, mods_ref, g2_ref, wg_ref, wu_ref, wd_ref, cw_ref, cb_ref, p0_ref, p1_ref, gf_ref,
                     x_hbm, h2_ref, mod_ref,
                     os_ref, a_ref, wgb_ref, wub_ref, wdb_ref, o_ref, halo_ref,
                     h2s_ref, accs_ref, acc_ref, ext_ref, x_sem, *, final_norm):
    f = pl.program_id(0)
    tm = h2_ref.shape[0]
    last = f == pl.num_programs(0) - 1
    x_copy = pltpu.make_async_copy(x_hbm.at[pl.ds(0, tm)], o_ref, x_sem)

    @pl.when(last)
    def _():
        x_copy.start()

    @pl.when(f == 0)
    def _():
        h2s_ref[...] = (_rms(xs_ref[...], g2_ref[...]) * (1.0 + mods_ref[4]) + mods_ref[3]).astype(BF16)
        accs_ref[...] = jnp.zeros_like(accs_ref)
        acc_ref[...] = jnp.zeros_like(acc_ref)

    wg, wu, wd = wg_ref[...].astype(BF16), wu_ref[...].astype(BF16), wd_ref[...].astype(BF16)
    wgb_ref[...] = wg
    wub_ref[...] = wu
    wdb_ref[...] = wd
    cw, cb = cw_ref[f], cb_ref[f]

    h2s = h2s_ref[...]
    a = _dot(h2s, wg)
    a_ref[...] = a
    conv = cb + cw[0:1] * p0_ref[...] + cw[1:2] * p1_ref[...] + cw[2:3] * a
    accs_ref[...] += _dot((jax.nn.silu(conv) * _dot(h2s, wu)).astype(BF16), wd)

    ext_ref[0:V7X_SUBLANES] = jnp.zeros((V7X_SUBLANES, ext_ref.shape[1]), F32)
    _ffn_tile(h2_ref, ext_ref, acc_ref, wg, wu, wd, cw, cb)
    halo_ref[...] = ext_ref[tm:tm + V7X_SUBLANES]

    @pl.when(last)
    def _():
        x2 = xs_ref[...] + mods_ref[5] * accs_ref[...]
        os_ref[...] = _rms(x2, gf_ref[...]) if final_norm else x2
        x_copy.wait()
        _ffn_finish(o_ref, acc_ref, mod_ref[5:6], gf_ref, final_norm)


def _ffn_head(layer, xs, mods, g2, w_gate, w_up, w_down, cw, cb, p0, p1, g_final, x, h2, mod, final_norm):
    rows = xs.shape[0]
    tokens = x.shape[0]
    nfh = D_FF // TF_HEAD
    per_tile = TF // TF_HEAD
    full = lambda shape: pl.BlockSpec(shape, lambda f: (0,) * len(shape))
    cols = lambda r: pl.BlockSpec((r, TF_HEAD), lambda f: (0, f))
    tile_copy = pl.BlockSpec((None, D_MODEL, TF_HEAD), lambda f: (f // per_tile, 0, f % per_tile))
    return pl.pallas_call(
        functools.partial(_ffn_head_kernel, final_norm=final_norm),
        out_shape=(
            jax.ShapeDtypeStruct((rows, D_MODEL), F32), jax.ShapeDtypeStruct((rows, D_FF), F32),
            jax.ShapeDtypeStruct((D_FF // TF, D_MODEL, TF), BF16), jax.ShapeDtypeStruct((D_FF // TF, D_MODEL, TF), BF16),
            jax.ShapeDtypeStruct((D_FF, D_MODEL), BF16),
            jax.ShapeDtypeStruct((tokens, D_MODEL), F32),
            jax.ShapeDtypeStruct((V7X_SUBLANES, D_FF), F32),
        ),
        grid=(nfh,),
        in_specs=[
            full((rows, D_MODEL)), full((6, rows, D_MODEL)), full((1, D_MODEL)),
            pl.BlockSpec((None, D_MODEL, TF_HEAD), lambda f: (layer, 0, f)),
            pl.BlockSpec((None, D_MODEL, TF_HEAD), lambda f: (layer, 0, f)),
            pl.BlockSpec((None, TF_HEAD, D_MODEL), lambda f: (layer, f, 0)),
            full((nfh, 3, TF_HEAD)), full((nfh, 1, TF_HEAD)),
            cols(rows), cols(rows),
            full((1, D_MODEL)),
            pl.BlockSpec(memory_space=pl.ANY),
            pl.BlockSpec((TM_FF, D_MODEL), lambda f: (0, 0), pipeline_mode=pl.Buffered(1)),
            pl.BlockSpec((None, 6, D_MODEL), lambda f: (0, 0, 0)),
        ],
        out_specs=(
            full((rows, D_MODEL)), cols(rows),
            tile_copy, tile_copy,
            pl.BlockSpec((TF_HEAD, D_MODEL), lambda f: (f, 0)),
            pl.BlockSpec((TM_FF, D_MODEL), lambda f: (0, 0)),
            cols(V7X_SUBLANES),
        ),
        scratch_shapes=[
            pltpu.VMEM((rows, D_MODEL), BF16), pltpu.VMEM((rows, D_MODEL), F32),
            pltpu.VMEM((TM_FF, D_MODEL), F32),
            pltpu.VMEM((TM_FF + V7X_SUBLANES, TF_HEAD), F32),
            pltpu.SemaphoreType.DMA(()),
        ],
        compiler_params=_params("arbitrary"),
        name="conv_ffn_head",
    )(xs, mods, g2, w_gate, w_up, w_down, cw, cb, p0, p1, g_final, x, h2, mod)


def _s_in_proj_kernel(x_ref, mod_ref, g1_ref, w_ref, z_ref, wb_ref, h_ref):
    @pl.when(pl.program_id(0) == 0)
    def _():
        h_ref[...] = (_rms(x_ref[...], g1_ref[...]) * (1.0 + mod_ref[1]) + mod_ref[0]).astype(BF16)

    w = w_ref[...].astype(BF16)
    wb_ref[...] = w
    z_ref[...] = _dot(h_ref[...], w)


def _s_in_proj(layer, x, mod, g1, w_in):
    rows = x.shape[0]
    full = lambda shape: pl.BlockSpec(shape, lambda j: (0,) * len(shape))
    return pl.pallas_call(
        _s_in_proj_kernel,
        out_shape=(jax.ShapeDtypeStruct((rows, IN_COLS), F32), jax.ShapeDtypeStruct((D_MODEL, IN_COLS), BF16)),
        grid=(IN_COLS // TN_IN,),
        in_specs=[
            full((rows, D_MODEL)), full((6, rows, D_MODEL)), full((1, D_MODEL)),
            pl.BlockSpec((None, D_MODEL, TN_IN), lambda j: (layer, 0, j)),
        ],
        out_specs=(pl.BlockSpec((rows, TN_IN), lambda j: (0, j)), pl.BlockSpec((D_MODEL, TN_IN), lambda j: (0, j))),
        scratch_shapes=[pltpu.VMEM((rows, D_MODEL), BF16)],
        compiler_params=_params("arbitrary"),
        name="sample_in_proj",
    )(x, mod, g1, w_in)


def _s_attn_kernel(sink_ref, z_ref, cos_ref, sin_ref, gain_ref, ws0_ref, bs0_ref, kt_ref, vt_ref,
                   attn_ref, gm_ref, gvn_ref, ktn_ref, vtn_ref):
    nseq = z_ref.shape[0]
    q_blocks, k_blocks, u, gvn = _post_project(
        z_ref[:, Q0:K0], z_ref[:, K0:V0], z_ref[:, U0:G0], z_ref[:, G0:IN_COLS],
        cos_ref[...], sin_ref[...], gain_ref[...], pair_q=False)
    v_new = z_ref[:, V0:U0]
    k_new = jnp.concatenate(k_blocks, axis=1)
    gvn_ref[...] = gvn
    gm_ref[...] = (u * (ws0_ref[...].astype(BF16).astype(F32) * gvn.astype(BF16).astype(F32) + bs0_ref[...])).astype(BF16)

    rows = GQA_GROUP * nseq
    keys = nseq * WINDOW
    seq_of_row = jnp.bitwise_and(_iota((rows, 1), 0), nseq - 1)
    head_of_row = lax.shift_right_logical(_iota((rows, 1), 0), nseq.bit_length() - 1)
    own_keys = lax.shift_right_logical(_iota((rows, keys), 1), WINDOW.bit_length() - 1) == seq_of_row
    own_new = _iota((rows, nseq), 1) == seq_of_row
    pieces = []
    for g in range(N_KV_HEADS):
        gs = slice(g * HEAD_DIM, (g + 1) * HEAD_DIM)
        heads = [g * GQA_GROUP + r for r in range(GQA_GROUP)]
        qg = jnp.concatenate(
            [q_blocks[h // 2][:, (h % 2) * HEAD_DIM:(h % 2 + 1) * HEAD_DIM] for h in heads], axis=0).astype(BF16)
        ktg = jnp.concatenate([kt_ref[b, gs, :] for b in range(nseq)], axis=1).astype(BF16)
        vtg = jnp.concatenate([vt_ref[b, gs, :] for b in range(nseq)], axis=1).astype(BF16)
        s = jnp.where(own_keys, _dot(qg, ktg), NEG)
        sn = jnp.where(own_new, _dot_nt(qg, k_new[:, gs].astype(BF16)), NEG)
        sk = jnp.full((rows, 1), sink_ref[heads[0]], F32)
        for r in range(1, GQA_GROUP):
            sk = jnp.where(head_of_row == r, sink_ref[heads[r]], sk)
        m = jnp.maximum(jnp.maximum(jnp.max(s, axis=-1, keepdims=True), jnp.max(sn, axis=-1, keepdims=True)), sk)
        p = jnp.exp(s - m)
        pn = jnp.exp(sn - m)
        denom = jnp.sum(p, axis=-1, keepdims=True) + jnp.sum(pn, axis=-1, keepdims=True) + jnp.exp(sk - m)
        o = (_dot_nt(p.astype(BF16), vtg) + _dot(pn.astype(BF16), v_new[:, gs].astype(BF16))) / denom
        pieces += [o[r * nseq:(r + 1) * nseq] for r in range(GQA_GROUP)]
    attn_ref[...] = jnp.concatenate(pieces, axis=1).astype(BF16)

    pad = jnp.zeros((V7X_LANES - nseq, KV_WIDTH), F32)
    knt = jnp.concatenate([k_new, pad], axis=0).T
    vnt = jnp.concatenate([v_new, pad], axis=0).T
    newest = _iota((1, WINDOW), 1) == WINDOW - 1
    for b in range(nseq):
        ktn_ref[b] = jnp.where(newest, knt[:, b:b + 1], pltpu.roll(kt_ref[b], WINDOW - 1, axis=1))
        vtn_ref[b] = jnp.where(newest, vnt[:, b:b + 1], pltpu.roll(vt_ref[b], WINDOW - 1, axis=1))


def _s_attention(layer, sinks, z, cos, sin_signed, gain, ws0, bs0, kt_all, vt_all):
    rows = z.shape[0]
    full = lambda shape: pl.BlockSpec(shape, lambda i: (0,) * len(shape))
    cache = pl.BlockSpec((None, rows, KV_WIDTH, WINDOW), lambda i: (layer, 0, 0, 0))
    return pl.pallas_call(
        _s_attn_kernel,
        out_shape=(
            jax.ShapeDtypeStruct((rows, ATTN_WIDTH), BF16),
            jax.ShapeDtypeStruct((rows, GM_WIDTH), BF16),
            jax.ShapeDtypeStruct((rows, GM_WIDTH), F32),
            jax.ShapeDtypeStruct((rows, KV_WIDTH, WINDOW), F32),
            jax.ShapeDtypeStruct((rows, KV_WIDTH, WINDOW), F32),
        ),
        grid=(1,),
        in_specs=[
            pl.BlockSpec(memory_space=pltpu.SMEM),
            full((rows, IN_COLS)), full((1, V7X_LANES)), full((1, V7X_LANES)),
            full((1, GM_WIDTH)), full((1, GM_WIDTH)), full((1, GM_WIDTH)),
            cache, cache,
        ],
        out_specs=(full((rows, ATTN_WIDTH)), full((rows, GM_WIDTH)), full((rows, GM_WIDTH)),
                   full((rows, KV_WIDTH, WINDOW)), full((rows, KV_WIDTH, WINDOW))),
        compiler_params=_params("arbitrary"),
        name="sample_attention",
    )(sinks, z, cos, sin_signed, gain, ws0, bs0, kt_all, vt_all)


def _paired_head(slot):
    p, rest = divmod(slot, 2 * GQA_GROUP)
    j, i = divmod(rest, 2)
    return p * 2 * GQA_GROUP + i * GQA_GROUP + j


def _s_mix_out_kernel(x_ref, gate_ref, a_ref, gm_ref, w_ref, o_ref, wb_ref):
    w = w_ref[...].astype(BF16)
    for slot in range(N_HEADS):
        head = _paired_head(slot)
        wb_ref[slot * HEAD_DIM:(slot + 1) * HEAD_DIM] = w[head * HEAD_DIM:(head + 1) * HEAD_DIM]
    wb_ref[ATTN_WIDTH:] = w[ATTN_WIDTH:]
    mix = _dot(a_ref[...], w[:ATTN_WIDTH]) + _dot(gm_ref[...], w[ATTN_WIDTH:])
    o_ref[...] = x_ref[...] + gate_ref[...] * mix


def _s_mix_out(layer, x, gate, attn, gm, w_out):
    rows = x.shape[0]
    full = lambda shape: pl.BlockSpec(shape, lambda j: (0,) * len(shape))
    col_tile = lambda r: pl.BlockSpec((r, TN_OUT), lambda j: (0, j))
    return pl.pallas_call(
        _s_mix_out_kernel,
        out_shape=(jax.ShapeDtypeStruct((rows, D_MODEL), F32), jax.ShapeDtypeStruct((D_MODEL, D_MODEL), BF16)),
        grid=(D_MODEL // TN_OUT,),
        in_specs=[
            col_tile(rows), col_tile(rows), full((rows, ATTN_WIDTH)), full((rows, GM_WIDTH)),
            pl.BlockSpec((None, D_MODEL, TN_OUT), lambda j: (layer, 0, j)),
        ],
        out_specs=(col_tile(rows), col_tile(D_MODEL)),
        compiler_params=_params("arbitrary"),
        name="sample_mix_out",
    )(x, gate, attn, gm, w_out)


def _rope_tables(pos):
    half = HEAD_DIM // 2
    inv = ROPE_THETA ** (-jnp.arange(half, dtype=jnp.float32) / half)
    ang = pos.astype(jnp.float32)[:, None] * inv[None, :]
    reps = V7X_LANES // half
    sign = jnp.where((jnp.arange(V7X_LANES) % HEAD_DIM) < half, -1.0, 1.0).astype(F32)
    return jnp.tile(jnp.cos(ang), (1, reps)), jnp.tile(jnp.sin(ang), (1, reps)) * sign[None, :]


def _to_planes(window_buf):
    lead = window_buf.shape[:-3]
    return jnp.moveaxis(window_buf, -3, -1).reshape(lead + (KV_WIDTH, WINDOW))


def _from_planes(planes):
    lead = planes.shape[:-2]
    return jnp.moveaxis(planes.reshape(lead + (N_KV_HEADS, HEAD_DIM, WINDOW)), -1, -3)


def kernel(x_prompt, x_sample, cache_k, cache_v, state_conv, c_prompt, c_sample, w_ada, b_ada, g_norm1, g_norm2,
           w_in, gm_gain, gm_ws, gm_bs, sinks, w_out, w_gate, w_up, conv_w, conv_b, w_down, g_final):
    batch, seq, _ = x_prompt.shape
    dec = x_sample.shape[0]
    nf = D_FF // TF
    assert x_sample.shape[1] == 1 and seq % TM == 0 and seq % TQ == 0 and TM % CHUNK == 0 and D_FF % TF == 0
    assert seq // TM_FF >= 2 and seq % TM_FF == 0 and TM_FF % FFN_SUBBLOCKS == 0 and TF % TF_HEAD == 0
    assert dec & (dec - 1) == 0 and dec <= V7X_LANES

    bst = jnp.swapaxes(gm_bs, 1, 2)
    ws0 = jnp.repeat(gm_ws[:, :, 0, 0], GM_WIDTH // GM_HEADS, axis=1)[:, None, :]
    bs0 = jnp.repeat(gm_bs[:, :, 0], GM_WIDTH // GM_HEADS, axis=1)[:, None, :]
    cw_tiles = jnp.swapaxes(conv_w.reshape(DEPTH, 3, nf, TF), 1, 2)
    cb_tiles = conv_b.reshape(DEPTH, nf, 1, TF)
    cw_head = jnp.swapaxes(conv_w.reshape(DEPTH, 3, D_FF // TF_HEAD, TF_HEAD), 1, 2)
    cb_head = conv_b.reshape(DEPTH, D_FF // TF_HEAD, 1, TF_HEAD)

    cos_p, sin_p = _rope_tables(jnp.arange(seq, dtype=jnp.int32))
    cos_s, sin_s = _rope_tables(PAST_LEN + jnp.arange(1, dtype=jnp.int32))

    c_all = jnp.concatenate([c_prompt, c_sample, jnp.zeros((MOD_ROWS - batch - dec, D_MODEL), F32)], axis=0)
    mod = _modulation(c_all, w_ada, b_ada)

    kt_all, vt_all = _to_planes(cache_k), _to_planes(cache_v)

    hp = x_prompt.reshape(batch * seq, D_MODEL)
    hs = x_sample.reshape(dec, D_MODEL)
    g_fin = g_final.reshape(1, D_MODEL)
    outs = [[] for _ in range(8)]
    for l in range(DEPTH):
        last = l == DEPTH - 1
        g1 = g_norm1[l].reshape(1, D_MODEL)
        g2 = g_norm2[l].reshape(1, D_MODEL)
        gain = gm_gain[l].reshape(1, GM_WIDTH)
        mod_p = jnp.swapaxes(mod[l, :, :batch], 0, 1)
        mod_s = mod[l, :, batch:batch + dec]

        z, w_in_b = _s_in_proj(l, hs, mod_s, g1, w_in)
        attn_s, gm_s, gvn_s, kt_new, vt_new = _s_attention(
            l, sinks[l], z, cos_s, sin_s, gain, ws0[l], bs0[l], kt_all, vt_all)
        hs, w_out_b = _s_mix_out(l, hs, mod_s[2], attn_s, gm_s, w_out)
        outs[4].append(kt_new)
        outs[5].append(vt_new)
        outs[6].append(gvn_s[:, None, :])

        q, k, v, gm, gvl = _mix_in(hp, mod_p, g1, w_in_b, cos_p, sin_p, gain, gm_ws[l], bst[l], seq)
        attn, kt_p, vt_p = _attention(sinks[l], q, k, v, seq)
        hp, h2 = _mix_out(hp, mod_p, g2, attn, gm, w_out_b, seq)
        outs[0].append(kt_p)
        outs[1].append(vt_p)
        outs[2].append(gvl)

        hs, a_new, wg_b, wu_b, wd_b, hp_done, halo = _ffn_head(
            l, hs, mod_s, g2, w_gate, w_up, w_down, cw_head[l], cb_head[l], state_conv[l, :, 0], state_conv[l, :, 1],
            g_fin, hp, h2, mod_p, last)
        hp, tail = _ffn(hp, hp_done, halo, h2, mod_p, wg_b, wu_b, wd_b, cw_tiles[l], cb_tiles[l], g_fin, seq, last)
        outs[7].append(jnp.stack([state_conv[l, :, 1], a_new], axis=1))
        tails = tail[seq // TM_FF - 2::seq // TM_FF, :, V7X_SUBLANES - 2:, :]
        outs[3].append(jnp.swapaxes(tails, 1, 2).reshape(batch, 2, D_FF))

    st = [jnp.stack(o) for o in outs]
    return (hp.reshape(batch, seq, D_MODEL), hs.reshape(dec, 1, D_MODEL),
            _from_planes(st[0]), _from_planes(st[1]), st[2], st[3],
            _from_planes(st[4]), _from_planes(st[5]), st[6], st[7])
```

```python
import functools

import jax
import jax.numpy as jnp
from jax import lax
from jax.experimental import pallas as pl
from jax.experimental.pallas import tpu as pltpu

D_MODEL = 2048
DEPTH = 2
PAST_LEN = 16384
ATTN_WIDTH = 1024
GM_WIDTH = 1024
HEAD_DIM = 64
N_HEADS = 16
N_KV_HEADS = 4
GQA_GROUP = N_HEADS // N_KV_HEADS
WINDOW = 128
ROPE_THETA = 10000.0
CHUNK = 128
GM_HEADS = 8
D_FF = 5632
KV_WIDTH = N_KV_HEADS * HEAD_DIM
IN_COLS = ATTN_WIDTH + 2 * KV_WIDTH + 2 * GM_WIDTH
EPS = 1e-6
NEG = -1e30

Q0, K0, V0, U0, G0 = 0, ATTN_WIDTH, ATTN_WIDTH + KV_WIDTH, ATTN_WIDTH + 2 * KV_WIDTH, ATTN_WIDTH + 2 * KV_WIDTH + GM_WIDTH

V7X_LANES = 128
V7X_SUBLANES = 8
V7X_VMEM_LIMIT_BYTES = 56 * 1024 * 1024

TM = 512
TF = 512
TM_FF = 1024
TF_HEAD = 256
TQ = 512
TN_IN = 512
TN_OUT = 512
MOD_TN = 1024
MOD_ROWS = 48
FFN_SUBBLOCKS = 4
MIX_OUT_SUBBLOCKS = 2

BF16 = jnp.bfloat16
F32 = jnp.float32


def _params(*semantics):
    return pltpu.CompilerParams(dimension_semantics=semantics, vmem_limit_bytes=V7X_VMEM_LIMIT_BYTES)


def _dot(a, b):
    return jnp.dot(a, b, preferred_element_type=F32)


def _dot_nt(a, b):
    return lax.dot_general(a, b, (((1,), (1,)), ((), ())), preferred_element_type=F32)


def _rms(x, g):
    return x * lax.rsqrt(jnp.mean(x * x, axis=-1, keepdims=True) + EPS) * g


def _resident(shape):
    zeros = (0,) * len(shape)
    return pl.BlockSpec(shape, lambda *_: zeros, pipeline_mode=pl.Buffered(1))


def _iota(shape, dim):
    return lax.broadcasted_iota(jnp.int32, shape, dim)


def _mod_kernel(c_ref, w_ref, b_ref, o_ref):
    s = jax.nn.silu(c_ref[...]).astype(BF16)
    o_ref[...] = _dot(s, w_ref[...].astype(BF16)) + b_ref[...]


def _modulation(c_all, w_ada, b_ada):
    n_tiles = (6 * D_MODEL) // MOD_TN
    per_chunk = D_MODEL // MOD_TN
    return pl.pallas_call(
        _mod_kernel,
        out_shape=jax.ShapeDtypeStruct((DEPTH, 6, MOD_ROWS, D_MODEL), F32),
        grid=(DEPTH, n_tiles),
        in_specs=[
            pl.BlockSpec((MOD_ROWS, D_MODEL), lambda l, j: (0, 0)),
            pl.BlockSpec((None, D_MODEL, MOD_TN), lambda l, j: (l, 0, j)),
            pl.BlockSpec((None, 1, MOD_TN), lambda l, j: (l, 0, j)),
        ],
        out_specs=pl.BlockSpec((None, None, MOD_ROWS, MOD_TN), lambda l, j: (l, j // per_chunk, 0, j % per_chunk)),
        compiler_params=_params("arbitrary", "arbitrary"),
        name="adaln_modulation",
    )(c_all, w_ada, b_ada.reshape(DEPTH, 1, 6 * D_MODEL))


def _rope_cols(z, cos, sin_signed):
    lane = _iota((1, V7X_LANES), 1)
    first_half = jnp.bitwise_and(lane, HEAD_DIM - 1) < HEAD_DIM // 2
    outs = []
    for c in range(z.shape[1] // V7X_LANES):
        zc = z[:, c * V7X_LANES:(c + 1) * V7X_LANES]
        partner = jnp.where(first_half,
                            pltpu.roll(zc, V7X_LANES - HEAD_DIM // 2, axis=1),
                            pltpu.roll(zc, HEAD_DIM // 2, axis=1))
        outs.append(zc * cos + partner * sin_signed)
    return outs


def _swap_halves(x):
    return pltpu.roll(x, HEAD_DIM, axis=1)


def _pair_q_blocks(nat):
    low = _iota((1, V7X_LANES), 1) < HEAD_DIM
    out = []
    for c in range(len(nat)):
        p, j = divmod(c, 4)
        a, b = nat[p * 4 + j // 2], nat[p * 4 + 2 + j // 2]
        out.append(jnp.where(low, a, _swap_halves(b)) if j % 2 == 0 else jnp.where(low, _swap_halves(a), b))
    return out


def _post_project(zq, zk, zu, zg, cos, sin_signed, gain, pair_q):
    q_blocks = _rope_cols(zq, cos, sin_signed)
    if pair_q:
        q_blocks = _pair_q_blocks(q_blocks)
    q_blocks = [qb * (HEAD_DIM ** -0.5) for qb in q_blocks]
    k_blocks = _rope_cols(zk, cos, sin_signed)
    u = jax.nn.gelu(zu)
    gvn = _rms(jax.nn.gelu(zg), gain)
    return q_blocks, k_blocks, u, gvn


def _store_cols(ref, blocks):
    for c, blk in enumerate(blocks):
        ref[:, c * V7X_LANES:(c + 1) * V7X_LANES] = blk.astype(ref.dtype)


def _mix_in_kernel(x_ref, mod_ref, g1_ref, w_ref, cos_ref, sin_ref, gain_ref, ws_ref, bst_ref,
                   q_ref, k_ref, v_ref, gm_ref, gvl_ref):
    tm = x_ref.shape[0]
    h = (_rms(x_ref[...], g1_ref[...]) * (1.0 + mod_ref[1:2]) + mod_ref[0:1]).astype(BF16)
    q_blocks, k_blocks, u, gvn = _post_project(
        _dot(h, w_ref[:, Q0:K0]), _dot(h, w_ref[:, K0:V0]), _dot(h, w_ref[:, U0:G0]), _dot(h, w_ref[:, G0:IN_COLS]),
        cos_ref[...], sin_ref[...], gain_ref[...], pair_q=True)
    _store_cols(q_ref, q_blocks)
    _store_cols(k_ref, k_blocks)
    v_ref[...] = _dot(h, w_ref[:, V0:U0])
    gvl_ref[...] = gvn[tm - CHUNK:, :]
    gvb = gvn.astype(BF16)
    causal = _iota((CHUNK, CHUNK), 1) <= _iota((CHUNK, CHUNK), 0)
    for hh in range(GM_HEADS):
        wm = jnp.where(causal, ws_ref[hh], 0.0).astype(BF16)
        bias = bst_ref[:, hh:hh + 1]
        cs = slice(hh * CHUNK, (hh + 1) * CHUNK)
        for c in range(tm // CHUNK):
            rs = slice(c * CHUNK, (c + 1) * CHUNK)
            s = _dot(wm, gvb[rs, cs]) + bias
            gm_ref[rs, cs] = (u[rs, cs] * s).astype(BF16)


def _mix_in(x, mod, g1, w_in, cos, sin_signed, gain, ws, bst, seq):
    tokens = x.shape[0]
    tiles_per_seq = seq // TM
    batch = tokens // seq
    row_tile = lambda width: pl.BlockSpec((TM, width), lambda i: (i, 0))
    return pl.pallas_call(
        _mix_in_kernel,
        out_shape=(
            jax.ShapeDtypeStruct((tokens, ATTN_WIDTH), BF16),
            jax.ShapeDtypeStruct((tokens, KV_WIDTH), F32),
            jax.ShapeDtypeStruct((tokens, KV_WIDTH), F32),
            jax.ShapeDtypeStruct((tokens, GM_WIDTH), BF16),
            jax.ShapeDtypeStruct((batch, CHUNK, GM_WIDTH), F32),
        ),
        grid=(tokens // TM,),
        in_specs=[
            row_tile(D_MODEL),
            pl.BlockSpec((None, 6, D_MODEL), lambda i: (i // tiles_per_seq, 0, 0)),
            _resident((1, D_MODEL)),
            _resident((D_MODEL, IN_COLS)),
            pl.BlockSpec((TM, V7X_LANES), lambda i: (i % tiles_per_seq, 0)),
            pl.BlockSpec((TM, V7X_LANES), lambda i: (i % tiles_per_seq, 0)),
            _resident((1, GM_WIDTH)),
            _resident((GM_HEADS, CHUNK, CHUNK)),
            _resident((CHUNK, GM_HEADS)),
        ],
        out_specs=(
            row_tile(ATTN_WIDTH), row_tile(KV_WIDTH), row_tile(KV_WIDTH), row_tile(GM_WIDTH),
            pl.BlockSpec((None, CHUNK, GM_WIDTH), lambda i: (i // tiles_per_seq, 0, 0)),
        ),
        compiler_params=_params("arbitrary"),
        name="mix_in",
    )(x, mod, g1, w_in, cos, sin_signed, gain, ws, bst)


def _attn_block(n, sink_ref, q_blk, k_prev, k_cur, v_prev, v_cur, out_blk):
    row = _iota((WINDOW, 2 * WINDOW), 0)
    key = _iota((WINDOW, 2 * WINDOW), 1) - WINDOW
    valid = (key <= row) & (key >= jnp.maximum(row - WINDOW, -n * WINDOW))
    low = _iota((WINDOW, V7X_LANES), 1) < HEAD_DIM
    cols_per_pair = ATTN_WIDTH // V7X_LANES // 2
    for p in range(2):
        ls = slice(p * V7X_LANES, (p + 1) * V7X_LANES)
        kcat = jnp.concatenate([k_prev[:, ls], k_cur[:, ls]], axis=0).astype(BF16)
        vcat = jnp.concatenate([v_prev[:, ls], v_cur[:, ls]], axis=0).astype(BF16)
        blocks = []
        for cc in range(cols_per_pair):
            c = p * cols_per_pair + cc
            qcol = q_blk[:, c * V7X_LANES:(c + 1) * V7X_LANES]
            zero = jnp.zeros_like(qcol)
            blocks += [jnp.where(low, qcol, zero), jnp.where(low, zero, qcol)]
        s = _dot_nt(jnp.concatenate(blocks, axis=0), kcat)
        probs, inv = [], []
        for j in range(2 * cols_per_pair):
            cc, e = divmod(j, 2)
            sj = jnp.where(valid, s[j * WINDOW:(j + 1) * WINDOW], NEG)
            sk = sink_ref[p * 2 * GQA_GROUP + e * GQA_GROUP + cc]
            m = jnp.maximum(jnp.max(sj, axis=-1, keepdims=True), sk)
            pj = jnp.exp(sj - m)
            inv.append(1.0 / (jnp.sum(pj, axis=-1, keepdims=True) + jnp.exp(sk - m)))
            probs.append(pj.astype(BF16))
        o = _dot(jnp.concatenate(probs, axis=0), vcat)
        for cc in range(cols_per_pair):
            c = p * cols_per_pair + cc
            even = o[(2 * cc) * WINDOW:(2 * cc + 1) * WINDOW] * inv[2 * cc]
            odd = o[(2 * cc + 1) * WINDOW:(2 * cc + 2) * WINDOW] * inv[2 * cc + 1]
            out_blk[:, c * V7X_LANES:(c + 1) * V7X_LANES] = jnp.where(low, even, odd).astype(BF16)


def _attn_kernel(sink_ref, q_ref, kp_ref, kc_ref, vp_ref, vc_ref, o_ref, kt_ref, vt_ref):
    step = pl.program_id(1)
    blocks = q_ref.shape[0] // WINDOW
    for t in range(blocks):
        rows = slice(t * WINDOW, (t + 1) * WINDOW)
        prev = slice((t - 1) * WINDOW, t * WINDOW)
        k_prev = kp_ref[...] if t == 0 else kc_ref[prev]
        v_prev = vp_ref[...] if t == 0 else vc_ref[prev]
        _attn_block(step * blocks + t, sink_ref, q_ref.at[rows], k_prev, kc_ref[rows], v_prev, vc_ref[rows], o_ref.at[rows])

    @pl.when(step == pl.num_programs(1) - 1)
    def _():
        last = slice((blocks - 1) * WINDOW, blocks * WINDOW)
        kt_ref[...] = kc_ref[last].T
        vt_ref[...] = vc_ref[last].T


def _attention(sinks, q, k, v, seq):
    tokens = q.shape[0]
    batch = tokens // seq
    steps = seq // TQ
    per_step = TQ // WINDOW
    cur = lambda b, n: (b * steps + n, 0)
    prev = lambda b, n: (jnp.maximum((b * steps + n) * per_step - 1, 0), 0)
    state = pl.BlockSpec((None, KV_WIDTH, WINDOW), lambda b, n: (b, 0, 0))
    return pl.pallas_call(
        _attn_kernel,
        out_shape=(
            jax.ShapeDtypeStruct((tokens, ATTN_WIDTH), BF16),
            jax.ShapeDtypeStruct((batch, KV_WIDTH, WINDOW), F32),
            jax.ShapeDtypeStruct((batch, KV_WIDTH, WINDOW), F32),
        ),
        grid=(batch, steps),
        in_specs=[
            pl.BlockSpec(memory_space=pltpu.SMEM),
            pl.BlockSpec((TQ, ATTN_WIDTH), cur),
            pl.BlockSpec((WINDOW, KV_WIDTH), prev),
            pl.BlockSpec((TQ, KV_WIDTH), cur),
            pl.BlockSpec((WINDOW, KV_WIDTH), prev),
            pl.BlockSpec((TQ, KV_WIDTH), cur),
        ],
        out_specs=(pl.BlockSpec((TQ, ATTN_WIDTH), cur), state, state),
        compiler_params=_params("arbitrary", "arbitrary"),
        name="swa_attention",
    )(sinks, q, k, k, v, v)


def _mix_out_kernel(x_ref, mod_ref, g2_ref, a_ref, gm_ref, wa_ref, wb_ref, o_ref, h2_ref):
    sub = x_ref.shape[0] // MIX_OUT_SUBBLOCKS
    for r in range(MIX_OUT_SUBBLOCKS):
        rs = slice(r * sub, (r + 1) * sub)
        mix = _dot(a_ref[rs], wa_ref[...]) + _dot(gm_ref[rs], wb_ref[...])
        x1 = x_ref[rs] + mod_ref[2:3] * mix
        o_ref[rs] = x1
        h2_ref[rs] = (_rms(x1, g2_ref[...]) * (1.0 + mod_ref[4:5]) + mod_ref[3:4]).astype(BF16)


def _mix_out(x, mod, g2, attn, gm, w_out, seq):
    tokens = x.shape[0]
    tiles_per_seq = seq // TM
    row_tile = lambda width: pl.BlockSpec((TM, width), lambda i: (i, 0))
    half = lambda k: pl.BlockSpec((ATTN_WIDTH, D_MODEL), lambda i: (k, 0), pipeline_mode=pl.Buffered(1))
    return pl.pallas_call(
        _mix_out_kernel,
        out_shape=(jax.ShapeDtypeStruct((tokens, D_MODEL), F32), jax.ShapeDtypeStruct((tokens, D_MODEL), BF16)),
        grid=(tokens // TM,),
        in_specs=[
            row_tile(D_MODEL),
            pl.BlockSpec((None, 6, D_MODEL), lambda i: (i // tiles_per_seq, 0, 0)),
            _resident((1, D_MODEL)),
            row_tile(ATTN_WIDTH), row_tile(GM_WIDTH),
            half(0), half(1),
        ],
        out_specs=(row_tile(D_MODEL), row_tile(D_MODEL)),
        compiler_params=_params("arbitrary"),
        name="mix_out",
    )(x, mod, g2, attn, gm, w_out, w_out)


def _ffn_tile(h2_ref, ext_ref, acc_ref, wg_parts, wu_parts, wd, cw, cb, riders=None):
    tm = h2_ref.shape[0]
    sub = tm // FFN_SUBBLOCKS
    width = ext_ref.shape[1] // len(wg_parts)
    ups = []
    rider_a = rider_up = None
    for r in range(FFN_SUBBLOCKS):
        h2 = h2_ref[r * sub:(r + 1) * sub]
        ride = riders is not None and r == FFN_SUBBLOCKS - 1
        if ride:
            h2 = jnp.concatenate([h2, riders[0]], axis=0)
        rows = slice(V7X_SUBLANES + r * sub, V7X_SUBLANES + (r + 1) * sub)
        a_parts = [_dot(h2, wg) for wg in wg_parts]
        up = jnp.concatenate([_dot(h2, wu) for wu in wu_parts], axis=1)
        for c, a in enumerate(a_parts):
            ext_ref[rows, c * width:(c + 1) * width] = a[:sub]
        ups.append(up[:sub])
        if ride:
            rider_a, rider_up = jnp.concatenate([a[sub:] for a in a_parts], axis=1), up[sub:]
    for r in range(FFN_SUBBLOCKS):
        lo = V7X_SUBLANES + r * sub
        conv = (cb + cw[0:1] * ext_ref[lo - 2:lo - 2 + sub]
                + cw[1:2] * ext_ref[lo - 1:lo - 1 + sub] + cw[2:3] * ext_ref[lo:lo + sub])
        y = (jax.nn.silu(conv) * ups[r]).astype(BF16)
        if riders is not None and r == FFN_SUBBLOCKS - 1:
            out = _dot(jnp.concatenate([y, riders[1](rider_a, rider_up)], axis=0), wd)
            acc_ref[r * sub:(r + 1) * sub] += out[:sub]
            riders[2][...] += out[sub:]
        else:
            acc_ref[r * sub:(r + 1) * sub] += _dot(y, wd)


def _ffn_finish(o_ref, acc_ref, gate, gf_ref, final_norm):
    sub = o_ref.shape[0] // FFN_SUBBLOCKS

    def finish(r, carry):
        rs = pl.ds(pl.multiple_of(r * sub, sub), sub)
        x2 = o_ref[rs, :] + gate * acc_ref[rs, :]
        o_ref[rs, :] = _rms(x2, gf_ref[...]) if final_norm else x2
        return carry

    lax.fori_loop(0, FFN_SUBBLOCKS, finish, 0)


def _ffn_kernel(x_hbm, head_hbm, h2_ref, mod_ref, wg_ref, wu_ref, wd_ref, cw_ref, cb_ref, halo_ref, gf_ref,
                o_hbm, tail_ref, obuf_ref, acc_ref, ext_ref, carry_ref, x_sem, wb_sem, head_sem,
                *, first_tile, tiles_per_seq, final_norm):
    i = pl.program_id(0)
    f = pl.program_id(1)
    tm = h2_ref.shape[0]
    tile = i + first_tile
    last = f == pl.num_programs(1) - 1
    rows = pl.ds(pl.multiple_of(tile * tm, tm), tm)
    x_copy = pltpu.make_async_copy(x_hbm.at[rows], obuf_ref, x_sem)
    write_back = pltpu.make_async_copy(obuf_ref, o_hbm.at[rows], wb_sem)
    head_copy = pltpu.make_async_copy(head_hbm, o_hbm.at[pl.ds(0, first_tile * tm)], head_sem)

    @pl.when((i == 0) & (f == 0))
    def _():
        head_copy.start()

    @pl.when(last & (i > 0))
    def _():
        write_back.wait()

    @pl.when(last)
    def _():
        x_copy.start()

    @pl.when(f == 0)
    def _():
        acc_ref[...] = jnp.zeros_like(acc_ref)

    seq_start = (tile % tiles_per_seq) == 0

    @pl.when(seq_start)
    def _():
        ext_ref[0:V7X_SUBLANES] = jnp.zeros((V7X_SUBLANES, ext_ref.shape[1]), F32)

    @pl.when(jnp.logical_not(seq_start) & (i == 0))
    def _():
        ext_ref[0:V7X_SUBLANES] = halo_ref[...]

    @pl.when(jnp.logical_not(seq_start) & (i > 0))
    def _():
        ext_ref[0:V7X_SUBLANES] = carry_ref[f]

    parts = range(wg_ref.shape[0])
    _ffn_tile(h2_ref, ext_ref, acc_ref, [wg_ref[c] for c in parts], [wu_ref[c] for c in parts], wd_ref[...],
              cw_ref[f], cb_ref[f])
    tail = ext_ref[tm:tm + V7X_SUBLANES]
    carry_ref[f] = tail
    tail_ref[f] = tail

    @pl.when(last)
    def _():
        x_copy.wait()
        _ffn_finish(obuf_ref, acc_ref, mod_ref[5:6], gf_ref, final_norm)
        write_back.start()

    @pl.when(last & (i == pl.num_programs(0) - 1))
    def _():
        write_back.wait()
        head_copy.wait()


def _ffn(x, head_rows, halo, h2, mod, wg, wu, wd, cw, cb, g_final, seq, final_norm):
    tokens = x.shape[0]
    tiles_per_seq = seq // TM_FF
    nf = D_FF // TF
    first_tile = 1
    kern = functools.partial(_ffn_kernel, first_tile=first_tile, tiles_per_seq=tiles_per_seq, final_norm=final_norm)
    return pl.pallas_call(
        kern,
        out_shape=(
            jax.ShapeDtypeStruct((tokens, D_MODEL), F32),
            jax.ShapeDtypeStruct((tokens // TM_FF - first_tile, nf, V7X_SUBLANES, TF), F32),
        ),
        grid=(tokens // TM_FF - first_tile, nf),
        in_specs=[
            pl.BlockSpec(memory_space=pl.ANY),
            pl.BlockSpec(memory_space=pl.ANY),
            pl.BlockSpec((TM_FF, D_MODEL), lambda i, f: (i + first_tile, 0)),
            pl.BlockSpec((None, 6, D_MODEL), lambda i, f: ((i + first_tile) // tiles_per_seq, 0, 0)),
            pl.BlockSpec((TF // TF_HEAD, D_MODEL, TF_HEAD), lambda i, f: (f, 0, 0)),
            pl.BlockSpec((TF // TF_HEAD, D_MODEL, TF_HEAD), lambda i, f: (f, 0, 0)),
            pl.BlockSpec((TF, D_MODEL), lambda i, f: (f, 0)),
            _resident((nf, 3, TF)),
            _resident((nf, 1, TF)),
            pl.BlockSpec((V7X_SUBLANES, TF), lambda i, f: (0, f)),
            _resident((1, D_MODEL)),
        ],
        out_specs=(
            pl.BlockSpec(memory_space=pl.ANY),
            pl.BlockSpec((None, nf, V7X_SUBLANES, TF), lambda i, f: (i, 0, 0, 0)),
        ),
        scratch_shapes=[
            pltpu.VMEM((TM_FF, D_MODEL), F32),
            pltpu.VMEM((TM_FF, D_MODEL), F32),
            pltpu.VMEM((TM_FF + V7X_SUBLANES, TF), F32),
            pltpu.VMEM((nf, V7X_SUBLANES, TF), F32),
            pltpu.SemaphoreType.DMA(()),
            pltpu.SemaphoreType.DMA(()),
            pltpu.SemaphoreType.DMA(()),
        ],
        compiler_params=_params("arbitrary", "arbitrary"),
        name="conv_ffn",
    )(x, head_rows, h2, mod, wg, wu, wd, cw, cb, halo, g_final)


def _ffn_head_kernel(xs_ref, mods_ref, g2_ref, wg_ref, wu_ref, wd_ref, cw_ref, cb_ref, p0_ref, p1_ref, gf_ref,
                     x_hbm, h2_ref, mod_ref,
                     os_ref, a_ref, wgb_ref, wub_ref, wdb_ref, o_ref, halo_ref,
                     h2s_ref, accs_ref, acc_ref, ext_ref, x_sem, *, final_norm):
    f = pl.program_id(0)
    tm = h2_ref.shape[0]
    last = f == pl.num_programs(0) - 1
    x_copy = pltpu.make_async_copy(x_hbm.at[pl.ds(0, tm)], o_ref, x_sem)

    @pl.when(last)
    def _():
        x_copy.start()

    @pl.when(f == 0)
    def _():
        h2s_ref[...] = (_rms(xs_ref[...], g2_ref[...]) * (1.0 + mods_ref[4]) + mods_ref[3]).astype(BF16)
        accs_ref[...] = jnp.zeros_like(accs_ref)
        acc_ref[...] = jnp.zeros_like(acc_ref)

    wg, wu, wd = wg_ref[...].astype(BF16), wu_ref[...].astype(BF16), wd_ref[...].astype(BF16)
    wgb_ref[...] = wg
    wub_ref[...] = wu
    wdb_ref[...] = wd
    cw, cb = cw_ref[f], cb_ref[f]

    def sample_gate(a, up):
        a_ref[...] = a
        conv = cb + cw[0:1] * p0_ref[...] + cw[1:2] * p1_ref[...] + cw[2:3] * a
        return (jax.nn.silu(conv) * up).astype(BF16)

    ext_ref[0:V7X_SUBLANES] = jnp.zeros((V7X_SUBLANES, ext_ref.shape[1]), F32)
    _ffn_tile(h2_ref, ext_ref, acc_ref, [wg], [wu], wd, cw, cb, riders=(h2s_ref[...], sample_gate, accs_ref))
    halo_ref[...] = ext_ref[tm:tm + V7X_SUBLANES]

    @pl.when(last)
    def _():
        x2 = xs_ref[...] + mods_ref[5] * accs_ref[...]
        os_ref[...] = _rms(x2, gf_ref[...]) if final_norm else x2
        x_copy.wait()
        _ffn_finish(o_ref, acc_ref, mod_ref[5:6], gf_ref, final_norm)


def _ffn_head(layer, xs, mods, g2, w_gate, w_up, w_down, cw, cb, p0, p1, g_final, x, h2, mod, final_norm):
    rows = xs.shape[0]
    tokens = x.shape[0]
    nfh = D_FF // TF_HEAD
    full = lambda shape: pl.BlockSpec(shape, lambda f: (0,) * len(shape))
    cols = lambda r: pl.BlockSpec((r, TF_HEAD), lambda f: (0, f))
    tile_copy = pl.BlockSpec((None, D_MODEL, TF_HEAD), lambda f: (f, 0, 0))
    return pl.pallas_call(
        functools.partial(_ffn_head_kernel, final_norm=final_norm),
        out_shape=(
            jax.ShapeDtypeStruct((rows, D_MODEL), F32), jax.ShapeDtypeStruct((rows, D_FF), F32),
            jax.ShapeDtypeStruct((nfh, D_MODEL, TF_HEAD), BF16), jax.ShapeDtypeStruct((nfh, D_MODEL, TF_HEAD), BF16),
            jax.ShapeDtypeStruct((D_FF, D_MODEL), BF16),
            jax.ShapeDtypeStruct((TM_FF, D_MODEL), F32),
            jax.ShapeDtypeStruct((V7X_SUBLANES, D_FF), F32),
        ),
        grid=(nfh,),
        in_specs=[
            full((rows, D_MODEL)), full((6, rows, D_MODEL)), full((1, D_MODEL)),
            pl.BlockSpec((None, D_MODEL, TF_HEAD), lambda f: (layer, 0, f)),
            pl.BlockSpec((None, D_MODEL, TF_HEAD), lambda f: (layer, 0, f)),
            pl.BlockSpec((None, TF_HEAD, D_MODEL), lambda f: (layer, f, 0)),
            full((nfh, 3, TF_HEAD)), full((nfh, 1, TF_HEAD)),
            cols(rows), cols(rows),
            full((1, D_MODEL)),
            pl.BlockSpec(memory_space=pl.ANY),
            pl.BlockSpec((TM_FF, D_MODEL), lambda f: (0, 0), pipeline_mode=pl.Buffered(1)),
            pl.BlockSpec((None, 6, D_MODEL), lambda f: (0, 0, 0)),
        ],
        out_specs=(
            full((rows, D_MODEL)), cols(rows),
            tile_copy, tile_copy,
            pl.BlockSpec((TF_HEAD, D_MODEL), lambda f: (f, 0)),
            pl.BlockSpec((TM_FF, D_MODEL), lambda f: (0, 0)),
            cols(V7X_SUBLANES),
        ),
        scratch_shapes=[
            pltpu.VMEM((rows, D_MODEL), BF16), pltpu.VMEM((rows, D_MODEL), F32),
            pltpu.VMEM((TM_FF, D_MODEL), F32),
            pltpu.VMEM((TM_FF + V7X_SUBLANES, TF_HEAD), F32),
            pltpu.SemaphoreType.DMA(()),
        ],
        compiler_params=_params("arbitrary"),
        name="conv_ffn_head",
    )(xs, mods, g2, w_gate, w_up, w_down, cw, cb, p0, p1, g_final, x, h2, mod)


def _s_in_proj_kernel(x_ref, mod_ref, g1_ref, w_ref, z_ref, wb_ref, h_ref):
    @pl.when(pl.program_id(0) == 0)
    def _():
        h_ref[...] = (_rms(x_ref[...], g1_ref[...]) * (1.0 + mod_ref[1]) + mod_ref[0]).astype(BF16)

    w = w_ref[...].astype(BF16)
    wb_ref[...] = w
    z_ref[...] = _dot(h_ref[...], w)


def _s_in_proj(layer, x, mod, g1, w_in):
    rows = x.shape[0]
    full = lambda shape: pl.BlockSpec(shape, lambda j: (0,) * len(shape))
    return pl.pallas_call(
        _s_in_proj_kernel,
        out_shape=(jax.ShapeDtypeStruct((rows, IN_COLS), F32), jax.ShapeDtypeStruct((D_MODEL, IN_COLS), BF16)),
        grid=(IN_COLS // TN_IN,),
        in_specs=[
            full((rows, D_MODEL)), full((6, rows, D_MODEL)), full((1, D_MODEL)),
            pl.BlockSpec((None, D_MODEL, TN_IN), lambda j: (layer, 0, j)),
        ],
        out_specs=(pl.BlockSpec((rows, TN_IN), lambda j: (0, j)), pl.BlockSpec((D_MODEL, TN_IN), lambda j: (0, j))),
        scratch_shapes=[pltpu.VMEM((rows, D_MODEL), BF16)],
        compiler_params=_params("arbitrary"),
        name="sample_in_proj",
    )(x, mod, g1, w_in)


def _s_attn_kernel(sink_ref, z_ref, cos_ref, sin_ref, gain_ref, ws0_ref, bs0_ref, kt_ref, vt_ref,
                   attn_ref, gm_ref, gvn_ref, ktn_ref, vtn_ref):
    nseq = z_ref.shape[0]
    q_blocks, k_blocks, u, gvn = _post_project(
        z_ref[:, Q0:K0], z_ref[:, K0:V0], z_ref[:, U0:G0], z_ref[:, G0:IN_COLS],
        cos_ref[...], sin_ref[...], gain_ref[...], pair_q=False)
    v_new = z_ref[:, V0:U0]
    k_new = jnp.concatenate(k_blocks, axis=1)
    gvn_ref[...] = gvn
    gm_ref[...] = (u * (ws0_ref[...].astype(BF16).astype(F32) * gvn.astype(BF16).astype(F32) + bs0_ref[...])).astype(BF16)

    rows = GQA_GROUP * nseq
    keys = nseq * WINDOW
    seq_of_row = jnp.bitwise_and(_iota((rows, 1), 0), nseq - 1)
    head_of_row = lax.shift_right_logical(_iota((rows, 1), 0), nseq.bit_length() - 1)
    own_keys = lax.shift_right_logical(_iota((rows, keys), 1), WINDOW.bit_length() - 1) == seq_of_row
    own_new = _iota((rows, nseq), 1) == seq_of_row
    pieces = []
    for g in range(N_KV_HEADS):
        gs = slice(g * HEAD_DIM, (g + 1) * HEAD_DIM)
        heads = [g * GQA_GROUP + r for r in range(GQA_GROUP)]
        qg = jnp.concatenate(
            [q_blocks[h // 2][:, (h % 2) * HEAD_DIM:(h % 2 + 1) * HEAD_DIM] for h in heads], axis=0).astype(BF16)
        ktg = jnp.concatenate([kt_ref[b, gs, :] for b in range(nseq)], axis=1).astype(BF16)
        vtg = jnp.concatenate([vt_ref[b, gs, :] for b in range(nseq)], axis=1).astype(BF16)
        s = jnp.where(own_keys, _dot(qg, ktg), NEG)
        sn = jnp.where(own_new, _dot_nt(qg, k_new[:, gs].astype(BF16)), NEG)
        sk = jnp.full((rows, 1), sink_ref[heads[0]], F32)
        for r in range(1, GQA_GROUP):
            sk = jnp.where(head_of_row == r, sink_ref[heads[r]], sk)
        m = jnp.maximum(jnp.maximum(jnp.max(s, axis=-1, keepdims=True), jnp.max(sn, axis=-1, keepdims=True)), sk)
        p = jnp.exp(s - m)
        pn = jnp.exp(sn - m)
        denom = jnp.sum(p, axis=-1, keepdims=True) + jnp.sum(pn, axis=-1, keepdims=True) + jnp.exp(sk - m)
        o = (_dot_nt(p.astype(BF16), vtg) + _dot(pn.astype(BF16), v_new[:, gs].astype(BF16))) / denom
        pieces += [o[r * nseq:(r + 1) * nseq] for r in range(GQA_GROUP)]
    attn_ref[...] = jnp.concatenate(pieces, axis=1).astype(BF16)

    pad = jnp.zeros((V7X_LANES - nseq, KV_WIDTH), F32)
    knt = jnp.concatenate([k_new, pad], axis=0).T
    vnt = jnp.concatenate([v_new, pad], axis=0).T
    newest = _iota((1, WINDOW), 1) == WINDOW - 1
    for b in range(nseq):
        ktn_ref[b] = jnp.where(newest, knt[:, b:b + 1], pltpu.roll(kt_ref[b], WINDOW - 1, axis=1))
        vtn_ref[b] = jnp.where(newest, vnt[:, b:b + 1], pltpu.roll(vt_ref[b], WINDOW - 1, axis=1))


def _s_attention(layer, sinks, z, cos, sin_signed, gain, ws0, bs0, kt_all, vt_all):
    rows = z.shape[0]
    full = lambda shape: pl.BlockSpec(shape, lambda i: (0,) * len(shape))
    cache = pl.BlockSpec((None, rows, KV_WIDTH, WINDOW), lambda i: (layer, 0, 0, 0))
    return pl.pallas_call(
        _s_attn_kernel,
        out_shape=(
            jax.ShapeDtypeStruct((rows, ATTN_WIDTH), BF16),
            jax.ShapeDtypeStruct((rows, GM_WIDTH), BF16),
            jax.ShapeDtypeStruct((rows, GM_WIDTH), F32),
            jax.ShapeDtypeStruct((rows, KV_WIDTH, WINDOW), F32),
            jax.ShapeDtypeStruct((rows, KV_WIDTH, WINDOW), F32),
        ),
        grid=(1,),
        in_specs=[
            pl.BlockSpec(memory_space=pltpu.SMEM),
            full((rows, IN_COLS)), full((1, V7X_LANES)), full((1, V7X_LANES)),
            full((1, GM_WIDTH)), full((1, GM_WIDTH)), full((1, GM_WIDTH)),
            cache, cache,
        ],
        out_specs=(full((rows, ATTN_WIDTH)), full((rows, GM_WIDTH)), full((rows, GM_WIDTH)),
                   full((rows, KV_WIDTH, WINDOW)), full((rows, KV_WIDTH, WINDOW))),
        compiler_params=_params("arbitrary"),
        name="sample_attention",
    )(sinks, z, cos, sin_signed, gain, ws0, bs0, kt_all, vt_all)


def _paired_head(slot):
    p, rest = divmod(slot, 2 * GQA_GROUP)
    j, i = divmod(rest, 2)
    return p * 2 * GQA_GROUP + i * GQA_GROUP + j


def _s_mix_out_kernel(x_ref, gate_ref, a_ref, gm_ref, w_ref, o_ref, wb_ref):
    w = w_ref[...].astype(BF16)
    for slot in range(N_HEADS):
        head = _paired_head(slot)
        wb_ref[slot * HEAD_DIM:(slot + 1) * HEAD_DIM] = w[head * HEAD_DIM:(head + 1) * HEAD_DIM]
    wb_ref[ATTN_WIDTH:] = w[ATTN_WIDTH:]
    mix = _dot(a_ref[...], w[:ATTN_WIDTH]) + _dot(gm_ref[...], w[ATTN_WIDTH:])
    o_ref[...] = x_ref[...] + gate_ref[...] * mix


def _s_mix_out(layer, x, gate, attn, gm, w_out):
    rows = x.shape[0]
    full = lambda shape: pl.BlockSpec(shape, lambda j: (0,) * len(shape))
    col_tile = lambda r: pl.BlockSpec((r, TN_OUT), lambda j: (0, j))
    return pl.pallas_call(
        _s_mix_out_kernel,
        out_shape=(jax.ShapeDtypeStruct((rows, D_MODEL), F32), jax.ShapeDtypeStruct((D_MODEL, D_MODEL), BF16)),
        grid=(D_MODEL // TN_OUT,),
        in_specs=[
            col_tile(rows), col_tile(rows), full((rows, ATTN_WIDTH)), full((rows, GM_WIDTH)),
            pl.BlockSpec((None, D_MODEL, TN_OUT), lambda j: (layer, 0, j)),
        ],
        out_specs=(col_tile(rows), col_tile(D_MODEL)),
        compiler_params=_params("arbitrary"),
        name="sample_mix_out",
    )(x, gate, attn, gm, w_out)


def _rope_tables(pos):
    half = HEAD_DIM // 2
    inv = ROPE_THETA ** (-jnp.arange(half, dtype=jnp.float32) / half)
    ang = pos.astype(jnp.float32)[:, None] * inv[None, :]
    reps = V7X_LANES // half
    sign = jnp.where((jnp.arange(V7X_LANES) % HEAD_DIM) < half, -1.0, 1.0).astype(F32)
    return jnp.tile(jnp.cos(ang), (1, reps)), jnp.tile(jnp.sin(ang), (1, reps)) * sign[None, :]


def _to_planes(window_buf):
    lead = window_buf.shape[:-3]
    return jnp.moveaxis(window_buf, -3, -1).reshape(lead + (KV_WIDTH, WINDOW))


def _from_planes(planes):
    lead = planes.shape[:-2]
    return jnp.moveaxis(planes.reshape(lead + (N_KV_HEADS, HEAD_DIM, WINDOW)), -1, -3)


def kernel(x_prompt, x_sample, cache_k, cache_v, state_conv, c_prompt, c_sample, w_ada, b_ada, g_norm1, g_norm2,
           w_in, gm_gain, gm_ws, gm_bs, sinks, w_out, w_gate, w_up, conv_w, conv_b, w_down, g_final):
    batch, seq, _ = x_prompt.shape
    dec = x_sample.shape[0]
    nf = D_FF // TF
    assert x_sample.shape[1] == 1 and seq % TM == 0 and seq % TQ == 0 and TM % CHUNK == 0 and D_FF % TF == 0
    assert seq // TM_FF >= 2 and seq % TM_FF == 0 and TM_FF % FFN_SUBBLOCKS == 0 and TF % TF_HEAD == 0
    assert dec & (dec - 1) == 0 and dec <= V7X_LANES

    bst = jnp.swapaxes(gm_bs, 1, 2)
    ws0 = jnp.repeat(gm_ws[:, :, 0, 0], GM_WIDTH // GM_HEADS, axis=1)[:, None, :]
    bs0 = jnp.repeat(gm_bs[:, :, 0], GM_WIDTH // GM_HEADS, axis=1)[:, None, :]
    cw_tiles = jnp.swapaxes(conv_w.reshape(DEPTH, 3, nf, TF), 1, 2)
    cb_tiles = conv_b.reshape(DEPTH, nf, 1, TF)
    cw_head = jnp.swapaxes(conv_w.reshape(DEPTH, 3, D_FF // TF_HEAD, TF_HEAD), 1, 2)
    cb_head = conv_b.reshape(DEPTH, D_FF // TF_HEAD, 1, TF_HEAD)

    cos_p, sin_p = _rope_tables(jnp.arange(seq, dtype=jnp.int32))
    cos_s, sin_s = _rope_tables(PAST_LEN + jnp.arange(1, dtype=jnp.int32))

    c_all = jnp.concatenate([c_prompt, c_sample, jnp.zeros((MOD_ROWS - batch - dec, D_MODEL), F32)], axis=0)
    mod = _modulation(c_all, w_ada, b_ada)

    kt_all, vt_all = _to_planes(cache_k), _to_planes(cache_v)

    hp = x_prompt.reshape(batch * seq, D_MODEL)
    hs = x_sample.reshape(dec, D_MODEL)
    g_fin = g_final.reshape(1, D_MODEL)
    outs = [[] for _ in range(8)]
    for l in range(DEPTH):
        last = l == DEPTH - 1
        g1 = g_norm1[l].reshape(1, D_MODEL)
        g2 = g_norm2[l].reshape(1, D_MODEL)
        gain = gm_gain[l].reshape(1, GM_WIDTH)
        mod_p = jnp.swapaxes(mod[l, :, :batch], 0, 1)
        mod_s = mod[l, :, batch:batch + dec]

        z, w_in_b = _s_in_proj(l, hs, mod_s, g1, w_in)
        attn_s, gm_s, gvn_s, kt_new, vt_new = _s_attention(
            l, sinks[l], z, cos_s, sin_s, gain, ws0[l], bs0[l], kt_all, vt_all)
        hs, w_out_b = _s_mix_out(l, hs, mod_s[2], attn_s, gm_s, w_out)
        outs[4].append(kt_new)
        outs[5].append(vt_new)
        outs[6].append(gvn_s[:, None, :])

        q, k, v, gm, gvl = _mix_in(hp, mod_p, g1, w_in_b, cos_p, sin_p, gain, gm_ws[l], bst[l], seq)
        attn, kt_p, vt_p = _attention(sinks[l], q, k, v, seq)
        hp, h2 = _mix_out(hp, mod_p, g2, attn, gm, w_out_b, seq)
        outs[0].append(kt_p)
        outs[1].append(vt_p)
        outs[2].append(gvl)

        hs, a_new, wg_b, wu_b, wd_b, head_rows, halo = _ffn_head(
            l, hs, mod_s, g2, w_gate, w_up, w_down, cw_head[l], cb_head[l], state_conv[l, :, 0], state_conv[l, :, 1],
            g_fin, hp, h2, mod_p, last)
        hp, tail = _ffn(hp, head_rows, halo, h2, mod_p, wg_b, wu_b, wd_b, cw_tiles[l], cb_tiles[l], g_fin, seq, last)
        outs[7].append(jnp.stack([state_conv[l, :, 1], a_new], axis=1))
        tails = tail[seq // TM_FF - 2::seq // TM_FF, :, V7X_SUBLANES - 2:, :]
        outs[3].append(jnp.swapaxes(tails, 1, 2).reshape(batch, 2, D_FF))

    st = [jnp.stack(o) for o in outs]
    return (hp.reshape(batch, seq, D_MODEL), hs.reshape(dec, 1, D_MODEL),
            _from_planes(st[0]), _from_planes(st[1]), st[2], st[3],
            _from_planes(st[4]), _from_planes(st[5]), st[6], st[7])
```

```python
import functools

import jax
import jax.numpy as jnp
from jax import lax
from jax.experimental import pallas as pl
from jax.experimental.pallas import tpu as pltpu

D_MODEL = 2048
DEPTH = 2
PAST_LEN = 16384
ATTN_WIDTH = 1024
GM_WIDTH = 1024
HEAD_DIM = 64
N_HEADS = 16
N_KV_HEADS = 4
GQA_GROUP = N_HEADS // N_KV_HEADS
WINDOW = 128
ROPE_THETA = 10000.0
CHUNK = 128
GM_HEADS = 8
D_FF = 5632
KV_WIDTH = N_KV_HEADS * HEAD_DIM
IN_COLS = ATTN_WIDTH + 2 * KV_WIDTH + 2 * GM_WIDTH
EPS = 1e-6
NEG = -1e30
LOG2_E = 1.4426950408889634

Q0, K0, V0, U0, G0 = 0, ATTN_WIDTH, ATTN_WIDTH + KV_WIDTH, ATTN_WIDTH + 2 * KV_WIDTH, ATTN_WIDTH + 2 * KV_WIDTH + GM_WIDTH

V7X_LANES = 128
V7X_SUBLANES = 8
V7X_VMEM_LIMIT_BYTES = 56 * 1024 * 1024

TM = 512
TF = 512
TM_FF = 1024
TF_HEAD = 256
TQ = 1024
TN_IN = 512
TN_OUT = 512
MOD_TN = 1024
MOD_ROWS = 48
FFN_SUBBLOCKS = 4
MIX_OUT_SUBBLOCKS = 2

BF16 = jnp.bfloat16
F32 = jnp.float32


def _params(*semantics):
    return pltpu.CompilerParams(dimension_semantics=semantics, vmem_limit_bytes=V7X_VMEM_LIMIT_BYTES)


def _dot(a, b):
    return jnp.dot(a, b, preferred_element_type=F32)


def _dot_nt(a, b):
    return lax.dot_general(a, b, (((1,), (1,)), ((), ())), preferred_element_type=F32)


def _rms(x, g):
    return x * lax.rsqrt(jnp.mean(x * x, axis=-1, keepdims=True) + EPS) * g


def _resident(shape):
    zeros = (0,) * len(shape)
    return pl.BlockSpec(shape, lambda *_: zeros, pipeline_mode=pl.Buffered(1))


def _iota(shape, dim):
    return lax.broadcasted_iota(jnp.int32, shape, dim)


def _mod_columns(c_ref, w_ref, b_ref, o_ref):
    s = jax.nn.silu(c_ref[...]).astype(BF16)
    o_ref[...] = _dot(s, w_ref[...].astype(BF16)) + b_ref[...]


def _mod_rider_specs(layer, first_col, width, step_of):
    assert first_col % width == 0
    first = first_col // width
    in_specs = [
        pl.BlockSpec((MOD_ROWS, D_MODEL), lambda *g: (0, 0)),
        pl.BlockSpec((None, D_MODEL, width), lambda *g: (layer, 0, first + step_of(*g))),
        pl.BlockSpec((None, 1, width), lambda *g: (layer, 0, first + step_of(*g))),
    ]
    return in_specs, pl.BlockSpec((MOD_ROWS, width), lambda *g: (0, step_of(*g)))


def _modulation_head(c_all, w_ada, b_ada3, n_cols):
    in_specs, out_spec = _mod_rider_specs(0, 0, MOD_TN, lambda j: j)
    return pl.pallas_call(
        _mod_columns,
        out_shape=jax.ShapeDtypeStruct((MOD_ROWS, n_cols), F32),
        grid=(n_cols // MOD_TN,),
        in_specs=in_specs,
        out_specs=out_spec,
        compiler_params=_params("arbitrary"),
        name="adaln_modulation",
    )(c_all, w_ada, b_ada3)


def _rope_cols(z, cos, sin_signed):
    lane = _iota((1, V7X_LANES), 1)
    first_half = jnp.bitwise_and(lane, HEAD_DIM - 1) < HEAD_DIM // 2
    outs = []
    for c in range(z.shape[1] // V7X_LANES):
        zc = z[:, c * V7X_LANES:(c + 1) * V7X_LANES]
        partner = jnp.where(first_half,
                            pltpu.roll(zc, V7X_LANES - HEAD_DIM // 2, axis=1),
                            pltpu.roll(zc, HEAD_DIM // 2, axis=1))
        outs.append(zc * cos + partner * sin_signed)
    return outs


def _swap_halves(x):
    return pltpu.roll(x, HEAD_DIM, axis=1)


def _pair_q_blocks(nat):
    low = _iota((1, V7X_LANES), 1) < HEAD_DIM
    out = []
    for c in range(len(nat)):
        p, j = divmod(c, 4)
        a, b = nat[p * 4 + j // 2], nat[p * 4 + 2 + j // 2]
        out.append(jnp.where(low, a, _swap_halves(b)) if j % 2 == 0 else jnp.where(low, _swap_halves(a), b))
    return out


def _post_project(zq, zk, zu, zg, cos, sin_signed, gain, pair_q, q_scale):
    q_blocks = _rope_cols(zq, cos, sin_signed)
    if pair_q:
        q_blocks = _pair_q_blocks(q_blocks)
    q_blocks = [qb * q_scale for qb in q_blocks]
    k_blocks = _rope_cols(zk, cos, sin_signed)
    u = jax.nn.gelu(zu)
    gvn = _rms(jax.nn.gelu(zg), gain)
    return q_blocks, k_blocks, u, gvn


def _store_cols(ref, blocks):
    for c, blk in enumerate(blocks):
        ref[:, c * V7X_LANES:(c + 1) * V7X_LANES] = blk.astype(ref.dtype)


def _mix_in_kernel(x_ref, mod_ref, g1_ref, w_ref, cos_ref, sin_ref, gain_ref, ws_ref, bst_ref, *rest, mod_rider):
    if mod_rider:
        c_ref, wa_ref, ba_ref, q_ref, k_ref, v_ref, gm_ref, gvl_ref, mo_ref = rest
        _mod_columns(c_ref, wa_ref, ba_ref, mo_ref)
    else:
        q_ref, k_ref, v_ref, gm_ref, gvl_ref = rest
    tm = x_ref.shape[0]
    h = (_rms(x_ref[...], g1_ref[...]) * (1.0 + mod_ref[1:2]) + mod_ref[0:1]).astype(BF16)
    q_blocks, k_blocks, u, gvn = _post_project(
        _dot(h, w_ref[:, Q0:K0]), _dot(h, w_ref[:, K0:V0]), _dot(h, w_ref[:, U0:G0]), _dot(h, w_ref[:, G0:IN_COLS]),
        cos_ref[...], sin_ref[...], gain_ref[...], pair_q=True, q_scale=HEAD_DIM ** -0.5 * LOG2_E)
    _store_cols(q_ref, q_blocks)
    _store_cols(k_ref, k_blocks)
    v_ref[...] = _dot(h, w_ref[:, V0:U0])
    gvl_ref[...] = gvn[tm - CHUNK:, :]
    gvb = gvn.astype(BF16)
    causal = _iota((CHUNK, CHUNK), 1) <= _iota((CHUNK, CHUNK), 0)
    for hh in range(GM_HEADS):
        wm = jnp.where(causal, ws_ref[hh], 0.0).astype(BF16)
        bias = bst_ref[:, hh:hh + 1]
        cs = slice(hh * CHUNK, (hh + 1) * CHUNK)
        for c in range(tm // CHUNK):
            rs = slice(c * CHUNK, (c + 1) * CHUNK)
            s = _dot(wm, gvb[rs, cs]) + bias
            gm_ref[rs, cs] = (u[rs, cs] * s).astype(BF16)


def _mix_in(x, mod, g1, w_in, cos, sin_signed, gain, ws, bst, seq, mod_rider=None):
    tokens = x.shape[0]
    tiles_per_seq = seq // TM
    batch = tokens // seq
    steps = tokens // TM
    row_tile = lambda width: pl.BlockSpec((TM, width), lambda i: (i, 0))
    rider_in, rider_out, rider_shape, rider_args = [], (), (), ()
    if mod_rider is not None:
        c_all, w_ada, b_ada3, layer, first_col, n_cols = mod_rider
        rider_in, out_spec = _mod_rider_specs(layer, first_col, n_cols // steps, lambda i: i)
        rider_out, rider_shape = (out_spec,), (jax.ShapeDtypeStruct((MOD_ROWS, n_cols), F32),)
        rider_args = (c_all, w_ada, b_ada3)
    return pl.pallas_call(
        functools.partial(_mix_in_kernel, mod_rider=mod_rider is not None),
        out_shape=(
            jax.ShapeDtypeStruct((tokens, ATTN_WIDTH), BF16),
            jax.ShapeDtypeStruct((tokens, KV_WIDTH), F32),
            jax.ShapeDtypeStruct((tokens, KV_WIDTH), F32),
            jax.ShapeDtypeStruct((tokens, GM_WIDTH), BF16),
            jax.ShapeDtypeStruct((batch, CHUNK, GM_WIDTH), F32),
        ) + rider_shape,
        grid=(steps,),
        in_specs=[
            row_tile(D_MODEL),
            pl.BlockSpec((None, mod.shape[1], D_MODEL), lambda i: (i // tiles_per_seq, 0, 0)),
            _resident((1, D_MODEL)),
            _resident((D_MODEL, IN_COLS)),
            pl.BlockSpec((TM, V7X_LANES), lambda i: (i % tiles_per_seq, 0)),
            pl.BlockSpec((TM, V7X_LANES), lambda i: (i % tiles_per_seq, 0)),
            _resident((1, GM_WIDTH)),
            _resident((GM_HEADS, CHUNK, CHUNK)),
            _resident((CHUNK, GM_HEADS)),
        ] + rider_in,
        out_specs=(
            row_tile(ATTN_WIDTH), row_tile(KV_WIDTH), row_tile(KV_WIDTH), row_tile(GM_WIDTH),
            pl.BlockSpec((None, CHUNK, GM_WIDTH), lambda i: (i // tiles_per_seq, 0, 0)),
        ) + rider_out,
        compiler_params=_params("arbitrary"),
        name="mix_in",
    )(x, mod, g1, w_in, cos, sin_signed, gain, ws, bst, *rider_args)


def _attn_block(n, sink_ref, q_blk, k_prev, k_cur, v_prev, v_cur, out_blk):
    row = _iota((WINDOW, 2 * WINDOW), 0)
    key = _iota((WINDOW, 2 * WINDOW), 1) - WINDOW
    valid = (key <= row) & (key >= jnp.maximum(row - WINDOW, -n * WINDOW))
    low = _iota((WINDOW, V7X_LANES), 1) < HEAD_DIM
    cols_per_pair = ATTN_WIDTH // V7X_LANES // 2
    for p in range(2):
        ls = slice(p * V7X_LANES, (p + 1) * V7X_LANES)
        kcat = jnp.concatenate([k_prev[:, ls], k_cur[:, ls]], axis=0).astype(BF16)
        vcat = jnp.concatenate([v_prev[:, ls], v_cur[:, ls]], axis=0).astype(BF16)
        blocks = []
        for cc in range(cols_per_pair):
            c = p * cols_per_pair + cc
            qcol = q_blk[:, c * V7X_LANES:(c + 1) * V7X_LANES]
            zero = jnp.zeros_like(qcol)
            blocks += [jnp.where(low, qcol, zero), jnp.where(low, zero, qcol)]
        s = _dot_nt(jnp.concatenate(blocks, axis=0), kcat)
        probs, inv = [], []
        for j in range(2 * cols_per_pair):
            cc, e = divmod(j, 2)
            sj = jnp.where(valid, s[j * WINDOW:(j + 1) * WINDOW], NEG)
            sk = sink_ref[p * 2 * GQA_GROUP + e * GQA_GROUP + cc] * LOG2_E
            m = jnp.maximum(jnp.max(sj, axis=-1, keepdims=True), sk)
            pj = jnp.exp2(sj - m)
            inv.append(1.0 / (jnp.sum(pj, axis=-1, keepdims=True) + jnp.exp2(sk - m)))
            probs.append(pj.astype(BF16))
        o = _dot(jnp.concatenate(probs, axis=0), vcat)
        for cc in range(cols_per_pair):
            c = p * cols_per_pair + cc
            even = o[(2 * cc) * WINDOW:(2 * cc + 1) * WINDOW] * inv[2 * cc]
            odd = o[(2 * cc + 1) * WINDOW:(2 * cc + 2) * WINDOW] * inv[2 * cc + 1]
            out_blk[:, c * V7X_LANES:(c + 1) * V7X_LANES] = jnp.where(low, even, odd).astype(BF16)


def _attn_kernel(sink_ref, q_ref, kp_ref, kc_ref, vp_ref, vc_ref, *rest, mod_rider):
    if mod_rider:
        c_ref, wa_ref, ba_ref, o_ref, kt_ref, vt_ref, mo_ref = rest
        _mod_columns(c_ref, wa_ref, ba_ref, mo_ref)
    else:
        o_ref, kt_ref, vt_ref = rest
    step = pl.program_id(1)
    blocks = q_ref.shape[0] // WINDOW
    for t in range(blocks):
        rows = slice(t * WINDOW, (t + 1) * WINDOW)
        prev = slice((t - 1) * WINDOW, t * WINDOW)
        k_prev = kp_ref[...] if t == 0 else kc_ref[prev]
        v_prev = vp_ref[...] if t == 0 else vc_ref[prev]
        _attn_block(step * blocks + t, sink_ref, q_ref.at[rows], k_prev, kc_ref[rows], v_prev, vc_ref[rows], o_ref.at[rows])

    @pl.when(step == pl.num_programs(1) - 1)
    def _():
        last = slice((blocks - 1) * WINDOW, blocks * WINDOW)
        kt_ref[...] = kc_ref[last].T
        vt_ref[...] = vc_ref[last].T


def _attention(sinks, q, k, v, seq, mod_rider=None):
    tokens = q.shape[0]
    batch = tokens // seq
    steps = seq // TQ
    per_step = TQ // WINDOW
    cur = lambda b, n: (b * steps + n, 0)
    prev = lambda b, n: (jnp.maximum((b * steps + n) * per_step - 1, 0), 0)
    state = pl.BlockSpec((None, KV_WIDTH, WINDOW), lambda b, n: (b, 0, 0))
    rider_in, rider_out, rider_shape, rider_args = [], (), (), ()
    if mod_rider is not None:
        c_all, w_ada, b_ada3, layer, first_col, n_cols = mod_rider
        rider_in, out_spec = _mod_rider_specs(layer, first_col, n_cols // (batch * steps), lambda b, n: b * steps + n)
        rider_out, rider_shape = (out_spec,), (jax.ShapeDtypeStruct((MOD_ROWS, n_cols), F32),)
        rider_args = (c_all, w_ada, b_ada3)
    return pl.pallas_call(
        functools.partial(_attn_kernel, mod_rider=mod_rider is not None),
        out_shape=(
            jax.ShapeDtypeStruct((tokens, ATTN_WIDTH), BF16),
            jax.ShapeDtypeStruct((batch, KV_WIDTH, WINDOW), F32),
            jax.ShapeDtypeStruct((batch, KV_WIDTH, WINDOW), F32),
        ) + rider_shape,
        grid=(batch, steps),
        in_specs=[
            pl.BlockSpec(memory_space=pltpu.SMEM),
            pl.BlockSpec((TQ, ATTN_WIDTH), cur),
            pl.BlockSpec((WINDOW, KV_WIDTH), prev),
            pl.BlockSpec((TQ, KV_WIDTH), cur),
            pl.BlockSpec((WINDOW, KV_WIDTH), prev),
            pl.BlockSpec((TQ, KV_WIDTH), cur),
        ] + rider_in,
        out_specs=(pl.BlockSpec((TQ, ATTN_WIDTH), cur), state, state) + rider_out,
        compiler_params=_params("arbitrary", "arbitrary"),
        name="swa_attention",
    )(sinks, q, k, k, v, v, *rider_args)


def _mix_out_kernel(x_ref, mod_ref, g2_ref, a_ref, gm_ref, wa_ref, wb_ref, o_ref, h2_ref):
    sub = x_ref.shape[0] // MIX_OUT_SUBBLOCKS
    for r in range(MIX_OUT_SUBBLOCKS):
        rs = slice(r * sub, (r + 1) * sub)
        mix = _dot(a_ref[rs], wa_ref[...]) + _dot(gm_ref[rs], wb_ref[...])
        x1 = x_ref[rs] + mod_ref[2:3] * mix
        o_ref[rs] = x1
        h2_ref[rs] = (_rms(x1, g2_ref[...]) * (1.0 + mod_ref[4:5]) + mod_ref[3:4]).astype(BF16)


def _mix_out(x, mod, g2, attn, gm, w_out, seq):
    tokens = x.shape[0]
    tiles_per_seq = seq // TM
    row_tile = lambda width: pl.BlockSpec((TM, width), lambda i: (i, 0))
    half = lambda k: pl.BlockSpec((ATTN_WIDTH, D_MODEL), lambda i: (k, 0), pipeline_mode=pl.Buffered(1))
    return pl.pallas_call(
        _mix_out_kernel,
        out_shape=(jax.ShapeDtypeStruct((tokens, D_MODEL), F32), jax.ShapeDtypeStruct((tokens, D_MODEL), BF16)),
        grid=(tokens // TM,),
        in_specs=[
            row_tile(D_MODEL),
            pl.BlockSpec((None, 6, D_MODEL), lambda i: (i // tiles_per_seq, 0, 0)),
            _resident((1, D_MODEL)),
            row_tile(ATTN_WIDTH), row_tile(GM_WIDTH),
            half(0), half(1),
        ],
        out_specs=(row_tile(D_MODEL), row_tile(D_MODEL)),
        compiler_params=_params("arbitrary"),
        name="mix_out",
    )(x, mod, g2, attn, gm, w_out, w_out)


def _ffn_tile(h2_ref, ext_ref, acc_ref, wg_parts, wu_parts, wd, cw, cb, riders=None):
    tm = h2_ref.shape[0]
    sub = tm // FFN_SUBBLOCKS
    width = ext_ref.shape[1] // len(wg_parts)
    ups = []
    rider_a = rider_up = None
    for r in range(FFN_SUBBLOCKS):
        h2 = h2_ref[r * sub:(r + 1) * sub]
        ride = riders is not None and r == FFN_SUBBLOCKS - 1
        if ride:
            h2 = jnp.concatenate([h2, riders[0]], axis=0)
        rows = slice(V7X_SUBLANES + r * sub, V7X_SUBLANES + (r + 1) * sub)
        a_parts = [_dot(h2, wg) for wg in wg_parts]
        up = jnp.concatenate([_dot(h2, wu) for wu in wu_parts], axis=1)
        for c, a in enumerate(a_parts):
            ext_ref[rows, c * width:(c + 1) * width] = a[:sub]
        ups.append(up[:sub])
        if ride:
            rider_a, rider_up = jnp.concatenate([a[sub:] for a in a_parts], axis=1), up[sub:]
    for r in range(FFN_SUBBLOCKS):
        lo = V7X_SUBLANES + r * sub
        conv = (cb + cw[0:1] * ext_ref[lo - 2:lo - 2 + sub]
                + cw[1:2] * ext_ref[lo - 1:lo - 1 + sub] + cw[2:3] * ext_ref[lo:lo + sub])
        y = (jax.nn.silu(conv) * ups[r]).astype(BF16)
        if riders is not None and r == FFN_SUBBLOCKS - 1:
            out = _dot(jnp.concatenate([y, riders[1](rider_a, rider_up)], axis=0), wd)
            acc_ref[r * sub:(r + 1) * sub] += out[:sub]
            riders[2][...] += out[sub:]
        else:
            acc_ref[r * sub:(r + 1) * sub] += _dot(y, wd)


def _ffn_finish(o_ref, acc_ref, gate, gf_ref, final_norm):
    sub = o_ref.shape[0] // FFN_SUBBLOCKS

    def finish(r, carry):
        rs = pl.ds(pl.multiple_of(r * sub, sub), sub)
        x2 = o_ref[rs, :] + gate * acc_ref[rs, :]
        o_ref[rs, :] = _rms(x2, gf_ref[...]) if final_norm else x2
        return carry

    lax.fori_loop(0, FFN_SUBBLOCKS, finish, 0)


def _ffn_kernel(x_hbm, head_hbm, h2_ref, mod_ref, wg_ref, wu_ref, wd_ref, cw_ref, cb_ref, halo_ref, gf_ref,
                o_hbm, tail_ref, obuf_ref, acc_ref, ext_ref, carry_ref, x_sem, wb_sem, head_sem,
                *, first_tile, tiles_per_seq, final_norm):
    i = pl.program_id(0)
    f = pl.program_id(1)
    tm = h2_ref.shape[0]
    tile = i + first_tile
    last = f == pl.num_programs(1) - 1
    rows = pl.ds(pl.multiple_of(tile * tm, tm), tm)
    x_copy = pltpu.make_async_copy(x_hbm.at[rows], obuf_ref, x_sem)
    write_back = pltpu.make_async_copy(obuf_ref, o_hbm.at[rows], wb_sem)
    head_copy = pltpu.make_async_copy(head_hbm, o_hbm.at[pl.ds(0, first_tile * tm)], head_sem)

    @pl.when((i == 0) & (f == 0))
    def _():
        head_copy.start()

    @pl.when(last & (i > 0))
    def _():
        write_back.wait()

    @pl.when(last)
    def _():
        x_copy.start()

    @pl.when(f == 0)
    def _():
        acc_ref[...] = jnp.zeros_like(acc_ref)

    seq_start = (tile % tiles_per_seq) == 0

    @pl.when(seq_start)
    def _():
        ext_ref[0:V7X_SUBLANES] = jnp.zeros((V7X_SUBLANES, ext_ref.shape[1]), F32)

    @pl.when(jnp.logical_not(seq_start) & (i == 0))
    def _():
        ext_ref[0:V7X_SUBLANES] = halo_ref[...]

    @pl.when(jnp.logical_not(seq_start) & (i > 0))
    def _():
        ext_ref[0:V7X_SUBLANES] = carry_ref[f]

    parts = range(wg_ref.shape[0])
    _ffn_tile(h2_ref, ext_ref, acc_ref, [wg_ref[c] for c in parts], [wu_ref[c] for c in parts], wd_ref[...],
              cw_ref[f], cb_ref[f])
    tail = ext_ref[tm:tm + V7X_SUBLANES]
    carry_ref[f] = tail
    tail_ref[f] = tail

    @pl.when(last)
    def _():
        x_copy.wait()
        _ffn_finish(obuf_ref, acc_ref, mod_ref[5:6], gf_ref, final_norm)
        write_back.start()

    @pl.when(last & (i == pl.num_programs(0) - 1))
    def _():
        write_back.wait()
        head_copy.wait()


def _ffn(x, head_rows, halo, h2, mod, wg, wu, wd, cw, cb, g_final, seq, final_norm):
    tokens = x.shape[0]
    tiles_per_seq = seq // TM_FF
    nf = D_FF // TF
    first_tile = 1
    kern = functools.partial(_ffn_kernel, first_tile=first_tile, tiles_per_seq=tiles_per_seq, final_norm=final_norm)
    return pl.pallas_call(
        kern,
        out_shape=(
            jax.ShapeDtypeStruct((tokens, D_MODEL), F32),
            jax.ShapeDtypeStruct((tokens // TM_FF - first_tile, nf, V7X_SUBLANES, TF), F32),
        ),
        grid=(tokens // TM_FF - first_tile, nf),
        in_specs=[
            pl.BlockSpec(memory_space=pl.ANY),
            pl.BlockSpec(memory_space=pl.ANY),
            pl.BlockSpec((TM_FF, D_MODEL), lambda i, f: (i + first_tile, 0)),
            pl.BlockSpec((None, 6, D_MODEL), lambda i, f: ((i + first_tile) // tiles_per_seq, 0, 0)),
            pl.BlockSpec((TF // TF_HEAD, D_MODEL, TF_HEAD), lambda i, f: (f, 0, 0)),
            pl.BlockSpec((TF // TF_HEAD, D_MODEL, TF_HEAD), lambda i, f: (f, 0, 0)),
            pl.BlockSpec((TF, D_MODEL), lambda i, f: (f, 0)),
            _resident((nf, 3, TF)),
            _resident((nf, 1, TF)),
            pl.BlockSpec((V7X_SUBLANES, TF), lambda i, f: (0, f)),
            _resident((1, D_MODEL)),
        ],
        out_specs=(
            pl.BlockSpec(memory_space=pl.ANY),
            pl.BlockSpec((None, nf, V7X_SUBLANES, TF), lambda i, f: (i, 0, 0, 0)),
        ),
        scratch_shapes=[
            pltpu.VMEM((TM_FF, D_MODEL), F32),
            pltpu.VMEM((TM_FF, D_MODEL), F32),
            pltpu.VMEM((TM_FF + V7X_SUBLANES, TF), F32),
            pltpu.VMEM((nf, V7X_SUBLANES, TF), F32),
            pltpu.SemaphoreType.DMA(()),
            pltpu.SemaphoreType.DMA(()),
            pltpu.SemaphoreType.DMA(()),
        ],
        compiler_params=_params("arbitrary", "arbitrary"),
        name="conv_ffn",
    )(x, head_rows, h2, mod, wg, wu, wd, cw, cb, halo, g_final)


def _ffn_head_kernel(xs_ref, mods_ref, g2_ref, wg_ref, wu_ref, wd_ref, cw_ref, cb_ref, p0_ref, p1_ref, gf_ref,
                     x_hbm, h2_ref, mod_ref,
                     os_ref, a_ref, wgb_ref, wub_ref, wdb_ref, o_ref, halo_ref,
                     h2s_ref, accs_ref, acc_ref, ext_ref, x_sem, *, final_norm):
    f = pl.program_id(0)
    tm = h2_ref.shape[0]
    last = f == pl.num_programs(0) - 1
    x_copy = pltpu.make_async_copy(x_hbm.at[pl.ds(0, tm)], o_ref, x_sem)

    @pl.when(last)
    def _():
        x_copy.start()

    @pl.when(f == 0)
    def _():
        h2s_ref[...] = (_rms(xs_ref[...], g2_ref[...]) * (1.0 + mods_ref[4]) + mods_ref[3]).astype(BF16)
        accs_ref[...] = jnp.zeros_like(accs_ref)
        acc_ref[...] = jnp.zeros_like(acc_ref)

    wg, wu, wd = wg_ref[...].astype(BF16), wu_ref[...].astype(BF16), wd_ref[...].astype(BF16)
    wgb_ref[...] = wg
    wub_ref[...] = wu
    wdb_ref[...] = wd
    cw, cb = cw_ref[f], cb_ref[f]

    def sample_gate(a, up):
        a_ref[...] = a
        conv = cb + cw[0:1] * p0_ref[...] + cw[1:2] * p1_ref[...] + cw[2:3] * a
        return (jax.nn.silu(conv) * up).astype(BF16)

    ext_ref[0:V7X_SUBLANES] = jnp.zeros((V7X_SUBLANES, ext_ref.shape[1]), F32)
    _ffn_tile(h2_ref, ext_ref, acc_ref, [wg], [wu], wd, cw, cb, riders=(h2s_ref[...], sample_gate, accs_ref))
    halo_ref[...] = ext_ref[tm:tm + V7X_SUBLANES]

    @pl.when(last)
    def _():
        x2 = xs_ref[...] + mods_ref[5] * accs_ref[...]
        os_ref[...] = _rms(x2, gf_ref[...]) if final_norm else x2
        x_copy.wait()
        _ffn_finish(o_ref, acc_ref, mod_ref[5:6], gf_ref, final_norm)


def _ffn_head(layer, xs, mods, g2, w_gate, w_up, w_down, cw, cb, p0, p1, g_final, x, h2, mod, final_norm):
    rows = xs.shape[0]
    tokens = x.shape[0]
    nfh = D_FF // TF_HEAD
    full = lambda shape: pl.BlockSpec(shape, lambda f: (0,) * len(shape))
    cols = lambda r: pl.BlockSpec((r, TF_HEAD), lambda f: (0, f))
    tile_copy = pl.BlockSpec((None, D_MODEL, TF_HEAD), lambda f: (f, 0, 0))
    return pl.pallas_call(
        functools.partial(_ffn_head_kernel, final_norm=final_norm),
        out_shape=(
            jax.ShapeDtypeStruct((rows, D_MODEL), F32), jax.ShapeDtypeStruct((rows, D_FF), F32),
            jax.ShapeDtypeStruct((nfh, D_MODEL, TF_HEAD), BF16), jax.ShapeDtypeStruct((nfh, D_MODEL, TF_HEAD), BF16),
            jax.ShapeDtypeStruct((D_FF, D_MODEL), BF16),
            jax.ShapeDtypeStruct((TM_FF, D_MODEL), F32),
            jax.ShapeDtypeStruct((V7X_SUBLANES, D_FF), F32),
        ),
        grid=(nfh,),
        in_specs=[
            full((rows, D_MODEL)), full((6, rows, D_MODEL)), full((1, D_MODEL)),
            pl.BlockSpec((None, D_MODEL, TF_HEAD), lambda f: (layer, 0, f)),
            pl.BlockSpec((None, D_MODEL, TF_HEAD), lambda f: (layer, 0, f)),
            pl.BlockSpec((None, TF_HEAD, D_MODEL), lambda f: (layer, f, 0)),
            full((nfh, 3, TF_HEAD)), full((nfh, 1, TF_HEAD)),
            cols(rows), cols(rows),
            full((1, D_MODEL)),
            pl.BlockSpec(memory_space=pl.ANY),
            pl.BlockSpec((TM_FF, D_MODEL), lambda f: (0, 0), pipeline_mode=pl.Buffered(1)),
            pl.BlockSpec((None, 6, D_MODEL), lambda f: (0, 0, 0)),
        ],
        out_specs=(
            full((rows, D_MODEL)), cols(rows),
            tile_copy, tile_copy,
            pl.BlockSpec((TF_HEAD, D_MODEL), lambda f: (f, 0)),
            pl.BlockSpec((TM_FF, D_MODEL), lambda f: (0, 0)),
            cols(V7X_SUBLANES),
        ),
        scratch_shapes=[
            pltpu.VMEM((rows, D_MODEL), BF16), pltpu.VMEM((rows, D_MODEL), F32),
            pltpu.VMEM((TM_FF, D_MODEL), F32),
            pltpu.VMEM((TM_FF + V7X_SUBLANES, TF_HEAD), F32),
            pltpu.SemaphoreType.DMA(()),
        ],
        compiler_params=_params("arbitrary"),
        name="conv_ffn_head",
    )(xs, mods, g2, w_gate, w_up, w_down, cw, cb, p0, p1, g_final, x, h2, mod)


def _s_in_proj_kernel(x_ref, mod_ref, g1_ref, w_ref, z_ref, wb_ref, h_ref):
    @pl.when(pl.program_id(0) == 0)
    def _():
        h_ref[...] = (_rms(x_ref[...], g1_ref[...]) * (1.0 + mod_ref[1]) + mod_ref[0]).astype(BF16)

    w = w_ref[...].astype(BF16)
    wb_ref[...] = w
    z_ref[...] = _dot(h_ref[...], w)


def _s_in_proj(layer, x, mod, g1, w_in):
    rows = x.shape[0]
    full = lambda shape: pl.BlockSpec(shape, lambda j: (0,) * len(shape))
    return pl.pallas_call(
        _s_in_proj_kernel,
        out_shape=(jax.ShapeDtypeStruct((rows, IN_COLS), F32), jax.ShapeDtypeStruct((D_MODEL, IN_COLS), BF16)),
        grid=(IN_COLS // TN_IN,),
        in_specs=[
            full((rows, D_MODEL)), full(mod.shape), full((1, D_MODEL)),
            pl.BlockSpec((None, D_MODEL, TN_IN), lambda j: (layer, 0, j)),
        ],
        out_specs=(pl.BlockSpec((rows, TN_IN), lambda j: (0, j)), pl.BlockSpec((D_MODEL, TN_IN), lambda j: (0, j))),
        scratch_shapes=[pltpu.VMEM((rows, D_MODEL), BF16)],
        compiler_params=_params("arbitrary"),
        name="sample_in_proj",
    )(x, mod, g1, w_in)


def _s_attn_kernel(sink_ref, z_ref, cos_ref, sin_ref, gain_ref, ws0_ref, bs0_ref, kt_ref, vt_ref,
                   attn_ref, gm_ref, gvn_ref, ktn_ref, vtn_ref):
    nseq = z_ref.shape[0]
    q_blocks, k_blocks, u, gvn = _post_project(
        z_ref[:, Q0:K0], z_ref[:, K0:V0], z_ref[:, U0:G0], z_ref[:, G0:IN_COLS],
        cos_ref[...], sin_ref[...], gain_ref[...], pair_q=False, q_scale=HEAD_DIM ** -0.5)
    v_new = z_ref[:, V0:U0]
    k_new = jnp.concatenate(k_blocks, axis=1)
    gvn_ref[...] = gvn
    gm_ref[...] = (u * (ws0_ref[...].astype(BF16).astype(F32) * gvn.astype(BF16).astype(F32) + bs0_ref[...])).astype(BF16)

    rows = GQA_GROUP * nseq
    keys = nseq * WINDOW
    seq_of_row = jnp.bitwise_and(_iota((rows, 1), 0), nseq - 1)
    head_of_row = lax.shift_right_logical(_iota((rows, 1), 0), nseq.bit_length() - 1)
    own_keys = lax.shift_right_logical(_iota((rows, keys), 1), WINDOW.bit_length() - 1) == seq_of_row
    own_new = _iota((rows, nseq), 1) == seq_of_row
    pieces = []
    for g in range(N_KV_HEADS):
        gs = slice(g * HEAD_DIM, (g + 1) * HEAD_DIM)
        heads = [g * GQA_GROUP + r for r in range(GQA_GROUP)]
        qg = jnp.concatenate(
            [q_blocks[h // 2][:, (h % 2) * HEAD_DIM:(h % 2 + 1) * HEAD_DIM] for h in heads], axis=0).astype(BF16)
        ktg = jnp.concatenate([kt_ref[b, gs, :] for b in range(nseq)], axis=1).astype(BF16)
        vtg = jnp.concatenate([vt_ref[b, gs, :] for b in range(nseq)], axis=1).astype(BF16)
        s = jnp.where(own_keys, _dot(qg, ktg), NEG)
        sn = jnp.where(own_new, _dot_nt(qg, k_new[:, gs].astype(BF16)), NEG)
        sk = jnp.full((rows, 1), sink_ref[heads[0]], F32)
        for r in range(1, GQA_GROUP):
            sk = jnp.where(head_of_row == r, sink_ref[heads[r]], sk)
        m = jnp.maximum(jnp.maximum(jnp.max(s, axis=-1, keepdims=True), jnp.max(sn, axis=-1, keepdims=True)), sk)
        p = jnp.exp(s - m)
        pn = jnp.exp(sn - m)
        denom = jnp.sum(p, axis=-1, keepdims=True) + jnp.sum(pn, axis=-1, keepdims=True) + jnp.exp(sk - m)
        o = (_dot_nt(p.astype(BF16), vtg) + _dot(pn.astype(BF16), v_new[:, gs].astype(BF16))) / denom
        pieces += [o[r * nseq:(r + 1) * nseq] for r in range(GQA_GROUP)]
    attn_ref[...] = jnp.concatenate(pieces, axis=1).astype(BF16)

    pad = jnp.zeros((V7X_LANES - nseq, KV_WIDTH), F32)
    knt = jnp.concatenate([k_new, pad], axis=0).T
    vnt = jnp.concatenate([v_new, pad], axis=0).T
    newest = _iota((1, WINDOW), 1) == WINDOW - 1
    for b in range(nseq):
        ktn_ref[b] = jnp.where(newest, knt[:, b:b + 1], pltpu.roll(kt_ref[b], WINDOW - 1, axis=1))
        vtn_ref[b] = jnp.where(newest, vnt[:, b:b + 1], pltpu.roll(vt_ref[b], WINDOW - 1, axis=1))


def _s_attention(layer, sinks, z, cos, sin_signed, gain, ws0, bs0, kt_all, vt_all):
    rows = z.shape[0]
    full = lambda shape: pl.BlockSpec(shape, lambda i: (0,) * len(shape))
    cache = pl.BlockSpec((None, rows, KV_WIDTH, WINDOW), lambda i: (layer, 0, 0, 0))
    return pl.pallas_call(
        _s_attn_kernel,
        out_shape=(
            jax.ShapeDtypeStruct((rows, ATTN_WIDTH), BF16),
            jax.ShapeDtypeStruct((rows, GM_WIDTH), BF16),
            jax.ShapeDtypeStruct((rows, GM_WIDTH), F32),
            jax.ShapeDtypeStruct((rows, KV_WIDTH, WINDOW), F32),
            jax.ShapeDtypeStruct((rows, KV_WIDTH, WINDOW), F32),
        ),
        grid=(1,),
        in_specs=[
            pl.BlockSpec(memory_space=pltpu.SMEM),
            full((rows, IN_COLS)), full((1, V7X_LANES)), full((1, V7X_LANES)),
            full((1, GM_WIDTH)), full((1, GM_WIDTH)), full((1, GM_WIDTH)),
            cache, cache,
        ],
        out_specs=(full((rows, ATTN_WIDTH)), full((rows, GM_WIDTH)), full((rows, GM_WIDTH)),
                   full((rows, KV_WIDTH, WINDOW)), full((rows, KV_WIDTH, WINDOW))),
        compiler_params=_params("arbitrary"),
        name="sample_attention",
    )(sinks, z, cos, sin_signed, gain, ws0, bs0, kt_all, vt_all)


def _paired_head(slot):
    p, rest = divmod(slot, 2 * GQA_GROUP)
    j, i = divmod(rest, 2)
    return p * 2 * GQA_GROUP + i * GQA_GROUP + j


def _s_mix_out_kernel(x_ref, gate_ref, a_ref, gm_ref, w_ref, o_ref, wb_ref):
    w = w_ref[...].astype(BF16)
    for slot in range(N_HEADS):
        head = _paired_head(slot)
        wb_ref[slot * HEAD_DIM:(slot + 1) * HEAD_DIM] = w[head * HEAD_DIM:(head + 1) * HEAD_DIM]
    wb_ref[ATTN_WIDTH:] = w[ATTN_WIDTH:]
    mix = _dot(a_ref[...], w[:ATTN_WIDTH]) + _dot(gm_ref[...], w[ATTN_WIDTH:])
    o_ref[...] = x_ref[...] + gate_ref[...] * mix


def _s_mix_out(layer, x, gate, attn, gm, w_out):
    rows = x.shape[0]
    full = lambda shape: pl.BlockSpec(shape, lambda j: (0,) * len(shape))
    col_tile = lambda r: pl.BlockSpec((r, TN_OUT), lambda j: (0, j))
    return pl.pallas_call(
        _s_mix_out_kernel,
        out_shape=(jax.ShapeDtypeStruct((rows, D_MODEL), F32), jax.ShapeDtypeStruct((D_MODEL, D_MODEL), BF16)),
        grid=(D_MODEL // TN_OUT,),
        in_specs=[
            col_tile(rows), col_tile(rows), full((rows, ATTN_WIDTH)), full((rows, GM_WIDTH)),
            pl.BlockSpec((None, D_MODEL, TN_OUT), lambda j: (layer, 0, j)),
        ],
        out_specs=(col_tile(rows), col_tile(D_MODEL)),
        compiler_params=_params("arbitrary"),
        name="sample_mix_out",
    )(x, gate, attn, gm, w_out)


def _rope_tables(pos):
    half = HEAD_DIM // 2
    inv = ROPE_THETA ** (-jnp.arange(half, dtype=jnp.float32) / half)
    ang = pos.astype(jnp.float32)[:, None] * inv[None, :]
    reps = V7X_LANES // half
    sign = jnp.where((jnp.arange(V7X_LANES) % HEAD_DIM) < half, -1.0, 1.0).astype(F32)
    return jnp.tile(jnp.cos(ang), (1, reps)), jnp.tile(jnp.sin(ang), (1, reps)) * sign[None, :]


def _to_planes(window_buf):
    lead = window_buf.shape[:-3]
    return jnp.moveaxis(window_buf, -3, -1).reshape(lead + (KV_WIDTH, WINDOW))


def _from_planes(planes):
    lead = planes.shape[:-2]
    return jnp.moveaxis(planes.reshape(lead + (N_KV_HEADS, HEAD_DIM, WINDOW)), -1, -3)


def kernel(x_prompt, x_sample, cache_k, cache_v, state_conv, c_prompt, c_sample, w_ada, b_ada, g_norm1, g_norm2,
           w_in, gm_gain, gm_ws, gm_bs, sinks, w_out, w_gate, w_up, conv_w, conv_b, w_down, g_final):
    batch, seq, _ = x_prompt.shape
    dec = x_sample.shape[0]
    nf = D_FF // TF
    assert x_sample.shape[1] == 1 and seq % TM == 0 and seq % TQ == 0 and TM % CHUNK == 0 and D_FF % TF == 0
    assert seq // TM_FF >= 2 and seq % TM_FF == 0 and TM_FF % FFN_SUBBLOCKS == 0 and TF % TF_HEAD == 0
    assert dec & (dec - 1) == 0 and dec <= V7X_LANES

    bst = jnp.swapaxes(gm_bs, 1, 2)
    ws0 = jnp.repeat(gm_ws[:, :, 0, 0], GM_WIDTH // GM_HEADS, axis=1)[:, None, :]
    bs0 = jnp.repeat(gm_bs[:, :, 0], GM_WIDTH // GM_HEADS, axis=1)[:, None, :]
    cw_tiles = jnp.swapaxes(conv_w.reshape(DEPTH, 3, nf, TF), 1, 2)
    cb_tiles = conv_b.reshape(DEPTH, nf, 1, TF)
    cw_head = jnp.swapaxes(conv_w.reshape(DEPTH, 3, D_FF // TF_HEAD, TF_HEAD), 1, 2)
    cb_head = conv_b.reshape(DEPTH, D_FF // TF_HEAD, 1, TF_HEAD)

    cos_p, sin_p = _rope_tables(jnp.arange(seq, dtype=jnp.int32))
    cos_s, sin_s = _rope_tables(PAST_LEN + jnp.arange(1, dtype=jnp.int32))

    c_all = jnp.concatenate([c_prompt, c_sample, jnp.zeros((MOD_ROWS - batch - dec, D_MODEL), F32)], axis=0)
    b_ada3 = b_ada.reshape(DEPTH, 1, 6 * D_MODEL)
    prompt_mod = lambda slab: slab[:batch].reshape(batch, -1, D_MODEL)
    sample_mod = lambda slab: jnp.swapaxes(slab[batch:batch + dec].reshape(dec, -1, D_MODEL), 0, 1)
    mod_first = _modulation_head(c_all, w_ada, b_ada3, 2 * D_MODEL)
    mod_next = None

    kt_all, vt_all = _to_planes(cache_k), _to_planes(cache_v)

    hp = x_prompt.reshape(batch * seq, D_MODEL)
    hs = x_sample.reshape(dec, D_MODEL)
    g_fin = g_final.reshape(1, D_MODEL)
    outs = [[] for _ in range(8)]
    for l in range(DEPTH):
        last = l == DEPTH - 1
        g1 = g_norm1[l].reshape(1, D_MODEL)
        g2 = g_norm2[l].reshape(1, D_MODEL)
        gain = gm_gain[l].reshape(1, GM_WIDTH)
        if l == 0:
            mod_p, mod_s = prompt_mod(mod_first), sample_mod(mod_first)
            in_rider = (c_all, w_ada, b_ada3, 0, 2 * D_MODEL, 4 * D_MODEL)
        else:
            mod_p, mod_s = prompt_mod(mod_next), sample_mod(mod_next)
            in_rider = None
        attn_rider = (c_all, w_ada, b_ada3, l + 1, 0, 6 * D_MODEL) if l + 1 < DEPTH else None

        z, w_in_b = _s_in_proj(l, hs, mod_s, g1, w_in)
        attn_s, gm_s, gvn_s, kt_new, vt_new = _s_attention(
            l, sinks[l], z, cos_s, sin_s, gain, ws0[l], bs0[l], kt_all, vt_all)
        mixed = _mix_in(hp, mod_p, g1, w_in_b, cos_p, sin_p, gain, gm_ws[l], bst[l], seq, mod_rider=in_rider)
        q, k, v, gm, gvl = mixed[:5]
        if in_rider is not None:
            slab = jnp.concatenate([mod_first, mixed[5]], axis=1)
            mod_p, mod_s = prompt_mod(slab), sample_mod(slab)
        outs[4].append(kt_new)
        outs[5].append(vt_new)
        outs[6].append(gvn_s[:, None, :])

        hs, w_out_b = _s_mix_out(l, hs, mod_s[2], attn_s, gm_s, w_out)
        attended = _attention(sinks[l], q, k, v, seq, mod_rider=attn_rider)
        attn, kt_p, vt_p = attended[:3]
        if attn_rider is not None:
            mod_next = attended[3]
        hp, h2 = _mix_out(hp, mod_p, g2, attn, gm, w_out_b, seq)
        outs[0].append(kt_p)
        outs[1].append(vt_p)
        outs[2].append(gvl)

        hs, a_new, wg_b, wu_b, wd_b, head_rows, halo = _ffn_head(
            l, hs, mod_s, g2, w_gate, w_up, w_down, cw_head[l], cb_head[l], state_conv[l, :, 0], state_conv[l, :, 1],
            g_fin, hp, h2, mod_p, last)
        hp, tail = _ffn(hp, head_rows, halo, h2, mod_p, wg_b, wu_b, wd_b, cw_tiles[l], cb_tiles[l], g_fin, seq, last)
        outs[7].append(jnp.stack([state_conv[l, :, 1], a_new], axis=1))
        tails = tail[seq // TM_FF - 2::seq // TM_FF, :, V7X_SUBLANES - 2:, :]
        outs[3].append(jnp.swapaxes(tails, 1, 2).reshape(batch, 2, D_FF))

    st = [jnp.stack(o) for o in outs]
    return (hp.reshape(batch, seq, D_MODEL), hs.reshape(dec, 1, D_MODEL),
            _from_planes(st[0]), _from_planes(st[1]), st[2], st[3],
            _from_planes(st[4]), _from_planes(st[5]), st[6], st[7])
```

```python
import functools

import jax
import jax.numpy as jnp
from jax import lax
from jax.experimental import pallas as pl
from jax.experimental.pallas import tpu as pltpu

D_MODEL = 2048
DEPTH = 2
PAST_LEN = 16384
ATTN_WIDTH = 1024
GM_WIDTH = 1024
HEAD_DIM = 64
N_HEADS = 16
N_KV_HEADS = 4
GQA_GROUP = N_HEADS // N_KV_HEADS
WINDOW = 128
ROPE_THETA = 10000.0
CHUNK = 128
GM_HEADS = 8
D_FF = 5632
KV_WIDTH = N_KV_HEADS * HEAD_DIM
IN_COLS = ATTN_WIDTH + 2 * KV_WIDTH + 2 * GM_WIDTH
EPS = 1e-6
NEG = -1e30
LOG2_E = 1.4426950408889634

Q0, K0, V0, U0, G0 = 0, ATTN_WIDTH, ATTN_WIDTH + KV_WIDTH, ATTN_WIDTH + 2 * KV_WIDTH, ATTN_WIDTH + 2 * KV_WIDTH + GM_WIDTH

V7X_LANES = 128
V7X_SUBLANES = 8
V7X_VMEM_LIMIT_BYTES = 56 * 1024 * 1024

TM = 512
TF = 512
TM_FF = 1024
TF_HEAD = 256
TQ = 1024
TN_IN = 512
TN_OUT = 512
MOD_TN = 1024
MOD_ROWS = 48
FFN_SUBBLOCKS = 4
MIX_OUT_SUBBLOCKS = 2

BF16 = jnp.bfloat16
F32 = jnp.float32


def _params(*semantics):
    return pltpu.CompilerParams(dimension_semantics=semantics, vmem_limit_bytes=V7X_VMEM_LIMIT_BYTES)


def _dot(a, b):
    return jnp.dot(a, b, preferred_element_type=F32)


def _dot_nt(a, b):
    return lax.dot_general(a, b, (((1,), (1,)), ((), ())), preferred_element_type=F32)


def _rms(x, g):
    return x * lax.rsqrt(jnp.mean(x * x, axis=-1, keepdims=True) + EPS) * g


def _resident(shape):
    zeros = (0,) * len(shape)
    return pl.BlockSpec(shape, lambda *_: zeros, pipeline_mode=pl.Buffered(1))


def _iota(shape, dim):
    return lax.broadcasted_iota(jnp.int32, shape, dim)


def _mod_columns(c_ref, w_ref, b_ref, o_ref):
    s = jax.nn.silu(c_ref[...]).astype(BF16)
    o_ref[...] = _dot(s, w_ref[...].astype(BF16)) + b_ref[...]


def _mod_rider_specs(layer, first_col, width, step_of):
    assert first_col % width == 0
    first = first_col // width
    in_specs = [
        pl.BlockSpec((MOD_ROWS, D_MODEL), lambda *g: (0, 0)),
        pl.BlockSpec((None, D_MODEL, width), lambda *g: (layer, 0, first + step_of(*g))),
        pl.BlockSpec((None, 1, width), lambda *g: (layer, 0, first + step_of(*g))),
    ]
    return in_specs, pl.BlockSpec((MOD_ROWS, width), lambda *g: (0, step_of(*g)))


def _modulation_head(c_all, w_ada, b_ada3, n_cols):
    in_specs, out_spec = _mod_rider_specs(0, 0, MOD_TN, lambda j: j)
    return pl.pallas_call(
        _mod_columns,
        out_shape=jax.ShapeDtypeStruct((MOD_ROWS, n_cols), F32),
        grid=(n_cols // MOD_TN,),
        in_specs=in_specs,
        out_specs=out_spec,
        compiler_params=_params("arbitrary"),
        name="adaln_modulation",
    )(c_all, w_ada, b_ada3)


def _rope_cols(z, cos, sin_signed):
    lane = _iota((1, V7X_LANES), 1)
    first_half = jnp.bitwise_and(lane, HEAD_DIM - 1) < HEAD_DIM // 2
    outs = []
    for c in range(z.shape[1] // V7X_LANES):
        zc = z[:, c * V7X_LANES:(c + 1) * V7X_LANES]
        partner = jnp.where(first_half,
                            pltpu.roll(zc, V7X_LANES - HEAD_DIM // 2, axis=1),
                            pltpu.roll(zc, HEAD_DIM // 2, axis=1))
        outs.append(zc * cos + partner * sin_signed)
    return outs


def _swap_halves(x):
    return pltpu.roll(x, HEAD_DIM, axis=1)


def _pair_q_blocks(nat):
    low = _iota((1, V7X_LANES), 1) < HEAD_DIM
    out = []
    for c in range(len(nat)):
        p, j = divmod(c, 4)
        a, b = nat[p * 4 + j // 2], nat[p * 4 + 2 + j // 2]
        out.append(jnp.where(low, a, _swap_halves(b)) if j % 2 == 0 else jnp.where(low, _swap_halves(a), b))
    return out


def _post_project(zq, zk, zu, zg, cos, sin_signed, gain, pair_q, q_scale):
    q_blocks = _rope_cols(zq, cos, sin_signed)
    if pair_q:
        q_blocks = _pair_q_blocks(q_blocks)
    q_blocks = [qb * q_scale for qb in q_blocks]
    k_blocks = _rope_cols(zk, cos, sin_signed)
    u = jax.nn.gelu(zu)
    gvn = _rms(jax.nn.gelu(zg), gain)
    return q_blocks, k_blocks, u, gvn


def _store_cols(ref, blocks):
    for c, blk in enumerate(blocks):
        ref[:, c * V7X_LANES:(c + 1) * V7X_LANES] = blk.astype(ref.dtype)


def _mix_in_kernel(x_ref, mod_ref, g1_ref, w_ref, cos_ref, sin_ref, gain_ref, ws_ref, bst_ref, *rest, mod_rider):
    if mod_rider:
        c_ref, wa_ref, ba_ref, q_ref, k_ref, v_ref, gm_ref, gvl_ref, mo_ref = rest
        _mod_columns(c_ref, wa_ref, ba_ref, mo_ref)
    else:
        q_ref, k_ref, v_ref, gm_ref, gvl_ref = rest
    tm = x_ref.shape[0]
    h = (_rms(x_ref[...], g1_ref[...]) * (1.0 + mod_ref[1:2]) + mod_ref[0:1]).astype(BF16)
    q_blocks, k_blocks, u, gvn = _post_project(
        _dot(h, w_ref[:, Q0:K0]), _dot(h, w_ref[:, K0:V0]), _dot(h, w_ref[:, U0:G0]), _dot(h, w_ref[:, G0:IN_COLS]),
        cos_ref[...], sin_ref[...], gain_ref[...], pair_q=True, q_scale=HEAD_DIM ** -0.5 * LOG2_E)
    _store_cols(q_ref, q_blocks)
    _store_cols(k_ref, k_blocks)
    v_ref[...] = _dot(h, w_ref[:, V0:U0])
    gvl_ref[...] = gvn[tm - CHUNK:, :]
    gvb = gvn.astype(BF16)
    causal = _iota((CHUNK, CHUNK), 1) <= _iota((CHUNK, CHUNK), 0)
    for hh in range(GM_HEADS):
        wm = jnp.where(causal, ws_ref[hh], 0.0).astype(BF16)
        bias = bst_ref[:, hh:hh + 1]
        cs = slice(hh * CHUNK, (hh + 1) * CHUNK)
        for c in range(tm // CHUNK):
            rs = slice(c * CHUNK, (c + 1) * CHUNK)
            s = _dot(wm, gvb[rs, cs]) + bias
            gm_ref[rs, cs] = (u[rs, cs] * s).astype(BF16)


def _mix_in(x, mod, g1, w_in, cos, sin_signed, gain, ws, bst, seq, mod_rider=None):
    tokens = x.shape[0]
    tiles_per_seq = seq // TM
    batch = tokens // seq
    steps = tokens // TM
    row_tile = lambda width: pl.BlockSpec((TM, width), lambda i: (i, 0))
    rider_in, rider_out, rider_shape, rider_args = [], (), (), ()
    if mod_rider is not None:
        c_all, w_ada, b_ada3, layer, first_col, n_cols = mod_rider
        rider_in, out_spec = _mod_rider_specs(layer, first_col, n_cols // steps, lambda i: i)
        rider_out, rider_shape = (out_spec,), (jax.ShapeDtypeStruct((MOD_ROWS, n_cols), F32),)
        rider_args = (c_all, w_ada, b_ada3)
    return pl.pallas_call(
        functools.partial(_mix_in_kernel, mod_rider=mod_rider is not None),
        out_shape=(
            jax.ShapeDtypeStruct((tokens, ATTN_WIDTH), BF16),
            jax.ShapeDtypeStruct((tokens, KV_WIDTH), F32),
            jax.ShapeDtypeStruct((tokens, KV_WIDTH), F32),
            jax.ShapeDtypeStruct((tokens, GM_WIDTH), BF16),
            jax.ShapeDtypeStruct((batch, CHUNK, GM_WIDTH), F32),
        ) + rider_shape,
        grid=(steps,),
        in_specs=[
            row_tile(D_MODEL),
            pl.BlockSpec((None, mod.shape[1], D_MODEL), lambda i: (i // tiles_per_seq, 0, 0)),
            _resident((1, D_MODEL)),
            _resident((D_MODEL, IN_COLS)),
            pl.BlockSpec((TM, V7X_LANES), lambda i: (i % tiles_per_seq, 0)),
            pl.BlockSpec((TM, V7X_LANES), lambda i: (i % tiles_per_seq, 0)),
            _resident((1, GM_WIDTH)),
            _resident((GM_HEADS, CHUNK, CHUNK)),
            _resident((CHUNK, GM_HEADS)),
        ] + rider_in,
        out_specs=(
            row_tile(ATTN_WIDTH), row_tile(KV_WIDTH), row_tile(KV_WIDTH), row_tile(GM_WIDTH),
            pl.BlockSpec((None, CHUNK, GM_WIDTH), lambda i: (i // tiles_per_seq, 0, 0)),
        ) + rider_out,
        compiler_params=_params("arbitrary"),
        name="mix_in",
    )(x, mod, g1, w_in, cos, sin_signed, gain, ws, bst, *rider_args)


def _attn_block(n, sink_ref, q_blk, k_prev, k_cur, v_prev, v_cur, out_blk):
    row = _iota((WINDOW, 2 * WINDOW), 0)
    key = _iota((WINDOW, 2 * WINDOW), 1) - WINDOW
    valid = (key <= row) & (key >= jnp.maximum(row - WINDOW, -n * WINDOW))
    low = _iota((WINDOW, V7X_LANES), 1) < HEAD_DIM
    cols_per_pair = ATTN_WIDTH // V7X_LANES // 2
    for p in range(2):
        ls = slice(p * V7X_LANES, (p + 1) * V7X_LANES)
        kcat = jnp.concatenate([k_prev[:, ls], k_cur[:, ls]], axis=0).astype(BF16)
        vcat = jnp.concatenate([v_prev[:, ls], v_cur[:, ls]], axis=0).astype(BF16)
        blocks = []
        for cc in range(cols_per_pair):
            c = p * cols_per_pair + cc
            qcol = q_blk[:, c * V7X_LANES:(c + 1) * V7X_LANES]
            zero = jnp.zeros_like(qcol)
            blocks += [jnp.where(low, qcol, zero), jnp.where(low, zero, qcol)]
        s = _dot_nt(jnp.concatenate(blocks, axis=0), kcat)
        probs, inv = [], []
        for j in range(2 * cols_per_pair):
            cc, e = divmod(j, 2)
            sj = jnp.where(valid, s[j * WINDOW:(j + 1) * WINDOW], NEG)
            sk = sink_ref[p * 2 * GQA_GROUP + e * GQA_GROUP + cc] * LOG2_E
            m = jnp.maximum(jnp.max(sj, axis=-1, keepdims=True), sk)
            pj = jnp.exp2(sj - m)
            inv.append(1.0 / (jnp.sum(pj, axis=-1, keepdims=True) + jnp.exp2(sk - m)))
            probs.append(pj.astype(BF16))
        o = _dot(jnp.concatenate(probs, axis=0), vcat)
        for cc in range(cols_per_pair):
            c = p * cols_per_pair + cc
            even = o[(2 * cc) * WINDOW:(2 * cc + 1) * WINDOW] * inv[2 * cc]
            odd = o[(2 * cc + 1) * WINDOW:(2 * cc + 2) * WINDOW] * inv[2 * cc + 1]
            out_blk[:, c * V7X_LANES:(c + 1) * V7X_LANES] = jnp.where(low, even, odd).astype(BF16)


def _attn_kernel(sink_ref, q_ref, kp_ref, kc_ref, vp_ref, vc_ref, *rest, mod_rider):
    if mod_rider:
        c_ref, wa_ref, ba_ref, o_ref, kt_ref, vt_ref, mo_ref = rest
        _mod_columns(c_ref, wa_ref, ba_ref, mo_ref)
    else:
        o_ref, kt_ref, vt_ref = rest
    step = pl.program_id(1)
    blocks = q_ref.shape[0] // WINDOW
    for t in range(blocks):
        rows = slice(t * WINDOW, (t + 1) * WINDOW)
        prev = slice((t - 1) * WINDOW, t * WINDOW)
        k_prev = kp_ref[...] if t == 0 else kc_ref[prev]
        v_prev = vp_ref[...] if t == 0 else vc_ref[prev]
        _attn_block(step * blocks + t, sink_ref, q_ref.at[rows], k_prev, kc_ref[rows], v_prev, vc_ref[rows], o_ref.at[rows])

    @pl.when(step == pl.num_programs(1) - 1)
    def _():
        last = slice((blocks - 1) * WINDOW, blocks * WINDOW)
        kt_ref[...] = kc_ref[last].T
        vt_ref[...] = vc_ref[last].T


def _attention(sinks, q, k, v, seq, mod_rider=None):
    tokens = q.shape[0]
    batch = tokens // seq
    steps = seq // TQ
    per_step = TQ // WINDOW
    cur = lambda b, n: (b * steps + n, 0)
    prev = lambda b, n: (jnp.maximum((b * steps + n) * per_step - 1, 0), 0)
    state = pl.BlockSpec((None, KV_WIDTH, WINDOW), lambda b, n: (b, 0, 0))
    rider_in, rider_out, rider_shape, rider_args = [], (), (), ()
    if mod_rider is not None:
        c_all, w_ada, b_ada3, layer, first_col, n_cols = mod_rider
        rider_in, out_spec = _mod_rider_specs(layer, first_col, n_cols // (batch * steps), lambda b, n: b * steps + n)
        rider_out, rider_shape = (out_spec,), (jax.ShapeDtypeStruct((MOD_ROWS, n_cols), F32),)
        rider_args = (c_all, w_ada, b_ada3)
    return pl.pallas_call(
        functools.partial(_attn_kernel, mod_rider=mod_rider is not None),
        out_shape=(
            jax.ShapeDtypeStruct((tokens, ATTN_WIDTH), BF16),
            jax.ShapeDtypeStruct((batch, KV_WIDTH, WINDOW), F32),
            jax.ShapeDtypeStruct((batch, KV_WIDTH, WINDOW), F32),
        ) + rider_shape,
        grid=(batch, steps),
        in_specs=[
            pl.BlockSpec(memory_space=pltpu.SMEM),
            pl.BlockSpec((TQ, ATTN_WIDTH), cur),
            pl.BlockSpec((WINDOW, KV_WIDTH), prev),
            pl.BlockSpec((TQ, KV_WIDTH), cur),
            pl.BlockSpec((WINDOW, KV_WIDTH), prev),
            pl.BlockSpec((TQ, KV_WIDTH), cur),
        ] + rider_in,
        out_specs=(pl.BlockSpec((TQ, ATTN_WIDTH), cur), state, state) + rider_out,
        compiler_params=_params("arbitrary", "arbitrary"),
        name="swa_attention",
    )(sinks, q, k, k, v, v, *rider_args)


def _mix_out_kernel(x_ref, mod_ref, g2_ref, a_ref, gm_ref, wa_ref, wb_ref, o_ref, h2_ref):
    sub = x_ref.shape[0] // MIX_OUT_SUBBLOCKS
    for r in range(MIX_OUT_SUBBLOCKS):
        rs = slice(r * sub, (r + 1) * sub)
        mix = _dot(a_ref[rs], wa_ref[...]) + _dot(gm_ref[rs], wb_ref[...])
        x1 = x_ref[rs] + mod_ref[2:3] * mix
        o_ref[rs] = x1
        h2_ref[rs] = (_rms(x1, g2_ref[...]) * (1.0 + mod_ref[4:5]) + mod_ref[3:4]).astype(BF16)


def _mix_out(x, mod, g2, attn, gm, w_out, seq):
    tokens = x.shape[0]
    tiles_per_seq = seq // TM
    row_tile = lambda width: pl.BlockSpec((TM, width), lambda i: (i, 0))
    half = lambda k: pl.BlockSpec((ATTN_WIDTH, D_MODEL), lambda i: (k, 0), pipeline_mode=pl.Buffered(1))
    return pl.pallas_call(
        _mix_out_kernel,
        out_shape=(jax.ShapeDtypeStruct((tokens, D_MODEL), F32), jax.ShapeDtypeStruct((tokens, D_MODEL), BF16)),
        grid=(tokens // TM,),
        in_specs=[
            row_tile(D_MODEL),
            pl.BlockSpec((None, 6, D_MODEL), lambda i: (i // tiles_per_seq, 0, 0)),
            _resident((1, D_MODEL)),
            row_tile(ATTN_WIDTH), row_tile(GM_WIDTH),
            half(0), half(1),
        ],
        out_specs=(row_tile(D_MODEL), row_tile(D_MODEL)),
        compiler_params=_params("arbitrary"),
        name="mix_out",
    )(x, mod, g2, attn, gm, w_out, w_out)


def _ffn_tile(h2_ref, ext_ref, acc_ref, wg_parts, wu_parts, wd, cw, cb, riders=None, finish=None):
    tm = h2_ref.shape[0]
    sub = tm // FFN_SUBBLOCKS
    width = ext_ref.shape[1] // len(wg_parts)
    ups = []
    rider_a = rider_up = None
    for r in range(FFN_SUBBLOCKS):
        h2 = h2_ref[r * sub:(r + 1) * sub]
        ride = riders is not None and r == FFN_SUBBLOCKS - 1
        if ride:
            h2 = jnp.concatenate([h2, riders[0]], axis=0)
        rows = slice(V7X_SUBLANES + r * sub, V7X_SUBLANES + (r + 1) * sub)
        a_parts = [_dot(h2, wg) for wg in wg_parts]
        up = jnp.concatenate([_dot(h2, wu) for wu in wu_parts], axis=1)
        for c, a in enumerate(a_parts):
            ext_ref[rows, c * width:(c + 1) * width] = a[:sub]
        ups.append(up[:sub])
        if ride:
            rider_a, rider_up = jnp.concatenate([a[sub:] for a in a_parts], axis=1), up[sub:]
    for r in range(FFN_SUBBLOCKS):
        lo = V7X_SUBLANES + r * sub
        conv = (cb + cw[0:1] * ext_ref[lo - 2:lo - 2 + sub]
                + cw[1:2] * ext_ref[lo - 1:lo - 1 + sub] + cw[2:3] * ext_ref[lo:lo + sub])
        y = (jax.nn.silu(conv) * ups[r]).astype(BF16)
        if riders is not None and r == FFN_SUBBLOCKS - 1:
            out = _dot(jnp.concatenate([y, riders[1](rider_a, rider_up)], axis=0), wd)
            acc_ref[r * sub:(r + 1) * sub] += out[:sub]
            riders[2][...] += out[sub:]
        else:
            acc_ref[r * sub:(r + 1) * sub] += _dot(y, wd)
        if finish is not None:
            finish(slice(r * sub, (r + 1) * sub))


def _ffn_finish_rows(o_ref, acc_ref, rows, gate, gf_ref, final_norm):
    x2 = o_ref[rows] + gate * acc_ref[rows]
    o_ref[rows] = _rms(x2, gf_ref[...]) if final_norm else x2


def _ffn_kernel(x_hbm, head_hbm, h2_ref, mod_ref, wg_ref, wu_ref, wd_ref, cw_ref, cb_ref, halo_ref, gf_ref,
                o_hbm, tail_ref, obuf_ref, acc_ref, ext_ref, carry_ref, x_sem, wb_sem, head_sem,
                *, first_tile, tiles_per_seq, final_norm):
    i = pl.program_id(0)
    f = pl.program_id(1)
    tm = h2_ref.shape[0]
    tile = i + first_tile
    last = f == pl.num_programs(1) - 1
    rows = pl.ds(pl.multiple_of(tile * tm, tm), tm)
    x_copy = pltpu.make_async_copy(x_hbm.at[rows], obuf_ref, x_sem)
    write_back = pltpu.make_async_copy(obuf_ref, o_hbm.at[rows], wb_sem)
    head_copy = pltpu.make_async_copy(head_hbm, o_hbm.at[pl.ds(0, first_tile * tm)], head_sem)

    @pl.when((i == 0) & (f == 0))
    def _():
        head_copy.start()

    fetch = f == pl.num_programs(1) - 2

    @pl.when(fetch & (i > 0))
    def _():
        write_back.wait()

    @pl.when(fetch)
    def _():
        x_copy.start()

    @pl.when(f == 0)
    def _():
        acc_ref[...] = jnp.zeros_like(acc_ref)

    seq_start = (tile % tiles_per_seq) == 0

    @pl.when(seq_start)
    def _():
        ext_ref[0:V7X_SUBLANES] = jnp.zeros((V7X_SUBLANES, ext_ref.shape[1]), F32)

    @pl.when(jnp.logical_not(seq_start) & (i == 0))
    def _():
        ext_ref[0:V7X_SUBLANES] = halo_ref[...]

    @pl.when(jnp.logical_not(seq_start) & (i > 0))
    def _():
        ext_ref[0:V7X_SUBLANES] = carry_ref[f]

    parts = range(wg_ref.shape[0])
    step = functools.partial(_ffn_tile, h2_ref, ext_ref, acc_ref, [wg_ref[c] for c in parts],
                             [wu_ref[c] for c in parts], wd_ref[...], cw_ref[f], cb_ref[f])

    @pl.when(jnp.logical_not(last))
    def _():
        step()

    @pl.when(last)
    def _():
        x_copy.wait()
        step(finish=lambda rows: _ffn_finish_rows(obuf_ref, acc_ref, rows, mod_ref[5:6], gf_ref, final_norm))
        write_back.start()

    tail = ext_ref[tm:tm + V7X_SUBLANES]
    carry_ref[f] = tail
    tail_ref[f] = tail

    @pl.when(last & (i == pl.num_programs(0) - 1))
    def _():
        write_back.wait()
        head_copy.wait()


def _ffn(x, head_rows, halo, h2, mod, wg, wu, wd, cw, cb, g_final, seq, final_norm):
    tokens = x.shape[0]
    tiles_per_seq = seq // TM_FF
    nf = D_FF // TF
    first_tile = 1
    kern = functools.partial(_ffn_kernel, first_tile=first_tile, tiles_per_seq=tiles_per_seq, final_norm=final_norm)
    return pl.pallas_call(
        kern,
        out_shape=(
            jax.ShapeDtypeStruct((tokens, D_MODEL), F32),
            jax.ShapeDtypeStruct((tokens // TM_FF - first_tile, nf, V7X_SUBLANES, TF), F32),
        ),
        grid=(tokens // TM_FF - first_tile, nf),
        in_specs=[
            pl.BlockSpec(memory_space=pl.ANY),
            pl.BlockSpec(memory_space=pl.ANY),
            pl.BlockSpec((TM_FF, D_MODEL), lambda i, f: (i + first_tile, 0)),
            pl.BlockSpec((None, 6, D_MODEL), lambda i, f: ((i + first_tile) // tiles_per_seq, 0, 0)),
            pl.BlockSpec((TF // TF_HEAD, D_MODEL, TF_HEAD), lambda i, f: (f, 0, 0)),
            pl.BlockSpec((TF // TF_HEAD, D_MODEL, TF_HEAD), lambda i, f: (f, 0, 0)),
            pl.BlockSpec((TF, D_MODEL), lambda i, f: (f, 0)),
            _resident((nf, 3, TF)),
            _resident((nf, 1, TF)),
            pl.BlockSpec((V7X_SUBLANES, TF), lambda i, f: (0, f)),
            _resident((1, D_MODEL)),
        ],
        out_specs=(
            pl.BlockSpec(memory_space=pl.ANY),
            pl.BlockSpec((None, nf, V7X_SUBLANES, TF), lambda i, f: (i, 0, 0, 0)),
        ),
        scratch_shapes=[
            pltpu.VMEM((TM_FF, D_MODEL), F32),
            pltpu.VMEM((TM_FF, D_MODEL), F32),
            pltpu.VMEM((TM_FF + V7X_SUBLANES, TF), F32),
            pltpu.VMEM((nf, V7X_SUBLANES, TF), F32),
            pltpu.SemaphoreType.DMA(()),
            pltpu.SemaphoreType.DMA(()),
            pltpu.SemaphoreType.DMA(()),
        ],
        compiler_params=_params("arbitrary", "arbitrary"),
        name="conv_ffn",
    )(x, head_rows, h2, mod, wg, wu, wd, cw, cb, halo, g_final)


def _ffn_head_kernel(xs_ref, mods_ref, g2_ref, wg_ref, wu_ref, wd_ref, cw_ref, cb_ref, p0_ref, p1_ref, gf_ref,
                     x_hbm, h2_ref, mod_ref,
                     os_ref, a_ref, wgb_ref, wub_ref, wdb_ref, o_ref, halo_ref,
                     h2s_ref, accs_ref, acc_ref, ext_ref, x_sem, *, final_norm):
    f = pl.program_id(0)
    tm = h2_ref.shape[0]
    last = f == pl.num_programs(0) - 1
    x_copy = pltpu.make_async_copy(x_hbm.at[pl.ds(0, tm)], o_ref, x_sem)

    @pl.when(f == pl.num_programs(0) - 2)
    def _():
        x_copy.start()

    @pl.when(f == 0)
    def _():
        h2s_ref[...] = (_rms(xs_ref[...], g2_ref[...]) * (1.0 + mods_ref[4]) + mods_ref[3]).astype(BF16)
        accs_ref[...] = jnp.zeros_like(accs_ref)
        acc_ref[...] = jnp.zeros_like(acc_ref)

    wg, wu, wd = wg_ref[...].astype(BF16), wu_ref[...].astype(BF16), wd_ref[...].astype(BF16)
    wgb_ref[...] = wg
    wub_ref[...] = wu
    wdb_ref[...] = wd
    cw, cb = cw_ref[f], cb_ref[f]

    def sample_gate(a, up):
        a_ref[...] = a
        conv = cb + cw[0:1] * p0_ref[...] + cw[1:2] * p1_ref[...] + cw[2:3] * a
        return (jax.nn.silu(conv) * up).astype(BF16)

    ext_ref[0:V7X_SUBLANES] = jnp.zeros((V7X_SUBLANES, ext_ref.shape[1]), F32)
    step = functools.partial(_ffn_tile, h2_ref, ext_ref, acc_ref, [wg], [wu], wd, cw, cb,
                             riders=(h2s_ref[...], sample_gate, accs_ref))

    @pl.when(jnp.logical_not(last))
    def _():
        step()

    @pl.when(last)
    def _():
        x_copy.wait()
        step(finish=lambda rows: _ffn_finish_rows(o_ref, acc_ref, rows, mod_ref[5:6], gf_ref, final_norm))
        x2 = xs_ref[...] + mods_ref[5] * accs_ref[...]
        os_ref[...] = _rms(x2, gf_ref[...]) if final_norm else x2

    halo_ref[...] = ext_ref[tm:tm + V7X_SUBLANES]


def _ffn_head(layer, xs, mods, g2, w_gate, w_up, w_down, cw, cb, p0, p1, g_final, x, h2, mod, final_norm):
    rows = xs.shape[0]
    tokens = x.shape[0]
    nfh = D_FF // TF_HEAD
    full = lambda shape: pl.BlockSpec(shape, lambda f: (0,) * len(shape))
    cols = lambda r: pl.BlockSpec((r, TF_HEAD), lambda f: (0, f))
    tile_copy = pl.BlockSpec((None, D_MODEL, TF_HEAD), lambda f: (f, 0, 0))
    return pl.pallas_call(
        functools.partial(_ffn_head_kernel, final_norm=final_norm),
        out_shape=(
            jax.ShapeDtypeStruct((rows, D_MODEL), F32), jax.ShapeDtypeStruct((rows, D_FF), F32),
            jax.ShapeDtypeStruct((nfh, D_MODEL, TF_HEAD), BF16), jax.ShapeDtypeStruct((nfh, D_MODEL, TF_HEAD), BF16),
            jax.ShapeDtypeStruct((D_FF, D_MODEL), BF16),
            jax.ShapeDtypeStruct((TM_FF, D_MODEL), F32),
            jax.ShapeDtypeStruct((V7X_SUBLANES, D_FF), F32),
        ),
        grid=(nfh,),
        in_specs=[
            full((rows, D_MODEL)), full((6, rows, D_MODEL)), full((1, D_MODEL)),
            pl.BlockSpec((None, D_MODEL, TF_HEAD), lambda f: (layer, 0, f)),
            pl.BlockSpec((None, D_MODEL, TF_HEAD), lambda f: (layer, 0, f)),
            pl.BlockSpec((None, TF_HEAD, D_MODEL), lambda f: (layer, f, 0)),
            full((nfh, 3, TF_HEAD)), full((nfh, 1, TF_HEAD)),
            cols(rows), cols(rows),
            full((1, D_MODEL)),
            pl.BlockSpec(memory_space=pl.ANY),
            pl.BlockSpec((TM_FF, D_MODEL), lambda f: (0, 0), pipeline_mode=pl.Buffered(1)),
            pl.BlockSpec((None, 6, D_MODEL), lambda f: (0, 0, 0)),
        ],
        out_specs=(
            full((rows, D_MODEL)), cols(rows),
            tile_copy, tile_copy,
            pl.BlockSpec((TF_HEAD, D_MODEL), lambda f: (f, 0)),
            pl.BlockSpec((TM_FF, D_MODEL), lambda f: (0, 0)),
            cols(V7X_SUBLANES),
        ),
        scratch_shapes=[
            pltpu.VMEM((rows, D_MODEL), BF16), pltpu.VMEM((rows, D_MODEL), F32),
            pltpu.VMEM((TM_FF, D_MODEL), F32),
            pltpu.VMEM((TM_FF + V7X_SUBLANES, TF_HEAD), F32),
            pltpu.SemaphoreType.DMA(()),
        ],
        compiler_params=_params("arbitrary"),
        name="conv_ffn_head",
    )(xs, mods, g2, w_gate, w_up, w_down, cw, cb, p0, p1, g_final, x, h2, mod)


def _s_in_proj_kernel(x_ref, mod_ref, g1_ref, w_ref, z_ref, wb_ref, h_ref):
    @pl.when(pl.program_id(0) == 0)
    def _():
        h_ref[...] = (_rms(x_ref[...], g1_ref[...]) * (1.0 + mod_ref[1]) + mod_ref[0]).astype(BF16)

    w = w_ref[...].astype(BF16)
    wb_ref[...] = w
    z_ref[...] = _dot(h_ref[...], w)


def _s_in_proj(layer, x, mod, g1, w_in):
    rows = x.shape[0]
    full = lambda shape: pl.BlockSpec(shape, lambda j: (0,) * len(shape))
    return pl.pallas_call(
        _s_in_proj_kernel,
        out_shape=(jax.ShapeDtypeStruct((rows, IN_COLS), F32), jax.ShapeDtypeStruct((D_MODEL, IN_COLS), BF16)),
        grid=(IN_COLS // TN_IN,),
        in_specs=[
            full((rows, D_MODEL)), full(mod.shape), full((1, D_MODEL)),
            pl.BlockSpec((None, D_MODEL, TN_IN), lambda j: (layer, 0, j)),
        ],
        out_specs=(pl.BlockSpec((rows, TN_IN), lambda j: (0, j)), pl.BlockSpec((D_MODEL, TN_IN), lambda j: (0, j))),
        scratch_shapes=[pltpu.VMEM((rows, D_MODEL), BF16)],
        compiler_params=_params("arbitrary"),
        name="sample_in_proj",
    )(x, mod, g1, w_in)


def _s_attn_kernel(sink_ref, z_ref, cos_ref, sin_ref, gain_ref, ws0_ref, bs0_ref, kt_ref, vt_ref,
                   attn_ref, gm_ref, gvn_ref, ktn_ref, vtn_ref):
    nseq = z_ref.shape[0]
    q_blocks, k_blocks, u, gvn = _post_project(
        z_ref[:, Q0:K0], z_ref[:, K0:V0], z_ref[:, U0:G0], z_ref[:, G0:IN_COLS],
        cos_ref[...], sin_ref[...], gain_ref[...], pair_q=False, q_scale=HEAD_DIM ** -0.5)
    v_new = z_ref[:, V0:U0]
    k_new = jnp.concatenate(k_blocks, axis=1)
    gvn_ref[...] = gvn
    gm_ref[...] = (u * (ws0_ref[...].astype(BF16).astype(F32) * gvn.astype(BF16).astype(F32) + bs0_ref[...])).astype(BF16)

    rows = GQA_GROUP * nseq
    keys = nseq * WINDOW
    seq_of_row = jnp.bitwise_and(_iota((rows, 1), 0), nseq - 1)
    head_of_row = lax.shift_right_logical(_iota((rows, 1), 0), nseq.bit_length() - 1)
    own_keys = lax.shift_right_logical(_iota((rows, keys), 1), WINDOW.bit_length() - 1) == seq_of_row
    own_new = _iota((rows, nseq), 1) == seq_of_row
    pieces = []
    for g in range(N_KV_HEADS):
        gs = slice(g * HEAD_DIM, (g + 1) * HEAD_DIM)
        heads = [g * GQA_GROUP + r for r in range(GQA_GROUP)]
        qg = jnp.concatenate(
            [q_blocks[h // 2][:, (h % 2) * HEAD_DIM:(h % 2 + 1) * HEAD_DIM] for h in heads], axis=0).astype(BF16)
        ktg = jnp.concatenate([kt_ref[b, gs, :] for b in range(nseq)], axis=1).astype(BF16)
        vtg = jnp.concatenate([vt_ref[b, gs, :] for b in range(nseq)], axis=1).astype(BF16)
        s = jnp.where(own_keys, _dot(qg, ktg), NEG)
        sn = jnp.where(own_new, _dot_nt(qg, k_new[:, gs].astype(BF16)), NEG)
        sk = jnp.full((rows, 1), sink_ref[heads[0]], F32)
        for r in range(1, GQA_GROUP):
            sk = jnp.where(head_of_row == r, sink_ref[heads[r]], sk)
        m = jnp.maximum(jnp.maximum(jnp.max(s, axis=-1, keepdims=True), jnp.max(sn, axis=-1, keepdims=True)), sk)
        p = jnp.exp(s - m)
        pn = jnp.exp(sn - m)
        denom = jnp.sum(p, axis=-1, keepdims=True) + jnp.sum(pn, axis=-1, keepdims=True) + jnp.exp(sk - m)
        o = (_dot_nt(p.astype(BF16), vtg) + _dot(pn.astype(BF16), v_new[:, gs].astype(BF16))) / denom
        pieces += [o[r * nseq:(r + 1) * nseq] for r in range(GQA_GROUP)]
    attn_ref[...] = jnp.concatenate(pieces, axis=1).astype(BF16)

    pad = jnp.zeros((V7X_LANES - nseq, KV_WIDTH), F32)
    knt = jnp.concatenate([k_new, pad], axis=0).T
    vnt = jnp.concatenate([v_new, pad], axis=0).T
    newest = _iota((1, WINDOW), 1) == WINDOW - 1
    for b in range(nseq):
        ktn_ref[b] = jnp.where(newest, knt[:, b:b + 1], pltpu.roll(kt_ref[b], WINDOW - 1, axis=1))
        vtn_ref[b] = jnp.where(newest, vnt[:, b:b + 1], pltpu.roll(vt_ref[b], WINDOW - 1, axis=1))


def _s_attention(layer, sinks, z, cos, sin_signed, gain, ws0, bs0, kt_all, vt_all):
    rows = z.shape[0]
    full = lambda shape: pl.BlockSpec(shape, lambda i: (0,) * len(shape))
    cache = pl.BlockSpec((None, rows, KV_WIDTH, WINDOW), lambda i: (layer, 0, 0, 0))
    return pl.pallas_call(
        _s_attn_kernel,
        out_shape=(
            jax.ShapeDtypeStruct((rows, ATTN_WIDTH), BF16),
            jax.ShapeDtypeStruct((rows, GM_WIDTH), BF16),
            jax.ShapeDtypeStruct((rows, GM_WIDTH), F32),
            jax.ShapeDtypeStruct((rows, KV_WIDTH, WINDOW), F32),
            jax.ShapeDtypeStruct((rows, KV_WIDTH, WINDOW), F32),
        ),
        grid=(1,),
        in_specs=[
            pl.BlockSpec(memory_space=pltpu.SMEM),
            full((rows, IN_COLS)), full((1, V7X_LANES)), full((1, V7X_LANES)),
            full((1, GM_WIDTH)), full((1, GM_WIDTH)), full((1, GM_WIDTH)),
            cache, cache,
        ],
        out_specs=(full((rows, ATTN_WIDTH)), full((rows, GM_WIDTH)), full((rows, GM_WIDTH)),
                   full((rows, KV_WIDTH, WINDOW)), full((rows, KV_WIDTH, WINDOW))),
        compiler_params=_params("arbitrary"),
        name="sample_attention",
    )(sinks, z, cos, sin_signed, gain, ws0, bs0, kt_all, vt_all)


def _paired_head(slot):
    p, rest = divmod(slot, 2 * GQA_GROUP)
    j, i = divmod(rest, 2)
    return p * 2 * GQA_GROUP + i * GQA_GROUP + j


def _s_mix_out_kernel(x_ref, gate_ref, a_ref, gm_ref, w_ref, o_ref, wb_ref):
    w = w_ref[...].astype(BF16)
    for slot in range(N_HEADS):
        head = _paired_head(slot)
        wb_ref[slot * HEAD_DIM:(slot + 1) * HEAD_DIM] = w[head * HEAD_DIM:(head + 1) * HEAD_DIM]
    wb_ref[ATTN_WIDTH:] = w[ATTN_WIDTH:]
    mix = _dot(a_ref[...], w[:ATTN_WIDTH]) + _dot(gm_ref[...], w[ATTN_WIDTH:])
    o_ref[...] = x_ref[...] + gate_ref[...] * mix


def _s_mix_out(layer, x, gate, attn, gm, w_out):
    rows = x.shape[0]
    full = lambda shape: pl.BlockSpec(shape, lambda j: (0,) * len(shape))
    col_tile = lambda r: pl.BlockSpec((r, TN_OUT), lambda j: (0, j))
    return pl.pallas_call(
        _s_mix_out_kernel,
        out_shape=(jax.ShapeDtypeStruct((rows, D_MODEL), F32), jax.ShapeDtypeStruct((D_MODEL, D_MODEL), BF16)),
        grid=(D_MODEL // TN_OUT,),
        in_specs=[
            col_tile(rows), col_tile(rows), full((rows, ATTN_WIDTH)), full((rows, GM_WIDTH)),
            pl.BlockSpec((None, D_MODEL, TN_OUT), lambda j: (layer, 0, j)),
        ],
        out_specs=(col_tile(rows), col_tile(D_MODEL)),
        compiler_params=_params("arbitrary"),
        name="sample_mix_out",
    )(x, gate, attn, gm, w_out)


def _rope_tables(pos):
    half = HEAD_DIM // 2
    inv = ROPE_THETA ** (-jnp.arange(half, dtype=jnp.float32) / half)
    ang = pos.astype(jnp.float32)[:, None] * inv[None, :]
    reps = V7X_LANES // half
    sign = jnp.where((jnp.arange(V7X_LANES) % HEAD_DIM) < half, -1.0, 1.0).astype(F32)
    return jnp.tile(jnp.cos(ang), (1, reps)), jnp.tile(jnp.sin(ang), (1, reps)) * sign[None, :]


def _to_planes(window_buf):
    lead = window_buf.shape[:-3]
    return jnp.moveaxis(window_buf, -3, -1).reshape(lead + (KV_WIDTH, WINDOW))


def _from_planes(planes):
    lead = planes.shape[:-2]
    return jnp.moveaxis(planes.reshape(lead + (N_KV_HEADS, HEAD_DIM, WINDOW)), -1, -3)


def kernel(x_prompt, x_sample, cache_k, cache_v, state_conv, c_prompt, c_sample, w_ada, b_ada, g_norm1, g_norm2,
           w_in, gm_gain, gm_ws, gm_bs, sinks, w_out, w_gate, w_up, conv_w, conv_b, w_down, g_final):
    batch, seq, _ = x_prompt.shape
    dec = x_sample.shape[0]
    nf = D_FF // TF
    assert x_sample.shape[1] == 1 and seq % TM == 0 and seq % TQ == 0 and TM % CHUNK == 0 and D_FF % TF == 0
    assert seq // TM_FF >= 2 and seq % TM_FF == 0 and TM_FF % FFN_SUBBLOCKS == 0 and TF % TF_HEAD == 0
    assert dec & (dec - 1) == 0 and dec <= V7X_LANES

    bst = jnp.swapaxes(gm_bs, 1, 2)
    ws0 = jnp.repeat(gm_ws[:, :, 0, 0], GM_WIDTH // GM_HEADS, axis=1)[:, None, :]
    bs0 = jnp.repeat(gm_bs[:, :, 0], GM_WIDTH // GM_HEADS, axis=1)[:, None, :]
    cw_tiles = jnp.swapaxes(conv_w.reshape(DEPTH, 3, nf, TF), 1, 2)
    cb_tiles = conv_b.reshape(DEPTH, nf, 1, TF)
    cw_head = jnp.swapaxes(conv_w.reshape(DEPTH, 3, D_FF // TF_HEAD, TF_HEAD), 1, 2)
    cb_head = conv_b.reshape(DEPTH, D_FF // TF_HEAD, 1, TF_HEAD)

    cos_p, sin_p = _rope_tables(jnp.arange(seq, dtype=jnp.int32))
    cos_s, sin_s = _rope_tables(PAST_LEN + jnp.arange(1, dtype=jnp.int32))

    c_all = jnp.concatenate([c_prompt, c_sample, jnp.zeros((MOD_ROWS - batch - dec, D_MODEL), F32)], axis=0)
    b_ada3 = b_ada.reshape(DEPTH, 1, 6 * D_MODEL)
    prompt_mod = lambda slab: slab[:batch].reshape(batch, -1, D_MODEL)
    sample_mod = lambda slab: jnp.swapaxes(slab[batch:batch + dec].reshape(dec, -1, D_MODEL), 0, 1)
    mod_first = _modulation_head(c_all, w_ada, b_ada3, 2 * D_MODEL)
    mod_next = None

    kt_all, vt_all = _to_planes(cache_k), _to_planes(cache_v)

    hp = x_prompt.reshape(batch * seq, D_MODEL)
    hs = x_sample.reshape(dec, D_MODEL)
    g_fin = g_final.reshape(1, D_MODEL)
    outs = [[] for _ in range(8)]
    for l in range(DEPTH):
        last = l == DEPTH - 1
        g1 = g_norm1[l].reshape(1, D_MODEL)
        g2 = g_norm2[l].reshape(1, D_MODEL)
        gain = gm_gain[l].reshape(1, GM_WIDTH)
        if l == 0:
            mod_p, mod_s = prompt_mod(mod_first), sample_mod(mod_first)
            in_rider = (c_all, w_ada, b_ada3, 0, 2 * D_MODEL, 4 * D_MODEL)
        else:
            mod_p, mod_s = prompt_mod(mod_next), sample_mod(mod_next)
            in_rider = None
        attn_rider = (c_all, w_ada, b_ada3, l + 1, 0, 6 * D_MODEL) if l + 1 < DEPTH else None

        z, w_in_b = _s_in_proj(l, hs, mod_s, g1, w_in)
        attn_s, gm_s, gvn_s, kt_new, vt_new = _s_attention(
            l, sinks[l], z, cos_s, sin_s, gain, ws0[l], bs0[l], kt_all, vt_all)
        mixed = _mix_in(hp, mod_p, g1, w_in_b, cos_p, sin_p, gain, gm_ws[l], bst[l], seq, mod_rider=in_rider)
        q, k, v, gm, gvl = mixed[:5]
        if in_rider is not None:
            slab = jnp.concatenate([mod_first, mixed[5]], axis=1)
            mod_p, mod_s = prompt_mod(slab), sample_mod(slab)
        outs[4].append(kt_new)
        outs[5].append(vt_new)
        outs[6].append(gvn_s[:, None, :])

        hs, w_out_b = _s_mix_out(l, hs, mod_s[2], attn_s, gm_s, w_out)
        attended = _attention(sinks[l], q, k, v, seq, mod_rider=attn_rider)
        attn, kt_p, vt_p = attended[:3]
        if attn_rider is not None:
            mod_next = attended[3]
        hp, h2 = _mix_out(hp, mod_p, g2, attn, gm, w_out_b, seq)
        outs[0].append(kt_p)
        outs[1].append(vt_p)
        outs[2].append(gvl)

        hs, a_new, wg_b, wu_b, wd_b, head_rows, halo = _ffn_head(
            l, hs, mod_s, g2, w_gate, w_up, w_down, cw_head[l], cb_head[l], state_conv[l, :, 0], state_conv[l, :, 1],
            g_fin, hp, h2, mod_p, last)
        hp, tail = _ffn(hp, head_rows, halo, h2, mod_p, wg_b, wu_b, wd_b, cw_tiles[l], cb_tiles[l], g_fin, seq, last)
        outs[7].append(jnp.stack([state_conv[l, :, 1], a_new], axis=1))
        tails = tail[seq // TM_FF - 2::seq // TM_FF, :, V7X_SUBLANES - 2:, :]
        outs[3].append(jnp.swapaxes(tails, 1, 2).reshape(batch, 2, D_FF))

    st = [jnp.stack(o) for o in outs]
    return (hp.reshape(batch, seq, D_MODEL), hs.reshape(dec, 1, D_MODEL),
            _from_planes(st[0]), _from_planes(st[1]), st[2], st[3],
            _from_planes(st[4]), _from_planes(st[5]), st[6], st[7])
```

```python
import functools

import jax
import jax.numpy as jnp
from jax import lax
from jax.experimental import pallas as pl
from jax.experimental.pallas import tpu as pltpu

D_MODEL = 2048
DEPTH = 2
PAST_LEN = 16384
ATTN_WIDTH = 1024
GM_WIDTH = 1024
HEAD_DIM = 64
N_HEADS = 16
N_KV_HEADS = 4
GQA_GROUP = N_HEADS // N_KV_HEADS
WINDOW = 128
ROPE_THETA = 10000.0
CHUNK = 128
GM_HEADS = 8
D_FF = 5632
KV_WIDTH = N_KV_HEADS * HEAD_DIM
IN_COLS = ATTN_WIDTH + 2 * KV_WIDTH + 2 * GM_WIDTH
EPS = 1e-6
NEG = -1e30
LOG2_E = 1.4426950408889634

Q0, K0, V0, U0, G0 = 0, ATTN_WIDTH, ATTN_WIDTH + KV_WIDTH, ATTN_WIDTH + 2 * KV_WIDTH, ATTN_WIDTH + 2 * KV_WIDTH + GM_WIDTH

V7X_LANES = 128
V7X_SUBLANES = 8
V7X_VMEM_LIMIT_BYTES = 56 * 1024 * 1024

TM = 512
TF = 512
TM_FF = 1024
TF_HEAD = 256
TQ = 1024
TN_IN = 896
TN_OUT = 1024
MOD_TN = 1024
MOD_ROWS = 48
FFN_SUBBLOCKS = 4
MIX_OUT_SUBBLOCKS = 2

BF16 = jnp.bfloat16
F32 = jnp.float32


def _params(*semantics):
    return pltpu.CompilerParams(dimension_semantics=semantics, vmem_limit_bytes=V7X_VMEM_LIMIT_BYTES)


def _dot(a, b):
    return jnp.dot(a, b, preferred_element_type=F32)


def _dot_nt(a, b):
    return lax.dot_general(a, b, (((1,), (1,)), ((), ())), preferred_element_type=F32)


def _rms(x, g):
    return x * lax.rsqrt(jnp.mean(x * x, axis=-1, keepdims=True) + EPS) * g


def _resident(shape):
    zeros = (0,) * len(shape)
    return pl.BlockSpec(shape, lambda *_: zeros, pipeline_mode=pl.Buffered(1))


def _iota(shape, dim):
    return lax.broadcasted_iota(jnp.int32, shape, dim)


def _mod_columns(c_ref, w_ref, b_ref, o_ref):
    s = jax.nn.silu(c_ref[...]).astype(BF16)
    o_ref[...] = _dot(s, w_ref[...].astype(BF16)) + b_ref[...]


def _mod_rider_specs(layer, first_col, width, step_of):
    assert first_col % width == 0
    first = first_col // width
    in_specs = [
        pl.BlockSpec((MOD_ROWS, D_MODEL), lambda *g: (0, 0)),
        pl.BlockSpec((None, D_MODEL, width), lambda *g: (layer, 0, first + step_of(*g))),
        pl.BlockSpec((None, 1, width), lambda *g: (layer, 0, first + step_of(*g))),
    ]
    return in_specs, pl.BlockSpec((MOD_ROWS, width), lambda *g: (0, step_of(*g)))


def _modulation_head(c_all, w_ada, b_ada3, n_cols):
    in_specs, out_spec = _mod_rider_specs(0, 0, MOD_TN, lambda j: j)
    return pl.pallas_call(
        _mod_columns,
        out_shape=jax.ShapeDtypeStruct((MOD_ROWS, n_cols), F32),
        grid=(n_cols // MOD_TN,),
        in_specs=in_specs,
        out_specs=out_spec,
        compiler_params=_params("arbitrary"),
        name="adaln_modulation",
    )(c_all, w_ada, b_ada3)


def _rope_cols(z, cos, sin_signed):
    lane = _iota((1, V7X_LANES), 1)
    first_half = jnp.bitwise_and(lane, HEAD_DIM - 1) < HEAD_DIM // 2
    outs = []
    for c in range(z.shape[1] // V7X_LANES):
        zc = z[:, c * V7X_LANES:(c + 1) * V7X_LANES]
        partner = jnp.where(first_half,
                            pltpu.roll(zc, V7X_LANES - HEAD_DIM // 2, axis=1),
                            pltpu.roll(zc, HEAD_DIM // 2, axis=1))
        outs.append(zc * cos + partner * sin_signed)
    return outs


def _swap_halves(x):
    return pltpu.roll(x, HEAD_DIM, axis=1)


def _pair_q_blocks(nat):
    low = _iota((1, V7X_LANES), 1) < HEAD_DIM
    out = []
    for c in range(len(nat)):
        p, j = divmod(c, 4)
        a, b = nat[p * 4 + j // 2], nat[p * 4 + 2 + j // 2]
        out.append(jnp.where(low, a, _swap_halves(b)) if j % 2 == 0 else jnp.where(low, _swap_halves(a), b))
    return out


def _post_project(zq, zk, zu, zg, cos, sin_signed, gain, pair_q, q_scale):
    q_blocks = _rope_cols(zq, cos, sin_signed)
    if pair_q:
        q_blocks = _pair_q_blocks(q_blocks)
    q_blocks = [qb * q_scale for qb in q_blocks]
    k_blocks = _rope_cols(zk, cos, sin_signed)
    u = jax.nn.gelu(zu)
    gvn = _rms(jax.nn.gelu(zg), gain)
    return q_blocks, k_blocks, u, gvn


def _store_cols(ref, blocks):
    for c, blk in enumerate(blocks):
        ref[:, c * V7X_LANES:(c + 1) * V7X_LANES] = blk.astype(ref.dtype)


def _mix_in_kernel(x_ref, mod_ref, g1_ref, w_ref, cos_ref, sin_ref, gain_ref, ws_ref, bst_ref, *rest, mod_rider):
    if mod_rider:
        c_ref, wa_ref, ba_ref, q_ref, k_ref, v_ref, gm_ref, gvl_ref, mo_ref = rest
        _mod_columns(c_ref, wa_ref, ba_ref, mo_ref)
    else:
        q_ref, k_ref, v_ref, gm_ref, gvl_ref = rest
    tm = x_ref.shape[0]
    h = (_rms(x_ref[...], g1_ref[...]) * (1.0 + mod_ref[1:2]) + mod_ref[0:1]).astype(BF16)
    q_blocks, k_blocks, u, gvn = _post_project(
        _dot(h, w_ref[:, Q0:K0]), _dot(h, w_ref[:, K0:V0]), _dot(h, w_ref[:, U0:G0]), _dot(h, w_ref[:, G0:IN_COLS]),
        cos_ref[...], sin_ref[...], gain_ref[...], pair_q=True, q_scale=HEAD_DIM ** -0.5 * LOG2_E)
    _store_cols(q_ref, q_blocks)
    _store_cols(k_ref, k_blocks)
    v_ref[...] = _dot(h, w_ref[:, V0:U0])
    gvl_ref[...] = gvn[tm - CHUNK:, :]
    gvb = gvn.astype(BF16)
    causal = _iota((CHUNK, CHUNK), 1) <= _iota((CHUNK, CHUNK), 0)
    for hh in range(GM_HEADS):
        wm = jnp.where(causal, ws_ref[hh], 0.0).astype(BF16)
        bias = bst_ref[:, hh:hh + 1]
        cs = slice(hh * CHUNK, (hh + 1) * CHUNK)
        for c in range(tm // CHUNK):
            rs = slice(c * CHUNK, (c + 1) * CHUNK)
            s = _dot(wm, gvb[rs, cs]) + bias
            gm_ref[rs, cs] = (u[rs, cs] * s).astype(BF16)


def _mix_in(x, mod, g1, w_in, cos, sin_signed, gain, ws, bst, seq, mod_rider=None):
    tokens = x.shape[0]
    tiles_per_seq = seq // TM
    batch = tokens // seq
    steps = tokens // TM
    row_tile = lambda width: pl.BlockSpec((TM, width), lambda i: (i, 0))
    rider_in, rider_out, rider_shape, rider_args = [], (), (), ()
    if mod_rider is not None:
        c_all, w_ada, b_ada3, layer, first_col, n_cols = mod_rider
        rider_in, out_spec = _mod_rider_specs(layer, first_col, n_cols // steps, lambda i: i)
        rider_out, rider_shape = (out_spec,), (jax.ShapeDtypeStruct((MOD_ROWS, n_cols), F32),)
        rider_args = (c_all, w_ada, b_ada3)
    return pl.pallas_call(
        functools.partial(_mix_in_kernel, mod_rider=mod_rider is not None),
        out_shape=(
            jax.ShapeDtypeStruct((tokens, ATTN_WIDTH), BF16),
            jax.ShapeDtypeStruct((tokens, KV_WIDTH), F32),
            jax.ShapeDtypeStruct((tokens, KV_WIDTH), F32),
            jax.ShapeDtypeStruct((tokens, GM_WIDTH), BF16),
            jax.ShapeDtypeStruct((batch, CHUNK, GM_WIDTH), F32),
        ) + rider_shape,
        grid=(steps,),
        in_specs=[
            row_tile(D_MODEL),
            pl.BlockSpec((None, mod.shape[1], D_MODEL), lambda i: (i // tiles_per_seq, 0, 0)),
            _resident((1, D_MODEL)),
            _resident((D_MODEL, IN_COLS)),
            pl.BlockSpec((TM, V7X_LANES), lambda i: (i % tiles_per_seq, 0)),
            pl.BlockSpec((TM, V7X_LANES), lambda i: (i % tiles_per_seq, 0)),
            _resident((1, GM_WIDTH)),
            _resident((GM_HEADS, CHUNK, CHUNK)),
            _resident((CHUNK, GM_HEADS)),
        ] + rider_in,
        out_specs=(
            row_tile(ATTN_WIDTH), row_tile(KV_WIDTH), row_tile(KV_WIDTH), row_tile(GM_WIDTH),
            pl.BlockSpec((None, CHUNK, GM_WIDTH), lambda i: (i // tiles_per_seq, 0, 0)),
        ) + rider_out,
        compiler_params=_params("arbitrary"),
        name="mix_in",
    )(x, mod, g1, w_in, cos, sin_signed, gain, ws, bst, *rider_args)


def _attn_block(n, sink_ref, q_blk, k_prev, k_cur, v_prev, v_cur, out_blk):
    row = _iota((WINDOW, 2 * WINDOW), 0)
    key = _iota((WINDOW, 2 * WINDOW), 1) - WINDOW
    valid = (key <= row) & (key >= jnp.maximum(row - WINDOW, -n * WINDOW))
    low = _iota((WINDOW, V7X_LANES), 1) < HEAD_DIM
    cols_per_pair = ATTN_WIDTH // V7X_LANES // 2
    for p in range(2):
        ls = slice(p * V7X_LANES, (p + 1) * V7X_LANES)
        kcat = jnp.concatenate([k_prev[:, ls], k_cur[:, ls]], axis=0).astype(BF16)
        vcat = jnp.concatenate([v_prev[:, ls], v_cur[:, ls]], axis=0).astype(BF16)
        blocks = []
        for cc in range(cols_per_pair):
            c = p * cols_per_pair + cc
            qcol = q_blk[:, c * V7X_LANES:(c + 1) * V7X_LANES]
            zero = jnp.zeros_like(qcol)
            blocks += [jnp.where(low, qcol, zero), jnp.where(low, zero, qcol)]
        s = _dot_nt(jnp.concatenate(blocks, axis=0), kcat)
        probs, inv = [], []
        for j in range(2 * cols_per_pair):
            cc, e = divmod(j, 2)
            sj = jnp.where(valid, s[j * WINDOW:(j + 1) * WINDOW], NEG)
            sk = sink_ref[p * 2 * GQA_GROUP + e * GQA_GROUP + cc] * LOG2_E
            m = jnp.maximum(jnp.max(sj, axis=-1, keepdims=True), sk)
            pj = jnp.exp2(sj - m)
            inv.append(1.0 / (jnp.sum(pj, axis=-1, keepdims=True) + jnp.exp2(sk - m)))
            probs.append(pj.astype(BF16))
        o = _dot(jnp.concatenate(probs, axis=0), vcat)
        for cc in range(cols_per_pair):
            c = p * cols_per_pair + cc
            even = o[(2 * cc) * WINDOW:(2 * cc + 1) * WINDOW] * inv[2 * cc]
            odd = o[(2 * cc + 1) * WINDOW:(2 * cc + 2) * WINDOW] * inv[2 * cc + 1]
            out_blk[:, c * V7X_LANES:(c + 1) * V7X_LANES] = jnp.where(low, even, odd).astype(BF16)


def _attn_kernel(sink_ref, q_ref, kp_ref, kc_ref, vp_ref, vc_ref, *rest, mod_rider):
    if mod_rider:
        c_ref, wa_ref, ba_ref, o_ref, kt_ref, vt_ref, mo_ref = rest
        _mod_columns(c_ref, wa_ref, ba_ref, mo_ref)
    else:
        o_ref, kt_ref, vt_ref = rest
    step = pl.program_id(1)
    blocks = q_ref.shape[0] // WINDOW
    for t in range(blocks):
        rows = slice(t * WINDOW, (t + 1) * WINDOW)
        prev = slice((t - 1) * WINDOW, t * WINDOW)
        k_prev = kp_ref[...] if t == 0 else kc_ref[prev]
        v_prev = vp_ref[...] if t == 0 else vc_ref[prev]
        _attn_block(step * blocks + t, sink_ref, q_ref.at[rows], k_prev, kc_ref[rows], v_prev, vc_ref[rows], o_ref.at[rows])

    @pl.when(step == pl.num_programs(1) - 1)
    def _():
        last = slice((blocks - 1) * WINDOW, blocks * WINDOW)
        kt_ref[...] = kc_ref[last].T
        vt_ref[...] = vc_ref[last].T


def _attention(sinks, q, k, v, seq, mod_rider=None):
    tokens = q.shape[0]
    batch = tokens // seq
    steps = seq // TQ
    per_step = TQ // WINDOW
    cur = lambda b, n: (b * steps + n, 0)
    prev = lambda b, n: (jnp.maximum((b * steps + n) * per_step - 1, 0), 0)
    state = pl.BlockSpec((None, KV_WIDTH, WINDOW), lambda b, n: (b, 0, 0))
    rider_in, rider_out, rider_shape, rider_args = [], (), (), ()
    if mod_rider is not None:
        c_all, w_ada, b_ada3, layer, first_col, n_cols = mod_rider
        rider_in, out_spec = _mod_rider_specs(layer, first_col, n_cols // (batch * steps), lambda b, n: b * steps + n)
        rider_out, rider_shape = (out_spec,), (jax.ShapeDtypeStruct((MOD_ROWS, n_cols), F32),)
        rider_args = (c_all, w_ada, b_ada3)
    return pl.pallas_call(
        functools.partial(_attn_kernel, mod_rider=mod_rider is not None),
        out_shape=(
            jax.ShapeDtypeStruct((tokens, ATTN_WIDTH), BF16),
            jax.ShapeDtypeStruct((batch, KV_WIDTH, WINDOW), F32),
            jax.ShapeDtypeStruct((batch, KV_WIDTH, WINDOW), F32),
        ) + rider_shape,
        grid=(batch, steps),
        in_specs=[
            pl.BlockSpec(memory_space=pltpu.SMEM),
            pl.BlockSpec((TQ, ATTN_WIDTH), cur),
            pl.BlockSpec((WINDOW, KV_WIDTH), prev),
            pl.BlockSpec((TQ, KV_WIDTH), cur),
            pl.BlockSpec((WINDOW, KV_WIDTH), prev),
            pl.BlockSpec((TQ, KV_WIDTH), cur),
        ] + rider_in,
        out_specs=(pl.BlockSpec((TQ, ATTN_WIDTH), cur), state, state) + rider_out,
        compiler_params=_params("arbitrary", "arbitrary"),
        name="swa_attention",
    )(sinks, q, k, k, v, v, *rider_args)


def _mix_out_kernel(x_ref, mod_ref, g2_ref, a_ref, gm_ref, wa_ref, wb_ref, o_ref, h2_ref):
    sub = x_ref.shape[0] // MIX_OUT_SUBBLOCKS
    for r in range(MIX_OUT_SUBBLOCKS):
        rs = slice(r * sub, (r + 1) * sub)
        mix = _dot(a_ref[rs], wa_ref[...]) + _dot(gm_ref[rs], wb_ref[...])
        x1 = x_ref[rs] + mod_ref[2:3] * mix
        o_ref[rs] = x1
        h2_ref[rs] = (_rms(x1, g2_ref[...]) * (1.0 + mod_ref[4:5]) + mod_ref[3:4]).astype(BF16)


def _mix_out(x, mod, g2, attn, gm, w_out, seq):
    tokens = x.shape[0]
    tiles_per_seq = seq // TM
    row_tile = lambda width: pl.BlockSpec((TM, width), lambda i: (i, 0))
    half = lambda k: pl.BlockSpec((ATTN_WIDTH, D_MODEL), lambda i: (k, 0), pipeline_mode=pl.Buffered(1))
    return pl.pallas_call(
        _mix_out_kernel,
        out_shape=(jax.ShapeDtypeStruct((tokens, D_MODEL), F32), jax.ShapeDtypeStruct((tokens, D_MODEL), BF16)),
        grid=(tokens // TM,),
        in_specs=[
            row_tile(D_MODEL),
            pl.BlockSpec((None, 6, D_MODEL), lambda i: (i // tiles_per_seq, 0, 0)),
            _resident((1, D_MODEL)),
            row_tile(ATTN_WIDTH), row_tile(GM_WIDTH),
            half(0), half(1),
        ],
        out_specs=(row_tile(D_MODEL), row_tile(D_MODEL)),
        compiler_params=_params("arbitrary"),
        name="mix_out",
    )(x, mod, g2, attn, gm, w_out, w_out)


def _ffn_tile(h2_ref, ext_ref, acc_ref, wg_parts, wu_parts, wd, cw, cb, riders=None, restart=None):
    tm = h2_ref.shape[0]
    sub = tm // FFN_SUBBLOCKS
    width = ext_ref.shape[1] // len(wg_parts)
    ups = []
    rider_a = rider_up = None
    for r in range(FFN_SUBBLOCKS):
        h2 = h2_ref[r * sub:(r + 1) * sub]
        ride = riders is not None and r == FFN_SUBBLOCKS - 1
        if ride:
            h2 = jnp.concatenate([h2, riders[0]], axis=0)
        rows = slice(V7X_SUBLANES + r * sub, V7X_SUBLANES + (r + 1) * sub)
        a_parts = [_dot(h2, wg) for wg in wg_parts]
        up = jnp.concatenate([_dot(h2, wu) for wu in wu_parts], axis=1)
        for c, a in enumerate(a_parts):
            ext_ref[rows, c * width:(c + 1) * width] = a[:sub]
        ups.append(up[:sub])
        if ride:
            rider_a, rider_up = jnp.concatenate([a[sub:] for a in a_parts], axis=1), up[sub:]
    def carried(rows):
        old = acc_ref[rows]
        return old if restart is None else jnp.where(restart, 0.0, old)

    for r in range(FFN_SUBBLOCKS):
        lo = V7X_SUBLANES + r * sub
        conv = (cb + cw[0:1] * ext_ref[lo - 2:lo - 2 + sub]
                + cw[1:2] * ext_ref[lo - 1:lo - 1 + sub] + cw[2:3] * ext_ref[lo:lo + sub])
        y = (jax.nn.silu(conv) * ups[r]).astype(BF16)
        if riders is not None and r == FFN_SUBBLOCKS - 1:
            out = _dot(jnp.concatenate([y, riders[1](rider_a, rider_up)], axis=0), wd)
            acc_ref[r * sub:(r + 1) * sub] = carried(slice(r * sub, (r + 1) * sub)) + out[:sub]
            riders[2][...] += out[sub:]
        else:
            acc_ref[r * sub:(r + 1) * sub] = carried(slice(r * sub, (r + 1) * sub)) + _dot(y, wd)


def _ffn_finish(o_ref, acc_ref, gate, gf_ref, final_norm):
    sub = o_ref.shape[0] // FFN_SUBBLOCKS

    def finish(r, carry):
        rs = pl.ds(pl.multiple_of(r * sub, sub), sub)
        x2 = o_ref[rs, :] + gate * acc_ref[rs, :]
        o_ref[rs, :] = _rms(x2, gf_ref[...]) if final_norm else x2
        return carry

    lax.fori_loop(0, FFN_SUBBLOCKS, finish, 0)


def _ffn_kernel(x_hbm, head_hbm, wg_hbm, wu_hbm, wd_hbm, h2_ref, mod_ref, cw_ref, cb_ref, halo_ref, gf_ref,
                o_hbm, tail_ref, wg_buf, wu_buf, wd_buf, obuf_ref, acc_ref, ext_ref, carry_ref,
                w_sems, x_sem, wb_sem, head_sem, *, first_tile, tiles_per_seq, final_norm):
    i = pl.program_id(0)
    n_tiles = pl.num_programs(0)
    nf = wd_hbm.shape[0] // wd_buf.shape[1]
    per = wg_buf.shape[1]
    tm = h2_ref.shape[0]
    tile = i + first_tile
    rows = pl.ds(pl.multiple_of(tile * tm, tm), tm)
    x_copy = pltpu.make_async_copy(x_hbm.at[rows], obuf_ref, x_sem)
    write_back = pltpu.make_async_copy(obuf_ref, o_hbm.at[rows], wb_sem)
    head_copy = pltpu.make_async_copy(head_hbm, o_hbm.at[pl.ds(0, first_tile * tm)], head_sem)

    def weight_copies(f, slot):
        return (pltpu.make_async_copy(wg_hbm.at[pl.ds(f * per, per)], wg_buf.at[slot], w_sems.at[0, slot]),
                pltpu.make_async_copy(wu_hbm.at[pl.ds(f * per, per)], wu_buf.at[slot], w_sems.at[1, slot]),
                pltpu.make_async_copy(wd_hbm.at[pl.ds(pl.multiple_of(f * wd_buf.shape[1], wd_buf.shape[1]), wd_buf.shape[1])],
                                      wd_buf.at[slot], w_sems.at[2, slot]))

    @pl.when(i == 0)
    def _():
        head_copy.start()
        for c in weight_copies(0, 0):
            c.start()
        acc_ref[...] = jnp.zeros_like(acc_ref)

    seq_start = (tile % tiles_per_seq) == 0

    def column_step(f, carry):
        step = i * nf + f
        slot = step % 2
        last = f == nf - 1
        for c in weight_copies(f, slot):
            c.wait()

        @pl.when(jnp.logical_not(last & (i == n_tiles - 1)))
        def _():
            nxt = jnp.where(last, 0, f + 1)
            for c in weight_copies(nxt, 1 - slot):
                c.start()

        @pl.when(last & (i > 0))
        def _():
            write_back.wait()

        @pl.when(last)
        def _():
            x_copy.start()

        @pl.when(seq_start)
        def _():
            ext_ref[0:V7X_SUBLANES] = jnp.zeros((V7X_SUBLANES, ext_ref.shape[1]), F32)

        @pl.when(jnp.logical_not(seq_start) & (i == 0))
        def _():
            ext_ref[0:V7X_SUBLANES] = halo_ref[f]

        @pl.when(jnp.logical_not(seq_start) & (i > 0))
        def _():
            ext_ref[0:V7X_SUBLANES] = carry_ref[f]

        _ffn_tile(h2_ref, ext_ref, acc_ref, [wg_buf[slot, c] for c in range(per)], [wu_buf[slot, c] for c in range(per)],
                  wd_buf[slot], cw_ref[f], cb_ref[f], restart=f == 0)
        tail = ext_ref[tm:tm + V7X_SUBLANES]
        carry_ref[f] = tail
        tail_ref[f] = tail
        return carry

    lax.fori_loop(0, nf, column_step, 0)
    x_copy.wait()
    _ffn_finish(obuf_ref, acc_ref, mod_ref[5:6], gf_ref, final_norm)
    write_back.start()

    @pl.when(i == n_tiles - 1)
    def _():
        write_back.wait()
        head_copy.wait()


def _ffn(x, head_rows, halo, h2, mod, wg, wu, wd, cw, cb, g_final, seq, final_norm):
    tokens = x.shape[0]
    tiles_per_seq = seq // TM_FF
    nf = D_FF // TF
    per = TF // TF_HEAD
    first_tile = 1
    kern = functools.partial(_ffn_kernel, first_tile=first_tile, tiles_per_seq=tiles_per_seq, final_norm=final_norm)
    hbm = pl.BlockSpec(memory_space=pl.ANY)
    return pl.pallas_call(
        kern,
        out_shape=(
            jax.ShapeDtypeStruct((tokens, D_MODEL), F32),
            jax.ShapeDtypeStruct((tokens // TM_FF - first_tile, nf, V7X_SUBLANES, TF), F32),
        ),
        grid=(tokens // TM_FF - first_tile,),
        in_specs=[
            hbm, hbm, hbm, hbm, hbm,
            pl.BlockSpec((TM_FF, D_MODEL), lambda i: (i + first_tile, 0)),
            pl.BlockSpec((None, 6, D_MODEL), lambda i: ((i + first_tile) // tiles_per_seq, 0, 0)),
            _resident((nf, 3, TF)),
            _resident((nf, 1, TF)),
            _resident((nf, V7X_SUBLANES, TF)),
            _resident((1, D_MODEL)),
        ],
        out_specs=(
            hbm,
            pl.BlockSpec((None, nf, V7X_SUBLANES, TF), lambda i: (i, 0, 0, 0)),
        ),
        scratch_shapes=[
            pltpu.VMEM((2, per, D_MODEL, TF_HEAD), BF16),
            pltpu.VMEM((2, per, D_MODEL, TF_HEAD), BF16),
            pltpu.VMEM((2, TF, D_MODEL), BF16),
            pltpu.VMEM((TM_FF, D_MODEL), F32),
            pltpu.VMEM((TM_FF, D_MODEL), F32),
            pltpu.VMEM((TM_FF + V7X_SUBLANES, TF), F32),
            pltpu.VMEM((nf, V7X_SUBLANES, TF), F32),
            pltpu.SemaphoreType.DMA((3, 2)),
            pltpu.SemaphoreType.DMA(()),
            pltpu.SemaphoreType.DMA(()),
            pltpu.SemaphoreType.DMA(()),
        ],
        compiler_params=_params("arbitrary"),
        name="conv_ffn",
    )(x, head_rows, wg, wu, wd, h2, mod, cw, cb, jnp.swapaxes(halo.reshape(V7X_SUBLANES, nf, TF), 0, 1), g_final)


def _ffn_head_kernel(xs_ref, mods_ref, g2_ref, wg_ref, wu_ref, wd_ref, cw_ref, cb_ref, p0_ref, p1_ref, gf_ref,
                     x_hbm, h2_ref, mod_ref,
                     os_ref, a_ref, wgb_ref, wub_ref, wdb_ref, o_ref, halo_ref,
                     h2s_ref, accs_ref, acc_ref, ext_ref, x_sem, *, final_norm):
    f = pl.program_id(0)
    tm = h2_ref.shape[0]
    last = f == pl.num_programs(0) - 1
    x_copy = pltpu.make_async_copy(x_hbm.at[pl.ds(0, tm)], o_ref, x_sem)

    @pl.when(last)
    def _():
        x_copy.start()

    @pl.when(f == 0)
    def _():
        h2s_ref[...] = (_rms(xs_ref[...], g2_ref[...]) * (1.0 + mods_ref[4]) + mods_ref[3]).astype(BF16)
        accs_ref[...] = jnp.zeros_like(accs_ref)
        acc_ref[...] = jnp.zeros_like(acc_ref)

    wg, wu, wd = wg_ref[...].astype(BF16), wu_ref[...].astype(BF16), wd_ref[...].astype(BF16)
    wgb_ref[...] = wg
    wub_ref[...] = wu
    wdb_ref[...] = wd
    cw, cb = cw_ref[f], cb_ref[f]

    width = ext_ref.shape[1]
    cols = pl.ds(pl.multiple_of(f * width, width), width)

    def sample_gate(a, up):
        a_ref[:, cols] = a
        conv = cb + cw[0:1] * p0_ref[:, cols] + cw[1:2] * p1_ref[:, cols] + cw[2:3] * a
        return (jax.nn.silu(conv) * up).astype(BF16)

    ext_ref[0:V7X_SUBLANES] = jnp.zeros((V7X_SUBLANES, ext_ref.shape[1]), F32)
    _ffn_tile(h2_ref, ext_ref, acc_ref, [wg], [wu], wd, cw, cb, riders=(h2s_ref[...], sample_gate, accs_ref))
    halo_ref[:, cols] = ext_ref[tm:tm + V7X_SUBLANES]

    @pl.when(last)
    def _():
        x2 = xs_ref[...] + mods_ref[5] * accs_ref[...]
        os_ref[...] = _rms(x2, gf_ref[...]) if final_norm else x2
        x_copy.wait()
        _ffn_finish(o_ref, acc_ref, mod_ref[5:6], gf_ref, final_norm)


def _ffn_head(layer, xs, mods, g2, w_gate, w_up, w_down, cw, cb, p0, p1, g_final, x, h2, mod, final_norm):
    rows = xs.shape[0]
    tokens = x.shape[0]
    nfh = D_FF // TF_HEAD
    full = lambda shape: pl.BlockSpec(shape, lambda f: (0,) * len(shape))
    tile_copy = pl.BlockSpec((None, D_MODEL, TF_HEAD), lambda f: (f, 0, 0))
    return pl.pallas_call(
        functools.partial(_ffn_head_kernel, final_norm=final_norm),
        out_shape=(
            jax.ShapeDtypeStruct((rows, D_MODEL), F32), jax.ShapeDtypeStruct((rows, D_FF), F32),
            jax.ShapeDtypeStruct((nfh, D_MODEL, TF_HEAD), BF16), jax.ShapeDtypeStruct((nfh, D_MODEL, TF_HEAD), BF16),
            jax.ShapeDtypeStruct((D_FF, D_MODEL), BF16),
            jax.ShapeDtypeStruct((TM_FF, D_MODEL), F32),
            jax.ShapeDtypeStruct((V7X_SUBLANES, D_FF), F32),
        ),
        grid=(nfh,),
        in_specs=[
            full((rows, D_MODEL)), full((6, rows, D_MODEL)), full((1, D_MODEL)),
            pl.BlockSpec((None, D_MODEL, TF_HEAD), lambda f: (layer, 0, f)),
            pl.BlockSpec((None, D_MODEL, TF_HEAD), lambda f: (layer, 0, f)),
            pl.BlockSpec((None, TF_HEAD, D_MODEL), lambda f: (layer, f, 0)),
            full((nfh, 3, TF_HEAD)), full((nfh, 1, TF_HEAD)),
            full((rows, D_FF)), full((rows, D_FF)),
            full((1, D_MODEL)),
            pl.BlockSpec(memory_space=pl.ANY),
            pl.BlockSpec((TM_FF, D_MODEL), lambda f: (0, 0), pipeline_mode=pl.Buffered(1)),
            pl.BlockSpec((None, 6, D_MODEL), lambda f: (0, 0, 0)),
        ],
        out_specs=(
            full((rows, D_MODEL)), full((rows, D_FF)),
            tile_copy, tile_copy,
            pl.BlockSpec((TF_HEAD, D_MODEL), lambda f: (f, 0)),
            pl.BlockSpec((TM_FF, D_MODEL), lambda f: (0, 0)),
            full((V7X_SUBLANES, D_FF)),
        ),
        scratch_shapes=[
            pltpu.VMEM((rows, D_MODEL), BF16), pltpu.VMEM((rows, D_MODEL), F32),
            pltpu.VMEM((TM_FF, D_MODEL), F32),
            pltpu.VMEM((TM_FF + V7X_SUBLANES, TF_HEAD), F32),
            pltpu.SemaphoreType.DMA(()),
        ],
        compiler_params=_params("arbitrary"),
        name="conv_ffn_head",
    )(xs, mods, g2, w_gate, w_up, w_down, cw, cb, p0, p1, g_final, x, h2, mod)


def _s_in_proj_kernel(x_ref, mod_ref, g1_ref, w_ref, z_ref, wb_ref, h_ref):
    @pl.when(pl.program_id(0) == 0)
    def _():
        h_ref[...] = (_rms(x_ref[...], g1_ref[...]) * (1.0 + mod_ref[1]) + mod_ref[0]).astype(BF16)

    w = w_ref[...].astype(BF16)
    wb_ref[...] = w
    z_ref[...] = _dot(h_ref[...], w)


def _s_in_proj(layer, x, mod, g1, w_in):
    rows = x.shape[0]
    full = lambda shape: pl.BlockSpec(shape, lambda j: (0,) * len(shape))
    return pl.pallas_call(
        _s_in_proj_kernel,
        out_shape=(jax.ShapeDtypeStruct((rows, IN_COLS), F32), jax.ShapeDtypeStruct((D_MODEL, IN_COLS), BF16)),
        grid=(IN_COLS // TN_IN,),
        in_specs=[
            full((rows, D_MODEL)), full(mod.shape), full((1, D_MODEL)),
            pl.BlockSpec((None, D_MODEL, TN_IN), lambda j: (layer, 0, j)),
        ],
        out_specs=(pl.BlockSpec((rows, TN_IN), lambda j: (0, j)), pl.BlockSpec((D_MODEL, TN_IN), lambda j: (0, j))),
        scratch_shapes=[pltpu.VMEM((rows, D_MODEL), BF16)],
        compiler_params=_params("arbitrary"),
        name="sample_in_proj",
    )(x, mod, g1, w_in)


def _s_attn_kernel(sink_ref, z_ref, cos_ref, sin_ref, gain_ref, ws0_ref, bs0_ref, kt_ref, vt_ref,
                   attn_ref, gm_ref, gvn_ref, ktn_ref, vtn_ref):
    nseq = z_ref.shape[0]
    q_blocks, k_blocks, u, gvn = _post_project(
        z_ref[:, Q0:K0], z_ref[:, K0:V0], z_ref[:, U0:G0], z_ref[:, G0:IN_COLS],
        cos_ref[...], sin_ref[...], gain_ref[...], pair_q=False, q_scale=HEAD_DIM ** -0.5)
    v_new = z_ref[:, V0:U0]
    k_new = jnp.concatenate(k_blocks, axis=1)
    gvn_ref[...] = gvn
    gm_ref[...] = (u * (ws0_ref[...].astype(BF16).astype(F32) * gvn.astype(BF16).astype(F32) + bs0_ref[...])).astype(BF16)

    rows = GQA_GROUP * nseq
    keys = nseq * WINDOW
    seq_of_row = jnp.bitwise_and(_iota((rows, 1), 0), nseq - 1)
    head_of_row = lax.shift_right_logical(_iota((rows, 1), 0), nseq.bit_length() - 1)
    own_keys = lax.shift_right_logical(_iota((rows, keys), 1), WINDOW.bit_length() - 1) == seq_of_row
    own_new = _iota((rows, nseq), 1) == seq_of_row
    pieces = []
    for g in range(N_KV_HEADS):
        gs = slice(g * HEAD_DIM, (g + 1) * HEAD_DIM)
        heads = [g * GQA_GROUP + r for r in range(GQA_GROUP)]
        qg = jnp.concatenate(
            [q_blocks[h // 2][:, (h % 2) * HEAD_DIM:(h % 2 + 1) * HEAD_DIM] for h in heads], axis=0).astype(BF16)
        ktg = jnp.concatenate([kt_ref[b, gs, :] for b in range(nseq)], axis=1).astype(BF16)
        vtg = jnp.concatenate([vt_ref[b, gs, :] for b in range(nseq)], axis=1).astype(BF16)
        s = jnp.where(own_keys, _dot(qg, ktg), NEG)
        sn = jnp.where(own_new, _dot_nt(qg, k_new[:, gs].astype(BF16)), NEG)
        sk = jnp.full((rows, 1), sink_ref[heads[0]], F32)
        for r in range(1, GQA_GROUP):
            sk = jnp.where(head_of_row == r, sink_ref[heads[r]], sk)
        m = jnp.maximum(jnp.maximum(jnp.max(s, axis=-1, keepdims=True), jnp.max(sn, axis=-1, keepdims=True)), sk)
        p = jnp.exp(s - m)
        pn = jnp.exp(sn - m)
        denom = jnp.sum(p, axis=-1, keepdims=True) + jnp.sum(pn, axis=-1, keepdims=True) + jnp.exp(sk - m)
        o = (_dot_nt(p.astype(BF16), vtg) + _dot(pn.astype(BF16), v_new[:, gs].astype(BF16))) / denom
        pieces += [o[r * nseq:(r + 1) * nseq] for r in range(GQA_GROUP)]
    attn_ref[...] = jnp.concatenate(pieces, axis=1).astype(BF16)

    pad = jnp.zeros((V7X_LANES - nseq, KV_WIDTH), F32)
    knt = jnp.concatenate([k_new, pad], axis=0).T
    vnt = jnp.concatenate([v_new, pad], axis=0).T
    newest = _iota((1, WINDOW), 1) == WINDOW - 1
    for b in range(nseq):
        ktn_ref[b] = jnp.where(newest, knt[:, b:b + 1], pltpu.roll(kt_ref[b], WINDOW - 1, axis=1))
        vtn_ref[b] = jnp.where(newest, vnt[:, b:b + 1], pltpu.roll(vt_ref[b], WINDOW - 1, axis=1))


def _s_attention(layer, sinks, z, cos, sin_signed, gain, ws0, bs0, kt_all, vt_all):
    rows = z.shape[0]
    full = lambda shape: pl.BlockSpec(shape, lambda i: (0,) * len(shape))
    cache = pl.BlockSpec((None, rows, KV_WIDTH, WINDOW), lambda i: (layer, 0, 0, 0))
    return pl.pallas_call(
        _s_attn_kernel,
        out_shape=(
            jax.ShapeDtypeStruct((rows, ATTN_WIDTH), BF16),
            jax.ShapeDtypeStruct((rows, GM_WIDTH), BF16),
            jax.ShapeDtypeStruct((rows, GM_WIDTH), F32),
            jax.ShapeDtypeStruct((rows, KV_WIDTH, WINDOW), F32),
            jax.ShapeDtypeStruct((rows, KV_WIDTH, WINDOW), F32),
        ),
        grid=(1,),
        in_specs=[
            pl.BlockSpec(memory_space=pltpu.SMEM),
            full((rows, IN_COLS)), full((1, V7X_LANES)), full((1, V7X_LANES)),
            full((1, GM_WIDTH)), full((1, GM_WIDTH)), full((1, GM_WIDTH)),
            cache, cache,
        ],
        out_specs=(full((rows, ATTN_WIDTH)), full((rows, GM_WIDTH)), full((rows, GM_WIDTH)),
                   full((rows, KV_WIDTH, WINDOW)), full((rows, KV_WIDTH, WINDOW))),
        compiler_params=_params("arbitrary"),
        name="sample_attention",
    )(sinks, z, cos, sin_signed, gain, ws0, bs0, kt_all, vt_all)


def _paired_head(slot):
    p, rest = divmod(slot, 2 * GQA_GROUP)
    j, i = divmod(rest, 2)
    return p * 2 * GQA_GROUP + i * GQA_GROUP + j


def _s_mix_out_kernel(x_ref, gate_ref, a_ref, gm_ref, w_ref, o_ref, wb_ref):
    w = w_ref[...].astype(BF16)
    for slot in range(N_HEADS):
        head = _paired_head(slot)
        wb_ref[slot * HEAD_DIM:(slot + 1) * HEAD_DIM] = w[head * HEAD_DIM:(head + 1) * HEAD_DIM]
    wb_ref[ATTN_WIDTH:] = w[ATTN_WIDTH:]
    mix = _dot(a_ref[...], w[:ATTN_WIDTH]) + _dot(gm_ref[...], w[ATTN_WIDTH:])
    o_ref[...] = x_ref[...] + gate_ref[...] * mix


def _s_mix_out(layer, x, gate, attn, gm, w_out):
    rows = x.shape[0]
    full = lambda shape: pl.BlockSpec(shape, lambda j: (0,) * len(shape))
    col_tile = lambda r: pl.BlockSpec((r, TN_OUT), lambda j: (0, j))
    return pl.pallas_call(
        _s_mix_out_kernel,
        out_shape=(jax.ShapeDtypeStruct((rows, D_MODEL), F32), jax.ShapeDtypeStruct((D_MODEL, D_MODEL), BF16)),
        grid=(D_MODEL // TN_OUT,),
        in_specs=[
            col_tile(rows), col_tile(rows), full((rows, ATTN_WIDTH)), full((rows, GM_WIDTH)),
            pl.BlockSpec((None, D_MODEL, TN_OUT), lambda j: (layer, 0, j)),
        ],
        out_specs=(col_tile(rows), col_tile(D_MODEL)),
        compiler_params=_params("arbitrary"),
        name="sample_mix_out",
    )(x, gate, attn, gm, w_out)


def _rope_tables(pos):
    half = HEAD_DIM // 2
    inv = ROPE_THETA ** (-jnp.arange(half, dtype=jnp.float32) / half)
    ang = pos.astype(jnp.float32)[:, None] * inv[None, :]
    reps = V7X_LANES // half
    sign = jnp.where((jnp.arange(V7X_LANES) % HEAD_DIM) < half, -1.0, 1.0).astype(F32)
    return jnp.tile(jnp.cos(ang), (1, reps)), jnp.tile(jnp.sin(ang), (1, reps)) * sign[None, :]


def _to_planes(window_buf):
    lead = window_buf.shape[:-3]
    return jnp.moveaxis(window_buf, -3, -1).reshape(lead + (KV_WIDTH, WINDOW))


def _from_planes(planes):
    lead = planes.shape[:-2]
    return jnp.moveaxis(planes.reshape(lead + (N_KV_HEADS, HEAD_DIM, WINDOW)), -1, -3)


def kernel(x_prompt, x_sample, cache_k, cache_v, state_conv, c_prompt, c_sample, w_ada, b_ada, g_norm1, g_norm2,
           w_in, gm_gain, gm_ws, gm_bs, sinks, w_out, w_gate, w_up, conv_w, conv_b, w_down, g_final):
    batch, seq, _ = x_prompt.shape
    dec = x_sample.shape[0]
    nf = D_FF // TF
    assert x_sample.shape[1] == 1 and seq % TM == 0 and seq % TQ == 0 and TM % CHUNK == 0 and D_FF % TF == 0
    assert seq // TM_FF >= 2 and seq % TM_FF == 0 and TM_FF % FFN_SUBBLOCKS == 0 and TF % TF_HEAD == 0
    assert dec & (dec - 1) == 0 and dec <= V7X_LANES

    bst = jnp.swapaxes(gm_bs, 1, 2)
    ws0 = jnp.repeat(gm_ws[:, :, 0, 0], GM_WIDTH // GM_HEADS, axis=1)[:, None, :]
    bs0 = jnp.repeat(gm_bs[:, :, 0], GM_WIDTH // GM_HEADS, axis=1)[:, None, :]
    cw_tiles = jnp.swapaxes(conv_w.reshape(DEPTH, 3, nf, TF), 1, 2)
    cb_tiles = conv_b.reshape(DEPTH, nf, 1, TF)
    cw_head = jnp.swapaxes(conv_w.reshape(DEPTH, 3, D_FF // TF_HEAD, TF_HEAD), 1, 2)
    cb_head = conv_b.reshape(DEPTH, D_FF // TF_HEAD, 1, TF_HEAD)

    cos_p, sin_p = _rope_tables(jnp.arange(seq, dtype=jnp.int32))
    cos_s, sin_s = _rope_tables(PAST_LEN + jnp.arange(1, dtype=jnp.int32))

    c_all = jnp.concatenate([c_prompt, c_sample, jnp.zeros((MOD_ROWS - batch - dec, D_MODEL), F32)], axis=0)
    b_ada3 = b_ada.reshape(DEPTH, 1, 6 * D_MODEL)
    prompt_mod = lambda slab: slab[:batch].reshape(batch, -1, D_MODEL)
    sample_mod = lambda slab: jnp.swapaxes(slab[batch:batch + dec].reshape(dec, -1, D_MODEL), 0, 1)
    mod_first = _modulation_head(c_all, w_ada, b_ada3, 2 * D_MODEL)
    mod_next = None

    kt_all, vt_all = _to_planes(cache_k), _to_planes(cache_v)

    hp = x_prompt.reshape(batch * seq, D_MODEL)
    hs = x_sample.reshape(dec, D_MODEL)
    g_fin = g_final.reshape(1, D_MODEL)
    outs = [[] for _ in range(8)]
    for l in range(DEPTH):
        last = l == DEPTH - 1
        g1 = g_norm1[l].reshape(1, D_MODEL)
        g2 = g_norm2[l].reshape(1, D_MODEL)
        gain = gm_gain[l].reshape(1, GM_WIDTH)
        if l == 0:
            mod_p, mod_s = prompt_mod(mod_first), sample_mod(mod_first)
            in_rider = (c_all, w_ada, b_ada3, 0, 2 * D_MODEL, 4 * D_MODEL)
        else:
            mod_p, mod_s = prompt_mod(mod_next), sample_mod(mod_next)
            in_rider = None
        attn_rider = (c_all, w_ada, b_ada3, l + 1, 0, 6 * D_MODEL) if l + 1 < DEPTH else None

        z, w_in_b = _s_in_proj(l, hs, mod_s, g1, w_in)
        attn_s, gm_s, gvn_s, kt_new, vt_new = _s_attention(
            l, sinks[l], z, cos_s, sin_s, gain, ws0[l], bs0[l], kt_all, vt_all)
        mixed = _mix_in(hp, mod_p, g1, w_in_b, cos_p, sin_p, gain, gm_ws[l], bst[l], seq, mod_rider=in_rider)
        q, k, v, gm, gvl = mixed[:5]
        if in_rider is not None:
            slab = jnp.concatenate([mod_first, mixed[5]], axis=1)
            mod_p, mod_s = prompt_mod(slab), sample_mod(slab)
        outs[4].append(kt_new)
        outs[5].append(vt_new)
        outs[6].append(gvn_s[:, None, :])

        hs, w_out_b = _s_mix_out(l, hs, mod_s[2], attn_s, gm_s, w_out)
        attended = _attention(sinks[l], q, k, v, seq, mod_rider=attn_rider)
        attn, kt_p, vt_p = attended[:3]
        if attn_rider is not None:
            mod_next = attended[3]
        hp, h2 = _mix_out(hp, mod_p, g2, attn, gm, w_out_b, seq)
        outs[0].append(kt_p)
        outs[1].append(vt_p)
        outs[2].append(gvl)

        hs, a_new, wg_b, wu_b, wd_b, head_rows, halo = _ffn_head(
            l, hs, mod_s, g2, w_gate, w_up, w_down, cw_head[l], cb_head[l], state_conv[l, :, 0], state_conv[l, :, 1],
            g_fin, hp, h2, mod_p, last)
        hp, tail = _ffn(hp, head_rows, halo, h2, mod_p, wg_b, wu_b, wd_b, cw_tiles[l], cb_tiles[l], g_fin, seq, last)
        outs[7].append(jnp.stack([state_conv[l, :, 1], a_new], axis=1))
        tails = tail[seq // TM_FF - 2::seq // TM_FF, :, V7X_SUBLANES - 2:, :]
        outs[3].append(jnp.swapaxes(tails, 1, 2).reshape(batch, 2, D_FF))

    st = [jnp.stack(o) for o in outs]
    return (hp.reshape(batch, seq, D_MODEL), hs.reshape(dec, 1, D_MODEL),
            _from_planes(st[0]), _from_planes(st[1]), st[2], st[3],
            _from_planes(st[4]), _from_planes(st[5]), st[6], st[7])
```

```python
import functools

import jax
import jax.numpy as jnp
from jax import lax
from jax.experimental import pallas as pl
from jax.experimental.pallas import tpu as pltpu

D_MODEL = 2048
DEPTH = 2
PAST_LEN = 16384
ATTN_WIDTH = 1024
GM_WIDTH = 1024
HEAD_DIM = 64
N_HEADS = 16
N_KV_HEADS = 4
GQA_GROUP = N_HEADS // N_KV_HEADS
WINDOW = 128
ROPE_THETA = 10000.0
CHUNK = 128
GM_HEADS = 8
D_FF = 5632
KV_WIDTH = N_KV_HEADS * HEAD_DIM
IN_COLS = ATTN_WIDTH + 2 * KV_WIDTH + 2 * GM_WIDTH
EPS = 1e-6
NEG = -1e30
LOG2_E = 1.4426950408889634

Q0, K0, V0, U0, G0 = 0, ATTN_WIDTH, ATTN_WIDTH + KV_WIDTH, ATTN_WIDTH + 2 * KV_WIDTH, ATTN_WIDTH + 2 * KV_WIDTH + GM_WIDTH

V7X_LANES = 128
V7X_SUBLANES = 8
V7X_VMEM_LIMIT_BYTES = 56 * 1024 * 1024

TM = 512
TF = 512
TM_FF = 1024
TF_HEAD = 256
TQ = 1024
TN_IN = 896
TN_OUT = 1024
MOD_TN = 1024
MOD_ROWS = 48
FFN_SUBBLOCKS = 4
MIX_OUT_SUBBLOCKS = 2

BF16 = jnp.bfloat16
F32 = jnp.float32


def _params(*semantics):
    return pltpu.CompilerParams(dimension_semantics=semantics, vmem_limit_bytes=V7X_VMEM_LIMIT_BYTES)


def _dot(a, b):
    return jnp.dot(a, b, preferred_element_type=F32)


def _dot_nt(a, b):
    return lax.dot_general(a, b, (((1,), (1,)), ((), ())), preferred_element_type=F32)


def _rms(x, g):
    return x * lax.rsqrt(jnp.mean(x * x, axis=-1, keepdims=True) + EPS) * g


def _resident(shape):
    zeros = (0,) * len(shape)
    return pl.BlockSpec(shape, lambda *_: zeros, pipeline_mode=pl.Buffered(1))


def _iota(shape, dim):
    return lax.broadcasted_iota(jnp.int32, shape, dim)


def _mod_columns(c_ref, w_ref, b_ref, o_ref):
    s = jax.nn.silu(c_ref[...]).astype(BF16)
    o_ref[...] = _dot(s, w_ref[...].astype(BF16)) + b_ref[...]


def _mod_rider_specs(layer, first_col, width, step_of):
    assert first_col % width == 0
    first = first_col // width
    in_specs = [
        pl.BlockSpec((MOD_ROWS, D_MODEL), lambda *g: (0, 0)),
        pl.BlockSpec((None, D_MODEL, width), lambda *g: (layer, 0, first + step_of(*g))),
        pl.BlockSpec((None, 1, width), lambda *g: (layer, 0, first + step_of(*g))),
    ]
    return in_specs, pl.BlockSpec((MOD_ROWS, width), lambda *g: (0, step_of(*g)))


def _modulation_head(c_all, w_ada, b_ada3, n_cols):
    in_specs, out_spec = _mod_rider_specs(0, 0, MOD_TN, lambda j: j)
    return pl.pallas_call(
        _mod_columns,
        out_shape=jax.ShapeDtypeStruct((MOD_ROWS, n_cols), F32),
        grid=(n_cols // MOD_TN,),
        in_specs=in_specs,
        out_specs=out_spec,
        compiler_params=_params("arbitrary"),
        name="adaln_modulation",
    )(c_all, w_ada, b_ada3)


def _rope_cols(z, cos, sin_signed):
    lane = _iota((1, V7X_LANES), 1)
    first_half = jnp.bitwise_and(lane, HEAD_DIM - 1) < HEAD_DIM // 2
    outs = []
    for c in range(z.shape[1] // V7X_LANES):
        zc = z[:, c * V7X_LANES:(c + 1) * V7X_LANES]
        partner = jnp.where(first_half,
                            pltpu.roll(zc, V7X_LANES - HEAD_DIM // 2, axis=1),
                            pltpu.roll(zc, HEAD_DIM // 2, axis=1))
        outs.append(zc * cos + partner * sin_signed)
    return outs


def _swap_halves(x):
    return pltpu.roll(x, HEAD_DIM, axis=1)


def _pair_q_blocks(nat):
    low = _iota((1, V7X_LANES), 1) < HEAD_DIM
    out = []
    for c in range(len(nat)):
        p, j = divmod(c, 4)
        a, b = nat[p * 4 + j // 2], nat[p * 4 + 2 + j // 2]
        out.append(jnp.where(low, a, _swap_halves(b)) if j % 2 == 0 else jnp.where(low, _swap_halves(a), b))
    return out


def _post_project(zq, zk, zu, zg, cos, sin_signed, gain, pair_q, q_scale):
    q_blocks = _rope_cols(zq, cos, sin_signed)
    if pair_q:
        q_blocks = _pair_q_blocks(q_blocks)
    q_blocks = [qb * q_scale for qb in q_blocks]
    k_blocks = _rope_cols(zk, cos, sin_signed)
    u = jax.nn.gelu(zu)
    gvn = _rms(jax.nn.gelu(zg), gain)
    return q_blocks, k_blocks, u, gvn


def _store_cols(ref, blocks):
    for c, blk in enumerate(blocks):
        ref[:, c * V7X_LANES:(c + 1) * V7X_LANES] = blk.astype(ref.dtype)


def _mix_in_kernel(x_ref, mod_ref, g1_ref, w_ref, cos_ref, sin_ref, gain_ref, ws_ref, bst_ref, *rest, mod_rider):
    if mod_rider:
        c_ref, wa_ref, ba_ref, q_ref, k_ref, v_ref, gm_ref, gvl_ref, mo_ref = rest
        _mod_columns(c_ref, wa_ref, ba_ref, mo_ref)
    else:
        q_ref, k_ref, v_ref, gm_ref, gvl_ref = rest
    tm = x_ref.shape[0]
    h = (_rms(x_ref[...], g1_ref[...]) * (1.0 + mod_ref[1:2]) + mod_ref[0:1]).astype(BF16)
    q_blocks, k_blocks, u, gvn = _post_project(
        _dot(h, w_ref[:, Q0:K0]), _dot(h, w_ref[:, K0:V0]), _dot(h, w_ref[:, U0:G0]), _dot(h, w_ref[:, G0:IN_COLS]),
        cos_ref[...], sin_ref[...], gain_ref[...], pair_q=True, q_scale=HEAD_DIM ** -0.5 * LOG2_E)
    _store_cols(q_ref, q_blocks)
    _store_cols(k_ref, k_blocks)
    v_ref[...] = _dot(h, w_ref[:, V0:U0])
    gvl_ref[...] = gvn[tm - CHUNK:, :]
    gvb = gvn.astype(BF16)
    causal = _iota((CHUNK, CHUNK), 1) <= _iota((CHUNK, CHUNK), 0)
    for hh in range(GM_HEADS):
        wm = jnp.where(causal, ws_ref[hh], 0.0).astype(BF16)
        bias = bst_ref[:, hh:hh + 1]
        cs = slice(hh * CHUNK, (hh + 1) * CHUNK)
        for c in range(tm // CHUNK):
            rs = slice(c * CHUNK, (c + 1) * CHUNK)
            s = _dot(wm, gvb[rs, cs]) + bias
            gm_ref[rs, cs] = (u[rs, cs] * s).astype(BF16)


def _mix_in(x, mod, g1, w_in, cos, sin_signed, gain, ws, bst, seq, mod_rider=None):
    tokens = x.shape[0]
    tiles_per_seq = seq // TM
    batch = tokens // seq
    steps = tokens // TM
    row_tile = lambda width: pl.BlockSpec((TM, width), lambda i: (i, 0))
    rider_in, rider_out, rider_shape, rider_args = [], (), (), ()
    if mod_rider is not None:
        c_all, w_ada, b_ada3, layer, first_col, n_cols = mod_rider
        rider_in, out_spec = _mod_rider_specs(layer, first_col, n_cols // steps, lambda i: i)
        rider_out, rider_shape = (out_spec,), (jax.ShapeDtypeStruct((MOD_ROWS, n_cols), F32),)
        rider_args = (c_all, w_ada, b_ada3)
    return pl.pallas_call(
        functools.partial(_mix_in_kernel, mod_rider=mod_rider is not None),
        out_shape=(
            jax.ShapeDtypeStruct((tokens, ATTN_WIDTH), BF16),
            jax.ShapeDtypeStruct((tokens, KV_WIDTH), F32),
            jax.ShapeDtypeStruct((tokens, KV_WIDTH), F32),
            jax.ShapeDtypeStruct((tokens, GM_WIDTH), BF16),
            jax.ShapeDtypeStruct((batch, CHUNK, GM_WIDTH), F32),
        ) + rider_shape,
        grid=(steps,),
        in_specs=[
            row_tile(D_MODEL),
            pl.BlockSpec((None, mod.shape[1], D_MODEL), lambda i: (i // tiles_per_seq, 0, 0)),
            _resident((1, D_MODEL)),
            _resident((D_MODEL, IN_COLS)),
            pl.BlockSpec((TM, V7X_LANES), lambda i: (i % tiles_per_seq, 0)),
            pl.BlockSpec((TM, V7X_LANES), lambda i: (i % tiles_per_seq, 0)),
            _resident((1, GM_WIDTH)),
            _resident((GM_HEADS, CHUNK, CHUNK)),
            _resident((CHUNK, GM_HEADS)),
        ] + rider_in,
        out_specs=(
            row_tile(ATTN_WIDTH), row_tile(KV_WIDTH), row_tile(KV_WIDTH), row_tile(GM_WIDTH),
            pl.BlockSpec((None, CHUNK, GM_WIDTH), lambda i: (i // tiles_per_seq, 0, 0)),
        ) + rider_out,
        compiler_params=_params("arbitrary"),
        name="mix_in",
    )(x, mod, g1, w_in, cos, sin_signed, gain, ws, bst, *rider_args)


def _attn_block(n, sink_ref, q_blk, k_prev, k_cur, v_prev, v_cur, out_blk):
    row = _iota((WINDOW, 2 * WINDOW), 0)
    key = _iota((WINDOW, 2 * WINDOW), 1) - WINDOW
    valid = (key <= row) & (key >= jnp.maximum(row - WINDOW, -n * WINDOW))
    low = _iota((WINDOW, V7X_LANES), 1) < HEAD_DIM
    cols_per_pair = ATTN_WIDTH // V7X_LANES // 2
    for p in range(2):
        ls = slice(p * V7X_LANES, (p + 1) * V7X_LANES)
        kcat = jnp.concatenate([k_prev[:, ls], k_cur[:, ls]], axis=0).astype(BF16)
        vcat = jnp.concatenate([v_prev[:, ls], v_cur[:, ls]], axis=0).astype(BF16)
        blocks = []
        for cc in range(cols_per_pair):
            c = p * cols_per_pair + cc
            qcol = q_blk[:, c * V7X_LANES:(c + 1) * V7X_LANES]
            zero = jnp.zeros_like(qcol)
            blocks += [jnp.where(low, qcol, zero), jnp.where(low, zero, qcol)]
        s = _dot_nt(jnp.concatenate(blocks, axis=0), kcat)
        probs, inv = [], []
        for j in range(2 * cols_per_pair):
            cc, e = divmod(j, 2)
            sj = jnp.where(valid, s[j * WINDOW:(j + 1) * WINDOW], NEG)
            sk = sink_ref[p * 2 * GQA_GROUP + e * GQA_GROUP + cc] * LOG2_E
            m = jnp.maximum(jnp.max(sj, axis=-1, keepdims=True), sk)
            pj = jnp.exp2(sj - m)
            inv.append(1.0 / (jnp.sum(pj, axis=-1, keepdims=True) + jnp.exp2(sk - m)))
            probs.append(pj.astype(BF16))
        o = _dot(jnp.concatenate(probs, axis=0), vcat)
        for cc in range(cols_per_pair):
            c = p * cols_per_pair + cc
            even = o[(2 * cc) * WINDOW:(2 * cc + 1) * WINDOW] * inv[2 * cc]
            odd = o[(2 * cc + 1) * WINDOW:(2 * cc + 2) * WINDOW] * inv[2 * cc + 1]
            out_blk[:, c * V7X_LANES:(c + 1) * V7X_LANES] = jnp.where(low, even, odd).astype(BF16)


def _attn_kernel(sink_ref, q_ref, kp_ref, kc_ref, vp_ref, vc_ref, *rest, mod_rider):
    if mod_rider:
        c_ref, wa_ref, ba_ref, o_ref, kt_ref, vt_ref, mo_ref = rest
        _mod_columns(c_ref, wa_ref, ba_ref, mo_ref)
    else:
        o_ref, kt_ref, vt_ref = rest
    step = pl.program_id(1)
    blocks = q_ref.shape[0] // WINDOW
    for t in range(blocks):
        rows = slice(t * WINDOW, (t + 1) * WINDOW)
        prev = slice((t - 1) * WINDOW, t * WINDOW)
        k_prev = kp_ref[...] if t == 0 else kc_ref[prev]
        v_prev = vp_ref[...] if t == 0 else vc_ref[prev]
        _attn_block(step * blocks + t, sink_ref, q_ref.at[rows], k_prev, kc_ref[rows], v_prev, vc_ref[rows], o_ref.at[rows])

    @pl.when(step == pl.num_programs(1) - 1)
    def _():
        last = slice((blocks - 1) * WINDOW, blocks * WINDOW)
        kt_ref[...] = kc_ref[last].T
        vt_ref[...] = vc_ref[last].T


def _attention(sinks, q, k, v, seq, mod_rider=None):
    tokens = q.shape[0]
    batch = tokens // seq
    steps = seq // TQ
    per_step = TQ // WINDOW
    cur = lambda b, n: (b * steps + n, 0)
    prev = lambda b, n: (jnp.maximum((b * steps + n) * per_step - 1, 0), 0)
    state = pl.BlockSpec((None, KV_WIDTH, WINDOW), lambda b, n: (b, 0, 0))
    rider_in, rider_out, rider_shape, rider_args = [], (), (), ()
    if mod_rider is not None:
        c_all, w_ada, b_ada3, layer, first_col, n_cols = mod_rider
        rider_in, out_spec = _mod_rider_specs(layer, first_col, n_cols // (batch * steps), lambda b, n: b * steps + n)
        rider_out, rider_shape = (out_spec,), (jax.ShapeDtypeStruct((MOD_ROWS, n_cols), F32),)
        rider_args = (c_all, w_ada, b_ada3)
    return pl.pallas_call(
        functools.partial(_attn_kernel, mod_rider=mod_rider is not None),
        out_shape=(
            jax.ShapeDtypeStruct((tokens, ATTN_WIDTH), BF16),
            jax.ShapeDtypeStruct((batch, KV_WIDTH, WINDOW), F32),
            jax.ShapeDtypeStruct((batch, KV_WIDTH, WINDOW), F32),
        ) + rider_shape,
        grid=(batch, steps),
        in_specs=[
            pl.BlockSpec(memory_space=pltpu.SMEM),
            pl.BlockSpec((TQ, ATTN_WIDTH), cur),
            pl.BlockSpec((WINDOW, KV_WIDTH), prev),
            pl.BlockSpec((TQ, KV_WIDTH), cur),
            pl.BlockSpec((WINDOW, KV_WIDTH), prev),
            pl.BlockSpec((TQ, KV_WIDTH), cur),
        ] + rider_in,
        out_specs=(pl.BlockSpec((TQ, ATTN_WIDTH), cur), state, state) + rider_out,
        compiler_params=_params("arbitrary", "arbitrary"),
        name="swa_attention",
    )(sinks, q, k, k, v, v, *rider_args)


def _mix_out_kernel(x_ref, mod_ref, g2_ref, a_ref, gm_ref, wa_ref, wb_ref, o_ref, h2_ref):
    sub = x_ref.shape[0] // MIX_OUT_SUBBLOCKS
    for r in range(MIX_OUT_SUBBLOCKS):
        rs = slice(r * sub, (r + 1) * sub)
        mix = _dot(a_ref[rs], wa_ref[...]) + _dot(gm_ref[rs], wb_ref[...])
        x1 = x_ref[rs] + mod_ref[2:3] * mix
        o_ref[rs] = x1
        h2_ref[rs] = (_rms(x1, g2_ref[...]) * (1.0 + mod_ref[4:5]) + mod_ref[3:4]).astype(BF16)


def _mix_out(x, mod, g2, attn, gm, w_out, seq):
    tokens = x.shape[0]
    tiles_per_seq = seq // TM
    row_tile = lambda width: pl.BlockSpec((TM, width), lambda i: (i, 0))
    half = lambda k: pl.BlockSpec((ATTN_WIDTH, D_MODEL), lambda i: (k, 0), pipeline_mode=pl.Buffered(1))
    return pl.pallas_call(
        _mix_out_kernel,
        out_shape=(jax.ShapeDtypeStruct((tokens, D_MODEL), F32), jax.ShapeDtypeStruct((tokens, D_MODEL), BF16)),
        grid=(tokens // TM,),
        in_specs=[
            row_tile(D_MODEL),
            pl.BlockSpec((None, 6, D_MODEL), lambda i: (i // tiles_per_seq, 0, 0)),
            _resident((1, D_MODEL)),
            row_tile(ATTN_WIDTH), row_tile(GM_WIDTH),
            half(0), half(1),
        ],
        out_specs=(row_tile(D_MODEL), row_tile(D_MODEL)),
        compiler_params=_params("arbitrary"),
        name="mix_out",
    )(x, mod, g2, attn, gm, w_out, w_out)


def _ffn_tile(h2_ref, ext_ref, acc_ref, wg_parts, wu_parts, wd, cw, cb, riders=None, restart=None, finish=None):
    tm = h2_ref.shape[0]
    sub = tm // FFN_SUBBLOCKS
    width = ext_ref.shape[1] // len(wg_parts)
    ups = []
    rider_a = rider_up = None
    for r in range(FFN_SUBBLOCKS):
        h2 = h2_ref[r * sub:(r + 1) * sub]
        ride = riders is not None and r == FFN_SUBBLOCKS - 1
        if ride:
            h2 = jnp.concatenate([h2, riders[0]], axis=0)
        rows = slice(V7X_SUBLANES + r * sub, V7X_SUBLANES + (r + 1) * sub)
        a_parts = [_dot(h2, wg) for wg in wg_parts]
        up = jnp.concatenate([_dot(h2, wu) for wu in wu_parts], axis=1)
        for c, a in enumerate(a_parts):
            ext_ref[rows, c * width:(c + 1) * width] = a[:sub]
        ups.append(up[:sub])
        if ride:
            rider_a, rider_up = jnp.concatenate([a[sub:] for a in a_parts], axis=1), up[sub:]
    def carried(rows):
        old = acc_ref[rows]
        return old if restart is None else jnp.where(restart, 0.0, old)

    for r in range(FFN_SUBBLOCKS):
        lo = V7X_SUBLANES + r * sub
        conv = (cb + cw[0:1] * ext_ref[lo - 2:lo - 2 + sub]
                + cw[1:2] * ext_ref[lo - 1:lo - 1 + sub] + cw[2:3] * ext_ref[lo:lo + sub])
        y = (jax.nn.silu(conv) * ups[r]).astype(BF16)
        if riders is not None and r == FFN_SUBBLOCKS - 1:
            out = _dot(jnp.concatenate([y, riders[1](rider_a, rider_up)], axis=0), wd)
            acc_ref[r * sub:(r + 1) * sub] = carried(slice(r * sub, (r + 1) * sub)) + out[:sub]
            riders[2][...] += out[sub:]
        else:
            acc_ref[r * sub:(r + 1) * sub] = carried(slice(r * sub, (r + 1) * sub)) + _dot(y, wd)
        if finish is not None:
            finish(slice(r * sub, (r + 1) * sub))


def _ffn_finish_rows(o_ref, acc_ref, rows, gate, gf_ref, final_norm):
    x2 = o_ref[rows] + gate * acc_ref[rows]
    o_ref[rows] = _rms(x2, gf_ref[...]) if final_norm else x2


def _ffn_finish(o_ref, acc_ref, gate, gf_ref, final_norm):
    sub = o_ref.shape[0] // FFN_SUBBLOCKS

    def finish(r, carry):
        rs = pl.ds(pl.multiple_of(r * sub, sub), sub)
        x2 = o_ref[rs, :] + gate * acc_ref[rs, :]
        o_ref[rs, :] = _rms(x2, gf_ref[...]) if final_norm else x2
        return carry

    lax.fori_loop(0, FFN_SUBBLOCKS, finish, 0)


def _ffn_kernel(x_hbm, head_hbm, wg_hbm, wu_hbm, wd_hbm, h2_ref, mod_ref, cw_ref, cb_ref, halo_ref, gf_ref,
                o_hbm, tail_ref, wg_buf, wu_buf, wd_buf, obuf_ref, acc_ref, ext_ref, carry_ref,
                w_sems, x_sem, wb_sem, head_sem, *, first_tile, tiles_per_seq, final_norm):
    i = pl.program_id(0)
    n_tiles = pl.num_programs(0)
    nf = wd_hbm.shape[0] // wd_buf.shape[1]
    per = wg_buf.shape[1]
    tm = h2_ref.shape[0]
    tile = i + first_tile
    rows = pl.ds(pl.multiple_of(tile * tm, tm), tm)
    x_copy = pltpu.make_async_copy(x_hbm.at[rows], obuf_ref, x_sem)
    write_back = pltpu.make_async_copy(obuf_ref, o_hbm.at[rows], wb_sem)
    head_copy = pltpu.make_async_copy(head_hbm, o_hbm.at[pl.ds(0, first_tile * tm)], head_sem)

    def weight_copies(f, slot):
        return (pltpu.make_async_copy(wg_hbm.at[pl.ds(f * per, per)], wg_buf.at[slot], w_sems.at[0, slot]),
                pltpu.make_async_copy(wu_hbm.at[pl.ds(f * per, per)], wu_buf.at[slot], w_sems.at[1, slot]),
                pltpu.make_async_copy(wd_hbm.at[pl.ds(pl.multiple_of(f * wd_buf.shape[1], wd_buf.shape[1]), wd_buf.shape[1])],
                                      wd_buf.at[slot], w_sems.at[2, slot]))

    @pl.when(i == 0)
    def _():
        head_copy.start()
        for c in weight_copies(0, 0):
            c.start()
        acc_ref[...] = jnp.zeros_like(acc_ref)

    seq_start = (tile % tiles_per_seq) == 0

    def column_step(f, carry, finish=None):
        step = i * nf + f
        slot = step % 2
        last = f == nf - 1
        for c in weight_copies(f, slot):
            c.wait()

        @pl.when(jnp.logical_not(last & (i == n_tiles - 1)))
        def _():
            nxt = jnp.where(last, 0, f + 1)
            for c in weight_copies(nxt, 1 - slot):
                c.start()

        fetch = f == nf - 2

        @pl.when(fetch & (i > 0))
        def _():
            write_back.wait()

        @pl.when(fetch)
        def _():
            x_copy.start()

        @pl.when(seq_start)
        def _():
            ext_ref[0:V7X_SUBLANES] = jnp.zeros((V7X_SUBLANES, ext_ref.shape[1]), F32)

        @pl.when(jnp.logical_not(seq_start) & (i == 0))
        def _():
            ext_ref[0:V7X_SUBLANES] = halo_ref[f]

        @pl.when(jnp.logical_not(seq_start) & (i > 0))
        def _():
            ext_ref[0:V7X_SUBLANES] = carry_ref[f]

        _ffn_tile(h2_ref, ext_ref, acc_ref, [wg_buf[slot, c] for c in range(per)], [wu_buf[slot, c] for c in range(per)],
                  wd_buf[slot], cw_ref[f], cb_ref[f], restart=f == 0, finish=finish)
        tail = ext_ref[tm:tm + V7X_SUBLANES]
        carry_ref[f] = tail
        tail_ref[f] = tail
        return carry

    lax.fori_loop(0, nf - 1, column_step, 0)
    x_copy.wait()
    column_step(jnp.int32(nf - 1), 0,
                finish=lambda rws: _ffn_finish_rows(obuf_ref, acc_ref, rws, mod_ref[5:6], gf_ref, final_norm))
    write_back.start()

    @pl.when(i == n_tiles - 1)
    def _():
        write_back.wait()
        head_copy.wait()


def _ffn(x, head_rows, halo, h2, mod, wg, wu, wd, cw, cb, g_final, seq, final_norm):
    tokens = x.shape[0]
    tiles_per_seq = seq // TM_FF
    nf = D_FF // TF
    per = TF // TF_HEAD
    first_tile = 1
    kern = functools.partial(_ffn_kernel, first_tile=first_tile, tiles_per_seq=tiles_per_seq, final_norm=final_norm)
    hbm = pl.BlockSpec(memory_space=pl.ANY)
    return pl.pallas_call(
        kern,
        out_shape=(
            jax.ShapeDtypeStruct((tokens, D_MODEL), F32),
            jax.ShapeDtypeStruct((tokens // TM_FF - first_tile, nf, V7X_SUBLANES, TF), F32),
        ),
        grid=(tokens // TM_FF - first_tile,),
        in_specs=[
            hbm, hbm, hbm, hbm, hbm,
            pl.BlockSpec((TM_FF, D_MODEL), lambda i: (i + first_tile, 0)),
            pl.BlockSpec((None, 6, D_MODEL), lambda i: ((i + first_tile) // tiles_per_seq, 0, 0)),
            _resident((nf, 3, TF)),
            _resident((nf, 1, TF)),
            _resident((nf, V7X_SUBLANES, TF)),
            _resident((1, D_MODEL)),
        ],
        out_specs=(
            hbm,
            pl.BlockSpec((None, nf, V7X_SUBLANES, TF), lambda i: (i, 0, 0, 0)),
        ),
        scratch_shapes=[
            pltpu.VMEM((2, per, D_MODEL, TF_HEAD), BF16),
            pltpu.VMEM((2, per, D_MODEL, TF_HEAD), BF16),
            pltpu.VMEM((2, TF, D_MODEL), BF16),
            pltpu.VMEM((TM_FF, D_MODEL), F32),
            pltpu.VMEM((TM_FF, D_MODEL), F32),
            pltpu.VMEM((TM_FF + V7X_SUBLANES, TF), F32),
            pltpu.VMEM((nf, V7X_SUBLANES, TF), F32),
            pltpu.SemaphoreType.DMA((3, 2)),
            pltpu.SemaphoreType.DMA(()),
            pltpu.SemaphoreType.DMA(()),
            pltpu.SemaphoreType.DMA(()),
        ],
        compiler_params=_params("arbitrary"),
        name="conv_ffn",
    )(x, head_rows, wg, wu, wd, h2, mod, cw, cb, jnp.swapaxes(halo.reshape(V7X_SUBLANES, nf, TF), 0, 1), g_final)


def _ffn_head_kernel(xs_ref, mods_ref, g2_ref, wg_ref, wu_ref, wd_ref, cw_ref, cb_ref, p0_ref, p1_ref, gf_ref,
                     x_hbm, h2_ref, mod_ref,
                     os_ref, a_ref, wgb_ref, wub_ref, wdb_ref, o_ref, halo_ref,
                     h2s_ref, accs_ref, acc_ref, ext_ref, x_sem, *, final_norm):
    f = pl.program_id(0)
    tm = h2_ref.shape[0]
    last = f == pl.num_programs(0) - 1
    x_copy = pltpu.make_async_copy(x_hbm.at[pl.ds(0, tm)], o_ref, x_sem)

    @pl.when(last)
    def _():
        x_copy.start()

    @pl.when(f == 0)
    def _():
        h2s_ref[...] = (_rms(xs_ref[...], g2_ref[...]) * (1.0 + mods_ref[4]) + mods_ref[3]).astype(BF16)
        accs_ref[...] = jnp.zeros_like(accs_ref)
        acc_ref[...] = jnp.zeros_like(acc_ref)

    wg, wu, wd = wg_ref[...].astype(BF16), wu_ref[...].astype(BF16), wd_ref[...].astype(BF16)
    wgb_ref[...] = wg
    wub_ref[...] = wu
    wdb_ref[...] = wd
    cw, cb = cw_ref[f], cb_ref[f]

    width = ext_ref.shape[1]
    cols = pl.ds(pl.multiple_of(f * width, width), width)

    def sample_gate(a, up):
        a_ref[:, cols] = a
        conv = cb + cw[0:1] * p0_ref[:, cols] + cw[1:2] * p1_ref[:, cols] + cw[2:3] * a
        return (jax.nn.silu(conv) * up).astype(BF16)

    ext_ref[0:V7X_SUBLANES] = jnp.zeros((V7X_SUBLANES, ext_ref.shape[1]), F32)
    _ffn_tile(h2_ref, ext_ref, acc_ref, [wg], [wu], wd, cw, cb, riders=(h2s_ref[...], sample_gate, accs_ref))
    halo_ref[:, cols] = ext_ref[tm:tm + V7X_SUBLANES]

    @pl.when(last)
    def _():
        x2 = xs_ref[...] + mods_ref[5] * accs_ref[...]
        os_ref[...] = _rms(x2, gf_ref[...]) if final_norm else x2
        x_copy.wait()
        _ffn_finish(o_ref, acc_ref, mod_ref[5:6], gf_ref, final_norm)


def _ffn_head(layer, xs, mods, g2, w_gate, w_up, w_down, cw, cb, p0, p1, g_final, x, h2, mod, final_norm):
    rows = xs.shape[0]
    tokens = x.shape[0]
    nfh = D_FF // TF_HEAD
    full = lambda shape: pl.BlockSpec(shape, lambda f: (0,) * len(shape))
    tile_copy = pl.BlockSpec((None, D_MODEL, TF_HEAD), lambda f: (f, 0, 0))
    return pl.pallas_call(
        functools.partial(_ffn_head_kernel, final_norm=final_norm),
        out_shape=(
            jax.ShapeDtypeStruct((rows, D_MODEL), F32), jax.ShapeDtypeStruct((rows, D_FF), F32),
            jax.ShapeDtypeStruct((nfh, D_MODEL, TF_HEAD), BF16), jax.ShapeDtypeStruct((nfh, D_MODEL, TF_HEAD), BF16),
            jax.ShapeDtypeStruct((D_FF, D_MODEL), BF16),
            jax.ShapeDtypeStruct((TM_FF, D_MODEL), F32),
            jax.ShapeDtypeStruct((V7X_SUBLANES, D_FF), F32),
        ),
        grid=(nfh,),
        in_specs=[
            full((rows, D_MODEL)), full((6, rows, D_MODEL)), full((1, D_MODEL)),
            pl.BlockSpec((None, D_MODEL, TF_HEAD), lambda f: (layer, 0, f)),
            pl.BlockSpec((None, D_MODEL, TF_HEAD), lambda f: (layer, 0, f)),
            pl.BlockSpec((None, TF_HEAD, D_MODEL), lambda f: (layer, f, 0)),
            full((nfh, 3, TF_HEAD)), full((nfh, 1, TF_HEAD)),
            full((rows, D_FF)), full((rows, D_FF)),
            full((1, D_MODEL)),
            pl.BlockSpec(memory_space=pl.ANY),
            pl.BlockSpec((TM_FF, D_MODEL), lambda f: (0, 0), pipeline_mode=pl.Buffered(1)),
            pl.BlockSpec((None, 6, D_MODEL), lambda f: (0, 0, 0)),
        ],
        out_specs=(
            full((rows, D_MODEL)), full((rows, D_FF)),
            tile_copy, tile_copy,
            pl.BlockSpec((TF_HEAD, D_MODEL), lambda f: (f, 0)),
            pl.BlockSpec((TM_FF, D_MODEL), lambda f: (0, 0)),
            full((V7X_SUBLANES, D_FF)),
        ),
        scratch_shapes=[
            pltpu.VMEM((rows, D_MODEL), BF16), pltpu.VMEM((rows, D_MODEL), F32),
            pltpu.VMEM((TM_FF, D_MODEL), F32),
            pltpu.VMEM((TM_FF + V7X_SUBLANES, TF_HEAD), F32),
            pltpu.SemaphoreType.DMA(()),
        ],
        compiler_params=_params("arbitrary"),
        name="conv_ffn_head",
    )(xs, mods, g2, w_gate, w_up, w_down, cw, cb, p0, p1, g_final, x, h2, mod)


def _s_in_proj_kernel(x_ref, mod_ref, g1_ref, w_ref, z_ref, wb_ref, h_ref):
    @pl.when(pl.program_id(0) == 0)
    def _():
        h_ref[...] = (_rms(x_ref[...], g1_ref[...]) * (1.0 + mod_ref[1]) + mod_ref[0]).astype(BF16)

    w = w_ref[...].astype(BF16)
    wb_ref[...] = w
    z_ref[...] = _dot(h_ref[...], w)


def _s_in_proj(layer, x, mod, g1, w_in):
    rows = x.shape[0]
    full = lambda shape: pl.BlockSpec(shape, lambda j: (0,) * len(shape))
    return pl.pallas_call(
        _s_in_proj_kernel,
        out_shape=(jax.ShapeDtypeStruct((rows, IN_COLS), F32), jax.ShapeDtypeStruct((D_MODEL, IN_COLS), BF16)),
        grid=(IN_COLS // TN_IN,),
        in_specs=[
            full((rows, D_MODEL)), full(mod.shape), full((1, D_MODEL)),
            pl.BlockSpec((None, D_MODEL, TN_IN), lambda j: (layer, 0, j)),
        ],
        out_specs=(pl.BlockSpec((rows, TN_IN), lambda j: (0, j)), pl.BlockSpec((D_MODEL, TN_IN), lambda j: (0, j))),
        scratch_shapes=[pltpu.VMEM((rows, D_MODEL), BF16)],
        compiler_params=_params("arbitrary"),
        name="sample_in_proj",
    )(x, mod, g1, w_in)


def _s_attn_kernel(sink_ref, z_ref, cos_ref, sin_ref, gain_ref, ws0_ref, bs0_ref, kt_ref, vt_ref,
                   attn_ref, gm_ref, gvn_ref, ktn_ref, vtn_ref):
    nseq = z_ref.shape[0]
    q_blocks, k_blocks, u, gvn = _post_project(
        z_ref[:, Q0:K0], z_ref[:, K0:V0], z_ref[:, U0:G0], z_ref[:, G0:IN_COLS],
        cos_ref[...], sin_ref[...], gain_ref[...], pair_q=False, q_scale=HEAD_DIM ** -0.5)
    v_new = z_ref[:, V0:U0]
    k_new = jnp.concatenate(k_blocks, axis=1)
    gvn_ref[...] = gvn
    gm_ref[...] = (u * (ws0_ref[...].astype(BF16).astype(F32) * gvn.astype(BF16).astype(F32) + bs0_ref[...])).astype(BF16)

    rows = GQA_GROUP * nseq
    keys = nseq * WINDOW
    seq_of_row = jnp.bitwise_and(_iota((rows, 1), 0), nseq - 1)
    head_of_row = lax.shift_right_logical(_iota((rows, 1), 0), nseq.bit_length() - 1)
    own_keys = lax.shift_right_logical(_iota((rows, keys), 1), WINDOW.bit_length() - 1) == seq_of_row
    own_new = _iota((rows, nseq), 1) == seq_of_row
    pieces = []
    for g in range(N_KV_HEADS):
        gs = slice(g * HEAD_DIM, (g + 1) * HEAD_DIM)
        heads = [g * GQA_GROUP + r for r in range(GQA_GROUP)]
        qg = jnp.concatenate(
            [q_blocks[h // 2][:, (h % 2) * HEAD_DIM:(h % 2 + 1) * HEAD_DIM] for h in heads], axis=0).astype(BF16)
        ktg = jnp.concatenate([kt_ref[b, gs, :] for b in range(nseq)], axis=1).astype(BF16)
        vtg = jnp.concatenate([vt_ref[b, gs, :] for b in range(nseq)], axis=1).astype(BF16)
        s = jnp.where(own_keys, _dot(qg, ktg), NEG)
        sn = jnp.where(own_new, _dot_nt(qg, k_new[:, gs].astype(BF16)), NEG)
        sk = jnp.full((rows, 1), sink_ref[heads[0]], F32)
        for r in range(1, GQA_GROUP):
            sk = jnp.where(head_of_row == r, sink_ref[heads[r]], sk)
        m = jnp.maximum(jnp.maximum(jnp.max(s, axis=-1, keepdims=True), jnp.max(sn, axis=-1, keepdims=True)), sk)
        p = jnp.exp(s - m)
        pn = jnp.exp(sn - m)
        denom = jnp.sum(p, axis=-1, keepdims=True) + jnp.sum(pn, axis=-1, keepdims=True) + jnp.exp(sk - m)
        o = (_dot_nt(p.astype(BF16), vtg) + _dot(pn.astype(BF16), v_new[:, gs].astype(BF16))) / denom
        pieces += [o[r * nseq:(r + 1) * nseq] for r in range(GQA_GROUP)]
    attn_ref[...] = jnp.concatenate(pieces, axis=1).astype(BF16)

    pad = jnp.zeros((V7X_LANES - nseq, KV_WIDTH), F32)
    knt = jnp.concatenate([k_new, pad], axis=0).T
    vnt = jnp.concatenate([v_new, pad], axis=0).T
    newest = _iota((1, WINDOW), 1) == WINDOW - 1
    for b in range(nseq):
        ktn_ref[b] = jnp.where(newest, knt[:, b:b + 1], pltpu.roll(kt_ref[b], WINDOW - 1, axis=1))
        vtn_ref[b] = jnp.where(newest, vnt[:, b:b + 1], pltpu.roll(vt_ref[b], WINDOW - 1, axis=1))


def _s_attention(layer, sinks, z, cos, sin_signed, gain, ws0, bs0, kt_all, vt_all):
    rows = z.shape[0]
    full = lambda shape: pl.BlockSpec(shape, lambda i: (0,) * len(shape))
    cache = pl.BlockSpec((None, rows, KV_WIDTH, WINDOW), lambda i: (layer, 0, 0, 0))
    return pl.pallas_call(
        _s_attn_kernel,
        out_shape=(
            jax.ShapeDtypeStruct((rows, ATTN_WIDTH), BF16),
            jax.ShapeDtypeStruct((rows, GM_WIDTH), BF16),
            jax.ShapeDtypeStruct((rows, GM_WIDTH), F32),
            jax.ShapeDtypeStruct((rows, KV_WIDTH, WINDOW), F32),
            jax.ShapeDtypeStruct((rows, KV_WIDTH, WINDOW), F32),
        ),
        grid=(1,),
        in_specs=[
            pl.BlockSpec(memory_space=pltpu.SMEM),
            full((rows, IN_COLS)), full((1, V7X_LANES)), full((1, V7X_LANES)),
            full((1, GM_WIDTH)), full((1, GM_WIDTH)), full((1, GM_WIDTH)),
            cache, cache,
        ],
        out_specs=(full((rows, ATTN_WIDTH)), full((rows, GM_WIDTH)), full((rows, GM_WIDTH)),
                   full((rows, KV_WIDTH, WINDOW)), full((rows, KV_WIDTH, WINDOW))),
        compiler_params=_params("arbitrary"),
        name="sample_attention",
    )(sinks, z, cos, sin_signed, gain, ws0, bs0, kt_all, vt_all)


def _paired_head(slot):
    p, rest = divmod(slot, 2 * GQA_GROUP)
    j, i = divmod(rest, 2)
    return p * 2 * GQA_GROUP + i * GQA_GROUP + j


def _s_mix_out_kernel(x_ref, gate_ref, a_ref, gm_ref, w_ref, o_ref, wb_ref):
    w = w_ref[...].astype(BF16)
    for slot in range(N_HEADS):
        head = _paired_head(slot)
        wb_ref[slot * HEAD_DIM:(slot + 1) * HEAD_DIM] = w[head * HEAD_DIM:(head + 1) * HEAD_DIM]
    wb_ref[ATTN_WIDTH:] = w[ATTN_WIDTH:]
    mix = _dot(a_ref[...], w[:ATTN_WIDTH]) + _dot(gm_ref[...], w[ATTN_WIDTH:])
    o_ref[...] = x_ref[...] + gate_ref[...] * mix


def _s_mix_out(layer, x, gate, attn, gm, w_out):
    rows = x.shape[0]
    full = lambda shape: pl.BlockSpec(shape, lambda j: (0,) * len(shape))
    col_tile = lambda r: pl.BlockSpec((r, TN_OUT), lambda j: (0, j))
    return pl.pallas_call(
        _s_mix_out_kernel,
        out_shape=(jax.ShapeDtypeStruct((rows, D_MODEL), F32), jax.ShapeDtypeStruct((D_MODEL, D_MODEL), BF16)),
        grid=(D_MODEL // TN_OUT,),
        in_specs=[
            col_tile(rows), col_tile(rows), full((rows, ATTN_WIDTH)), full((rows, GM_WIDTH)),
            pl.BlockSpec((None, D_MODEL, TN_OUT), lambda j: (layer, 0, j)),
        ],
        out_specs=(col_tile(rows), col_tile(D_MODEL)),
        compiler_params=_params("arbitrary"),
        name="sample_mix_out",
    )(x, gate, attn, gm, w_out)


def _rope_tables(pos):
    half = HEAD_DIM // 2
    inv = ROPE_THETA ** (-jnp.arange(half, dtype=jnp.float32) / half)
    ang = pos.astype(jnp.float32)[:, None] * inv[None, :]
    reps = V7X_LANES // half
    sign = jnp.where((jnp.arange(V7X_LANES) % HEAD_DIM) < half, -1.0, 1.0).astype(F32)
    return jnp.tile(jnp.cos(ang), (1, reps)), jnp.tile(jnp.sin(ang), (1, reps)) * sign[None, :]


def _to_planes(window_buf):
    lead = window_buf.shape[:-3]
    return jnp.moveaxis(window_buf, -3, -1).reshape(lead + (KV_WIDTH, WINDOW))


def _from_planes(planes):
    lead = planes.shape[:-2]
    return jnp.moveaxis(planes.reshape(lead + (N_KV_HEADS, HEAD_DIM, WINDOW)), -1, -3)


def kernel(x_prompt, x_sample, cache_k, cache_v, state_conv, c_prompt, c_sample, w_ada, b_ada, g_norm1, g_norm2,
           w_in, gm_gain, gm_ws, gm_bs, sinks, w_out, w_gate, w_up, conv_w, conv_b, w_down, g_final):
    batch, seq, _ = x_prompt.shape
    dec = x_sample.shape[0]
    nf = D_FF // TF
    assert x_sample.shape[1] == 1 and seq % TM == 0 and seq % TQ == 0 and TM % CHUNK == 0 and D_FF % TF == 0
    assert seq // TM_FF >= 2 and seq % TM_FF == 0 and TM_FF % FFN_SUBBLOCKS == 0 and TF % TF_HEAD == 0
    assert dec & (dec - 1) == 0 and dec <= V7X_LANES

    bst = jnp.swapaxes(gm_bs, 1, 2)
    ws0 = jnp.repeat(gm_ws[:, :, 0, 0], GM_WIDTH // GM_HEADS, axis=1)[:, None, :]
    bs0 = jnp.repeat(gm_bs[:, :, 0], GM_WIDTH // GM_HEADS, axis=1)[:, None, :]
    cw_tiles = jnp.swapaxes(conv_w.reshape(DEPTH, 3, nf, TF), 1, 2)
    cb_tiles = conv_b.reshape(DEPTH, nf, 1, TF)
    cw_head = jnp.swapaxes(conv_w.reshape(DEPTH, 3, D_FF // TF_HEAD, TF_HEAD), 1, 2)
    cb_head = conv_b.reshape(DEPTH, D_FF // TF_HEAD, 1, TF_HEAD)

    cos_p, sin_p = _rope_tables(jnp.arange(seq, dtype=jnp.int32))
    cos_s, sin_s = _rope_tables(PAST_LEN + jnp.arange(1, dtype=jnp.int32))

    c_all = jnp.concatenate([c_prompt, c_sample, jnp.zeros((MOD_ROWS - batch - dec, D_MODEL), F32)], axis=0)
    b_ada3 = b_ada.reshape(DEPTH, 1, 6 * D_MODEL)
    prompt_mod = lambda slab: slab[:batch].reshape(batch, -1, D_MODEL)
    sample_mod = lambda slab: jnp.swapaxes(slab[batch:batch + dec].reshape(dec, -1, D_MODEL), 0, 1)
    mod_first = _modulation_head(c_all, w_ada, b_ada3, 2 * D_MODEL)
    mod_next = None

    kt_all, vt_all = _to_planes(cache_k), _to_planes(cache_v)

    hp = x_prompt.reshape(batch * seq, D_MODEL)
    hs = x_sample.reshape(dec, D_MODEL)
    g_fin = g_final.reshape(1, D_MODEL)
    outs = [[] for _ in range(8)]
    for l in range(DEPTH):
        last = l == DEPTH - 1
        g1 = g_norm1[l].reshape(1, D_MODEL)
        g2 = g_norm2[l].reshape(1, D_MODEL)
        gain = gm_gain[l].reshape(1, GM_WIDTH)
        if l == 0:
            mod_p, mod_s = prompt_mod(mod_first), sample_mod(mod_first)
            in_rider = (c_all, w_ada, b_ada3, 0, 2 * D_MODEL, 4 * D_MODEL)
        else:
            mod_p, mod_s = prompt_mod(mod_next), sample_mod(mod_next)
            in_rider = None
        attn_rider = (c_all, w_ada, b_ada3, l + 1, 0, 6 * D_MODEL) if l + 1 < DEPTH else None

        z, w_in_b = _s_in_proj(l, hs, mod_s, g1, w_in)
        attn_s, gm_s, gvn_s, kt_new, vt_new = _s_attention(
            l, sinks[l], z, cos_s, sin_s, gain, ws0[l], bs0[l], kt_all, vt_all)
        mixed = _mix_in(hp, mod_p, g1, w_in_b, cos_p, sin_p, gain, gm_ws[l], bst[l], seq, mod_rider=in_rider)
        q, k, v, gm, gvl = mixed[:5]
        if in_rider is not None:
            slab = jnp.concatenate([mod_first, mixed[5]], axis=1)
            mod_p, mod_s = prompt_mod(slab), sample_mod(slab)
        outs[4].append(kt_new)
        outs[5].append(vt_new)
        outs[6].append(gvn_s[:, None, :])

        hs, w_out_b = _s_mix_out(l, hs, mod_s[2], attn_s, gm_s, w_out)
        attended = _attention(sinks[l], q, k, v, seq, mod_rider=attn_rider)
        attn, kt_p, vt_p = attended[:3]
        if attn_rider is not None:
            mod_next = attended[3]
        hp, h2 = _mix_out(hp, mod_p, g2, attn, gm, w_out_b, seq)
        outs[0].append(kt_p)
        outs[1].append(vt_p)
        outs[2].append(gvl)

        hs, a_new, wg_b, wu_b, wd_b, head_rows, halo = _ffn_head(
            l, hs, mod_s, g2, w_gate, w_up, w_down, cw_head[l], cb_head[l], state_conv[l, :, 0], state_conv[l, :, 1],
            g_fin, hp, h2, mod_p, last)
        hp, tail = _ffn(hp, head_rows, halo, h2, mod_p, wg_b, wu_b, wd_b, cw_tiles[l], cb_tiles[l], g_fin, seq, last)
        outs[7].append(jnp.stack([state_conv[l, :, 1], a_new], axis=1))
        tails = tail[seq // TM_FF - 2::seq // TM_FF, :, V7X_SUBLANES - 2:, :]
        outs[3].append(jnp.swapaxes(tails, 1, 2).reshape(batch, 2, D_FF))

    st = [jnp.stack(o) for o in outs]
    return (hp.reshape(batch, seq, D_MODEL), hs.reshape(dec, 1, D_MODEL),
            _from_planes(st[0]), _from_planes(st[1]), st[2], st[3],
            _from_planes(st[4]), _from_planes(st[5]), st[6], st[7])
```

```python
import functools

import jax
import jax.numpy as jnp
from jax import lax
from jax.experimental import pallas as pl
from jax.experimental.pallas import tpu as pltpu

D_MODEL = 2048
DEPTH = 2
PAST_LEN = 16384
ATTN_WIDTH = 1024
GM_WIDTH = 1024
HEAD_DIM = 64
N_HEADS = 16
N_KV_HEADS = 4
GQA_GROUP = N_HEADS // N_KV_HEADS
WINDOW = 128
ROPE_THETA = 10000.0
CHUNK = 128
GM_HEADS = 8
D_FF = 5632
KV_WIDTH = N_KV_HEADS * HEAD_DIM
IN_COLS = ATTN_WIDTH + 2 * KV_WIDTH + 2 * GM_WIDTH
EPS = 1e-6
NEG = -1e30
LOG2_E = 1.4426950408889634

Q0, K0, V0, U0, G0 = 0, ATTN_WIDTH, ATTN_WIDTH + KV_WIDTH, ATTN_WIDTH + 2 * KV_WIDTH, ATTN_WIDTH + 2 * KV_WIDTH + GM_WIDTH

V7X_LANES = 128
V7X_SUBLANES = 8
V7X_VMEM_LIMIT_BYTES = 56 * 1024 * 1024

TM = 512
TF = 512
TM_FF = 1024
TF_HEAD = 256
TQ = 1024
TN_IN = 896
TN_OUT = 1024
MOD_TN = 1024
MOD_ROWS = 48
FFN_SUBBLOCKS = 4
MIX_OUT_SUBBLOCKS = 2
S_ATTN_STEPS = 2

BF16 = jnp.bfloat16
F32 = jnp.float32


def _params(*semantics):
    return pltpu.CompilerParams(dimension_semantics=semantics, vmem_limit_bytes=V7X_VMEM_LIMIT_BYTES)


def _dot(a, b):
    return jnp.dot(a, b, preferred_element_type=F32)


def _dot_nt(a, b):
    return lax.dot_general(a, b, (((1,), (1,)), ((), ())), preferred_element_type=F32)


def _rms(x, g):
    return x * lax.rsqrt(jnp.mean(x * x, axis=-1, keepdims=True) + EPS) * g


def _resident(shape):
    zeros = (0,) * len(shape)
    return pl.BlockSpec(shape, lambda *_: zeros, pipeline_mode=pl.Buffered(1))


def _iota(shape, dim):
    return lax.broadcasted_iota(jnp.int32, shape, dim)


def _mod_columns(c_ref, w_ref, b_ref, o_ref):
    s = jax.nn.silu(c_ref[...]).astype(BF16)
    o_ref[...] = _dot(s, w_ref[...].astype(BF16)) + b_ref[...]


def _mod_rider_specs(layer, first_col, width, step_of):
    assert first_col % width == 0
    first = first_col // width
    in_specs = [
        pl.BlockSpec((MOD_ROWS, D_MODEL), lambda *g: (0, 0)),
        pl.BlockSpec((None, D_MODEL, width), lambda *g: (layer, 0, first + step_of(*g))),
        pl.BlockSpec((None, 1, width), lambda *g: (layer, 0, first + step_of(*g))),
    ]
    return in_specs, pl.BlockSpec((MOD_ROWS, width), lambda *g: (0, step_of(*g)))


def _modulation_head(c_all, w_ada, b_ada3, n_cols):
    in_specs, out_spec = _mod_rider_specs(0, 0, MOD_TN, lambda j: j)
    return pl.pallas_call(
        _mod_columns,
        out_shape=jax.ShapeDtypeStruct((MOD_ROWS, n_cols), F32),
        grid=(n_cols // MOD_TN,),
        in_specs=in_specs,
        out_specs=out_spec,
        compiler_params=_params("arbitrary"),
        name="adaln_modulation",
    )(c_all, w_ada, b_ada3)


def _rope_cols(z, cos, sin_signed):
    lane = _iota((1, V7X_LANES), 1)
    first_half = jnp.bitwise_and(lane, HEAD_DIM - 1) < HEAD_DIM // 2
    outs = []
    for c in range(z.shape[1] // V7X_LANES):
        zc = z[:, c * V7X_LANES:(c + 1) * V7X_LANES]
        partner = jnp.where(first_half,
                            pltpu.roll(zc, V7X_LANES - HEAD_DIM // 2, axis=1),
                            pltpu.roll(zc, HEAD_DIM // 2, axis=1))
        outs.append(zc * cos + partner * sin_signed)
    return outs


def _swap_halves(x):
    return pltpu.roll(x, HEAD_DIM, axis=1)


def _pair_q_blocks(nat):
    low = _iota((1, V7X_LANES), 1) < HEAD_DIM
    out = []
    for c in range(len(nat)):
        p, j = divmod(c, 4)
        a, b = nat[p * 4 + j // 2], nat[p * 4 + 2 + j // 2]
        out.append(jnp.where(low, a, _swap_halves(b)) if j % 2 == 0 else jnp.where(low, _swap_halves(a), b))
    return out


def _post_project(zq, zk, zu, zg, cos, sin_signed, gain, pair_q, q_scale):
    q_blocks = _rope_cols(zq, cos, sin_signed)
    if pair_q:
        q_blocks = _pair_q_blocks(q_blocks)
    q_blocks = [qb * q_scale for qb in q_blocks]
    k_blocks = _rope_cols(zk, cos, sin_signed)
    u = jax.nn.gelu(zu)
    gvn = _rms(jax.nn.gelu(zg), gain)
    return q_blocks, k_blocks, u, gvn


def _store_cols(ref, blocks):
    for c, blk in enumerate(blocks):
        ref[:, c * V7X_LANES:(c + 1) * V7X_LANES] = blk.astype(ref.dtype)


def _mix_in_kernel(x_ref, mod_ref, g1_ref, w_ref, cos_ref, sin_ref, gain_ref, ws_ref, bst_ref, *rest, mod_rider):
    if mod_rider:
        c_ref, wa_ref, ba_ref, q_ref, k_ref, v_ref, gm_ref, gvl_ref, mo_ref = rest
        _mod_columns(c_ref, wa_ref, ba_ref, mo_ref)
    else:
        q_ref, k_ref, v_ref, gm_ref, gvl_ref = rest
    tm = x_ref.shape[0]
    h = (_rms(x_ref[...], g1_ref[...]) * (1.0 + mod_ref[1:2]) + mod_ref[0:1]).astype(BF16)
    q_blocks, k_blocks, u, gvn = _post_project(
        _dot(h, w_ref[:, Q0:K0]), _dot(h, w_ref[:, K0:V0]), _dot(h, w_ref[:, U0:G0]), _dot(h, w_ref[:, G0:IN_COLS]),
        cos_ref[...], sin_ref[...], gain_ref[...], pair_q=True, q_scale=HEAD_DIM ** -0.5 * LOG2_E)
    _store_cols(q_ref, q_blocks)
    _store_cols(k_ref, k_blocks)
    v_ref[...] = _dot(h, w_ref[:, V0:U0])
    gvl_ref[...] = gvn[tm - CHUNK:, :]
    gvb = gvn.astype(BF16)
    causal = _iota((CHUNK, CHUNK), 1) <= _iota((CHUNK, CHUNK), 0)
    for hh in range(GM_HEADS):
        wm = jnp.where(causal, ws_ref[hh], 0.0).astype(BF16)
        bias = bst_ref[:, hh:hh + 1]
        cs = slice(hh * CHUNK, (hh + 1) * CHUNK)
        for c in range(tm // CHUNK):
            rs = slice(c * CHUNK, (c + 1) * CHUNK)
            s = _dot(wm, gvb[rs, cs]) + bias
            gm_ref[rs, cs] = (u[rs, cs] * s).astype(BF16)


def _mix_in(x, mod, g1, w_in, cos, sin_signed, gain, ws, bst, seq, mod_rider=None):
    tokens = x.shape[0]
    tiles_per_seq = seq // TM
    batch = tokens // seq
    steps = tokens // TM
    row_tile = lambda width: pl.BlockSpec((TM, width), lambda i: (i, 0))
    rider_in, rider_out, rider_shape, rider_args = [], (), (), ()
    if mod_rider is not None:
        c_all, w_ada, b_ada3, layer, first_col, n_cols = mod_rider
        rider_in, out_spec = _mod_rider_specs(layer, first_col, n_cols // steps, lambda i: i)
        rider_out, rider_shape = (out_spec,), (jax.ShapeDtypeStruct((MOD_ROWS, n_cols), F32),)
        rider_args = (c_all, w_ada, b_ada3)
    return pl.pallas_call(
        functools.partial(_mix_in_kernel, mod_rider=mod_rider is not None),
        out_shape=(
            jax.ShapeDtypeStruct((tokens, ATTN_WIDTH), BF16),
            jax.ShapeDtypeStruct((tokens, KV_WIDTH), F32),
            jax.ShapeDtypeStruct((tokens, KV_WIDTH), F32),
            jax.ShapeDtypeStruct((tokens, GM_WIDTH), BF16),
            jax.ShapeDtypeStruct((batch, CHUNK, GM_WIDTH), F32),
        ) + rider_shape,
        grid=(steps,),
        in_specs=[
            row_tile(D_MODEL),
            pl.BlockSpec((None, mod.shape[1], D_MODEL), lambda i: (i // tiles_per_seq, 0, 0)),
            _resident((1, D_MODEL)),
            _resident((D_MODEL, IN_COLS)),
            pl.BlockSpec((TM, V7X_LANES), lambda i: (i % tiles_per_seq, 0)),
            pl.BlockSpec((TM, V7X_LANES), lambda i: (i % tiles_per_seq, 0)),
            _resident((1, GM_WIDTH)),
            _resident((GM_HEADS, CHUNK, CHUNK)),
            _resident((CHUNK, GM_HEADS)),
        ] + rider_in,
        out_specs=(
            row_tile(ATTN_WIDTH), row_tile(KV_WIDTH), row_tile(KV_WIDTH), row_tile(GM_WIDTH),
            pl.BlockSpec((None, CHUNK, GM_WIDTH), lambda i: (i // tiles_per_seq, 0, 0)),
        ) + rider_out,
        compiler_params=_params("arbitrary"),
        name="mix_in",
    )(x, mod, g1, w_in, cos, sin_signed, gain, ws, bst, *rider_args)


def _attn_block(n, sink_ref, q_blk, k_prev, k_cur, v_prev, v_cur, out_blk):
    row = _iota((WINDOW, 2 * WINDOW), 0)
    key = _iota((WINDOW, 2 * WINDOW), 1) - WINDOW
    valid = (key <= row) & (key >= jnp.maximum(row - WINDOW, -n * WINDOW))
    low = _iota((WINDOW, V7X_LANES), 1) < HEAD_DIM
    cols_per_pair = ATTN_WIDTH // V7X_LANES // 2
    for p in range(2):
        ls = slice(p * V7X_LANES, (p + 1) * V7X_LANES)
        kcat = jnp.concatenate([k_prev[:, ls], k_cur[:, ls]], axis=0).astype(BF16)
        vcat = jnp.concatenate([v_prev[:, ls], v_cur[:, ls]], axis=0).astype(BF16)
        blocks = []
        for cc in range(cols_per_pair):
            c = p * cols_per_pair + cc
            qcol = q_blk[:, c * V7X_LANES:(c + 1) * V7X_LANES]
            zero = jnp.zeros_like(qcol)
            blocks += [jnp.where(low, qcol, zero), jnp.where(low, zero, qcol)]
        s = _dot_nt(jnp.concatenate(blocks, axis=0), kcat)
        probs, inv = [], []
        for j in range(2 * cols_per_pair):
            cc, e = divmod(j, 2)
            sj = jnp.where(valid, s[j * WINDOW:(j + 1) * WINDOW], NEG)
            sk = sink_ref[p * 2 * GQA_GROUP + e * GQA_GROUP + cc] * LOG2_E
            m = jnp.maximum(jnp.max(sj, axis=-1, keepdims=True), sk)
            pj = jnp.exp2(sj - m)
            inv.append(1.0 / (jnp.sum(pj, axis=-1, keepdims=True) + jnp.exp2(sk - m)))
            probs.append(pj.astype(BF16))
        o = _dot(jnp.concatenate(probs, axis=0), vcat)
        for cc in range(cols_per_pair):
            c = p * cols_per_pair + cc
            even = o[(2 * cc) * WINDOW:(2 * cc + 1) * WINDOW] * inv[2 * cc]
            odd = o[(2 * cc + 1) * WINDOW:(2 * cc + 2) * WINDOW] * inv[2 * cc + 1]
            out_blk[:, c * V7X_LANES:(c + 1) * V7X_LANES] = jnp.where(low, even, odd).astype(BF16)


def _attn_kernel(sink_ref, q_ref, kp_ref, kc_ref, vp_ref, vc_ref, *rest, mod_rider):
    if mod_rider:
        c_ref, wa_ref, ba_ref, o_ref, kt_ref, vt_ref, mo_ref = rest
        _mod_columns(c_ref, wa_ref, ba_ref, mo_ref)
    else:
        o_ref, kt_ref, vt_ref = rest
    step = pl.program_id(1)
    blocks = q_ref.shape[0] // WINDOW
    for t in range(blocks):
        rows = slice(t * WINDOW, (t + 1) * WINDOW)
        prev = slice((t - 1) * WINDOW, t * WINDOW)
        k_prev = kp_ref[...] if t == 0 else kc_ref[prev]
        v_prev = vp_ref[...] if t == 0 else vc_ref[prev]
        _attn_block(step * blocks + t, sink_ref, q_ref.at[rows], k_prev, kc_ref[rows], v_prev, vc_ref[rows], o_ref.at[rows])

    @pl.when(step == pl.num_programs(1) - 1)
    def _():
        last = slice((blocks - 1) * WINDOW, blocks * WINDOW)
        kt_ref[...] = kc_ref[last].T
        vt_ref[...] = vc_ref[last].T


def _attention(sinks, q, k, v, seq, mod_rider=None):
    tokens = q.shape[0]
    batch = tokens // seq
    steps = seq // TQ
    per_step = TQ // WINDOW
    cur = lambda b, n: (b * steps + n, 0)
    prev = lambda b, n: (jnp.maximum((b * steps + n) * per_step - 1, 0), 0)
    state = pl.BlockSpec((None, KV_WIDTH, WINDOW), lambda b, n: (b, 0, 0))
    rider_in, rider_out, rider_shape, rider_args = [], (), (), ()
    if mod_rider is not None:
        c_all, w_ada, b_ada3, layer, first_col, n_cols = mod_rider
        rider_in, out_spec = _mod_rider_specs(layer, first_col, n_cols // (batch * steps), lambda b, n: b * steps + n)
        rider_out, rider_shape = (out_spec,), (jax.ShapeDtypeStruct((MOD_ROWS, n_cols), F32),)
        rider_args = (c_all, w_ada, b_ada3)
    return pl.pallas_call(
        functools.partial(_attn_kernel, mod_rider=mod_rider is not None),
        out_shape=(
            jax.ShapeDtypeStruct((tokens, ATTN_WIDTH), BF16),
            jax.ShapeDtypeStruct((batch, KV_WIDTH, WINDOW), F32),
            jax.ShapeDtypeStruct((batch, KV_WIDTH, WINDOW), F32),
        ) + rider_shape,
        grid=(batch, steps),
        in_specs=[
            pl.BlockSpec(memory_space=pltpu.SMEM),
            pl.BlockSpec((TQ, ATTN_WIDTH), cur),
            pl.BlockSpec((WINDOW, KV_WIDTH), prev),
            pl.BlockSpec((TQ, KV_WIDTH), cur),
            pl.BlockSpec((WINDOW, KV_WIDTH), prev),
            pl.BlockSpec((TQ, KV_WIDTH), cur),
        ] + rider_in,
        out_specs=(pl.BlockSpec((TQ, ATTN_WIDTH), cur), state, state) + rider_out,
        compiler_params=_params("arbitrary", "arbitrary"),
        name="swa_attention",
    )(sinks, q, k, k, v, v, *rider_args)


def _mix_out_kernel(x_ref, mod_ref, g2_ref, a_ref, gm_ref, wa_ref, wb_ref, o_ref, h2_ref):
    sub = x_ref.shape[0] // MIX_OUT_SUBBLOCKS
    for r in range(MIX_OUT_SUBBLOCKS):
        rs = slice(r * sub, (r + 1) * sub)
        mix = _dot(a_ref[rs], wa_ref[...]) + _dot(gm_ref[rs], wb_ref[...])
        x1 = x_ref[rs] + mod_ref[2:3] * mix
        o_ref[rs] = x1
        h2_ref[rs] = (_rms(x1, g2_ref[...]) * (1.0 + mod_ref[4:5]) + mod_ref[3:4]).astype(BF16)


def _mix_out(x, mod, g2, attn, gm, w_out, seq):
    tokens = x.shape[0]
    tiles_per_seq = seq // TM
    row_tile = lambda width: pl.BlockSpec((TM, width), lambda i: (i, 0))
    half = lambda k: pl.BlockSpec((ATTN_WIDTH, D_MODEL), lambda i: (k, 0), pipeline_mode=pl.Buffered(1))
    return pl.pallas_call(
        _mix_out_kernel,
        out_shape=(jax.ShapeDtypeStruct((tokens, D_MODEL), F32), jax.ShapeDtypeStruct((tokens, D_MODEL), BF16)),
        grid=(tokens // TM,),
        in_specs=[
            row_tile(D_MODEL),
            pl.BlockSpec((None, 6, D_MODEL), lambda i: (i // tiles_per_seq, 0, 0)),
            _resident((1, D_MODEL)),
            row_tile(ATTN_WIDTH), row_tile(GM_WIDTH),
            half(0), half(1),
        ],
        out_specs=(row_tile(D_MODEL), row_tile(D_MODEL)),
        compiler_params=_params("arbitrary"),
        name="mix_out",
    )(x, mod, g2, attn, gm, w_out, w_out)


def _ffn_tile(h2_ref, ext_ref, acc_ref, wg_parts, wu_parts, wd, cw, cb, riders=None, restart=None, finish=None):
    tm = h2_ref.shape[0]
    sub = tm // FFN_SUBBLOCKS
    width = ext_ref.shape[1] // len(wg_parts)
    ups = []
    rider_a = rider_up = None
    for r in range(FFN_SUBBLOCKS):
        h2 = h2_ref[r * sub:(r + 1) * sub]
        ride = riders is not None and r == FFN_SUBBLOCKS - 1
        if ride:
            h2 = jnp.concatenate([h2, riders[0]], axis=0)
        rows = slice(V7X_SUBLANES + r * sub, V7X_SUBLANES + (r + 1) * sub)
        a_parts = [_dot(h2, wg) for wg in wg_parts]
        up = jnp.concatenate([_dot(h2, wu) for wu in wu_parts], axis=1)
        for c, a in enumerate(a_parts):
            ext_ref[rows, c * width:(c + 1) * width] = a[:sub]
        ups.append(up[:sub])
        if ride:
            rider_a, rider_up = jnp.concatenate([a[sub:] for a in a_parts], axis=1), up[sub:]
    def carried(rows):
        old = acc_ref[rows]
        return old if restart is None else jnp.where(restart, 0.0, old)

    for r in range(FFN_SUBBLOCKS):
        lo = V7X_SUBLANES + r * sub
        conv = (cb + cw[0:1] * ext_ref[lo - 2:lo - 2 + sub]
                + cw[1:2] * ext_ref[lo - 1:lo - 1 + sub] + cw[2:3] * ext_ref[lo:lo + sub])
        y = (jax.nn.silu(conv) * ups[r]).astype(BF16)
        if riders is not None and r == FFN_SUBBLOCKS - 1:
            out = _dot(jnp.concatenate([y, riders[1](rider_a, rider_up)], axis=0), wd)
            acc_ref[r * sub:(r + 1) * sub] = carried(slice(r * sub, (r + 1) * sub)) + out[:sub]
            riders[2][...] += out[sub:]
        else:
            acc_ref[r * sub:(r + 1) * sub] = carried(slice(r * sub, (r + 1) * sub)) + _dot(y, wd)
        if finish is not None:
            finish(slice(r * sub, (r + 1) * sub))


def _ffn_finish_rows(o_ref, acc_ref, rows, gate, gf_ref, final_norm):
    x2 = o_ref[rows] + gate * acc_ref[rows]
    o_ref[rows] = _rms(x2, gf_ref[...]) if final_norm else x2


def _ffn_finish(o_ref, acc_ref, gate, gf_ref, final_norm):
    sub = o_ref.shape[0] // FFN_SUBBLOCKS

    def finish(r, carry):
        rs = pl.ds(pl.multiple_of(r * sub, sub), sub)
        x2 = o_ref[rs, :] + gate * acc_ref[rs, :]
        o_ref[rs, :] = _rms(x2, gf_ref[...]) if final_norm else x2
        return carry

    lax.fori_loop(0, FFN_SUBBLOCKS, finish, 0)


def _ffn_kernel(x_hbm, head_hbm, wg_hbm, wu_hbm, wd_hbm, h2_ref, mod_ref, cw_ref, cb_ref, halo_ref, gf_ref,
                o_hbm, tail_ref, wg_buf, wu_buf, wd_buf, obuf_ref, acc_ref, ext_ref, carry_ref,
                w_sems, x_sem, wb_sem, head_sem, *, first_tile, tiles_per_seq, final_norm):
    i = pl.program_id(0)
    n_tiles = pl.num_programs(0)
    nf = wd_hbm.shape[0] // wd_buf.shape[1]
    per = wg_buf.shape[1]
    tm = h2_ref.shape[0]
    tile = i + first_tile
    rows = pl.ds(pl.multiple_of(tile * tm, tm), tm)
    x_copy = pltpu.make_async_copy(x_hbm.at[rows], obuf_ref, x_sem)
    write_back = pltpu.make_async_copy(obuf_ref, o_hbm.at[rows], wb_sem)
    head_copy = pltpu.make_async_copy(head_hbm, o_hbm.at[pl.ds(0, first_tile * tm)], head_sem)

    def weight_copies(f, slot):
        return (pltpu.make_async_copy(wg_hbm.at[pl.ds(f * per, per)], wg_buf.at[slot], w_sems.at[0, slot]),
                pltpu.make_async_copy(wu_hbm.at[pl.ds(f * per, per)], wu_buf.at[slot], w_sems.at[1, slot]),
                pltpu.make_async_copy(wd_hbm.at[pl.ds(pl.multiple_of(f * wd_buf.shape[1], wd_buf.shape[1]), wd_buf.shape[1])],
                                      wd_buf.at[slot], w_sems.at[2, slot]))

    @pl.when(i == 0)
    def _():
        head_copy.start()
        for c in weight_copies(0, 0):
            c.start()
        acc_ref[...] = jnp.zeros_like(acc_ref)

    seq_start = (tile % tiles_per_seq) == 0

    def column_step(f, carry, finish=None):
        step = i * nf + f
        slot = step % 2
        last = f == nf - 1
        for c in weight_copies(f, slot):
            c.wait()

        @pl.when(jnp.logical_not(last & (i == n_tiles - 1)))
        def _():
            nxt = jnp.where(last, 0, f + 1)
            for c in weight_copies(nxt, 1 - slot):
                c.start()

        fetch = f == nf - 2

        @pl.when(fetch & (i > 0))
        def _():
            write_back.wait()

        @pl.when(fetch)
        def _():
            x_copy.start()

        @pl.when(seq_start)
        def _():
            ext_ref[0:V7X_SUBLANES] = jnp.zeros((V7X_SUBLANES, ext_ref.shape[1]), F32)

        @pl.when(jnp.logical_not(seq_start) & (i == 0))
        def _():
            ext_ref[0:V7X_SUBLANES] = halo_ref[f]

        @pl.when(jnp.logical_not(seq_start) & (i > 0))
        def _():
            ext_ref[0:V7X_SUBLANES] = carry_ref[f]

        _ffn_tile(h2_ref, ext_ref, acc_ref, [wg_buf[slot, c] for c in range(per)], [wu_buf[slot, c] for c in range(per)],
                  wd_buf[slot], cw_ref[f], cb_ref[f], restart=f == 0, finish=finish)
        tail = ext_ref[tm:tm + V7X_SUBLANES]
        carry_ref[f] = tail
        tail_ref[f] = tail
        return carry

    lax.fori_loop(0, nf - 1, column_step, 0)
    x_copy.wait()
    column_step(jnp.int32(nf - 1), 0,
                finish=lambda rws: _ffn_finish_rows(obuf_ref, acc_ref, rws, mod_ref[5:6], gf_ref, final_norm))
    write_back.start()

    @pl.when(i == n_tiles - 1)
    def _():
        write_back.wait()
        head_copy.wait()


def _ffn(x, head_rows, halo, h2, mod, wg, wu, wd, cw, cb, g_final, seq, final_norm):
    tokens = x.shape[0]
    tiles_per_seq = seq // TM_FF
    nf = D_FF // TF
    per = TF // TF_HEAD
    first_tile = 1
    kern = functools.partial(_ffn_kernel, first_tile=first_tile, tiles_per_seq=tiles_per_seq, final_norm=final_norm)
    hbm = pl.BlockSpec(memory_space=pl.ANY)
    return pl.pallas_call(
        kern,
        out_shape=(
            jax.ShapeDtypeStruct((tokens, D_MODEL), F32),
            jax.ShapeDtypeStruct((tokens // TM_FF - first_tile, nf, V7X_SUBLANES, TF), F32),
        ),
        grid=(tokens // TM_FF - first_tile,),
        in_specs=[
            hbm, hbm, hbm, hbm, hbm,
            pl.BlockSpec((TM_FF, D_MODEL), lambda i: (i + first_tile, 0)),
            pl.BlockSpec((None, 6, D_MODEL), lambda i: ((i + first_tile) // tiles_per_seq, 0, 0)),
            _resident((nf, 3, TF)),
            _resident((nf, 1, TF)),
            _resident((nf, V7X_SUBLANES, TF)),
            _resident((1, D_MODEL)),
        ],
        out_specs=(
            hbm,
            pl.BlockSpec((None, nf, V7X_SUBLANES, TF), lambda i: (i, 0, 0, 0)),
        ),
        scratch_shapes=[
            pltpu.VMEM((2, per, D_MODEL, TF_HEAD), BF16),
            pltpu.VMEM((2, per, D_MODEL, TF_HEAD), BF16),
            pltpu.VMEM((2, TF, D_MODEL), BF16),
            pltpu.VMEM((TM_FF, D_MODEL), F32),
            pltpu.VMEM((TM_FF, D_MODEL), F32),
            pltpu.VMEM((TM_FF + V7X_SUBLANES, TF), F32),
            pltpu.VMEM((nf, V7X_SUBLANES, TF), F32),
            pltpu.SemaphoreType.DMA((3, 2)),
            pltpu.SemaphoreType.DMA(()),
            pltpu.SemaphoreType.DMA(()),
            pltpu.SemaphoreType.DMA(()),
        ],
        compiler_params=_params("arbitrary"),
        name="conv_ffn",
    )(x, head_rows, wg, wu, wd, h2, mod, cw, cb, jnp.swapaxes(halo.reshape(V7X_SUBLANES, nf, TF), 0, 1), g_final)


def _ffn_head_kernel(xs_ref, mods_ref, g2_ref, wg_ref, wu_ref, wd_ref, cw_ref, cb_ref, p0_ref, p1_ref, gf_ref,
                     x_hbm, h2_ref, mod_ref,
                     os_ref, a_ref, wgb_ref, wub_ref, wdb_ref, o_ref, halo_ref,
                     h2s_ref, accs_ref, acc_ref, ext_ref, x_sem, *, final_norm):
    f = pl.program_id(0)
    tm = h2_ref.shape[0]
    last = f == pl.num_programs(0) - 1
    x_copy = pltpu.make_async_copy(x_hbm.at[pl.ds(0, tm)], o_ref, x_sem)

    @pl.when(last)
    def _():
        x_copy.start()

    @pl.when(f == 0)
    def _():
        h2s_ref[...] = (_rms(xs_ref[...], g2_ref[...]) * (1.0 + mods_ref[4]) + mods_ref[3]).astype(BF16)
        accs_ref[...] = jnp.zeros_like(accs_ref)
        acc_ref[...] = jnp.zeros_like(acc_ref)

    wg, wu, wd = wg_ref[...].astype(BF16), wu_ref[...].astype(BF16), wd_ref[...].astype(BF16)
    wgb_ref[...] = wg
    wub_ref[...] = wu
    wdb_ref[...] = wd
    cw, cb = cw_ref[f], cb_ref[f]

    width = ext_ref.shape[1]
    cols = pl.ds(pl.multiple_of(f * width, width), width)

    def sample_gate(a, up):
        a_ref[:, cols] = a
        conv = cb + cw[0:1] * p0_ref[:, cols] + cw[1:2] * p1_ref[:, cols] + cw[2:3] * a
        return (jax.nn.silu(conv) * up).astype(BF16)

    ext_ref[0:V7X_SUBLANES] = jnp.zeros((V7X_SUBLANES, ext_ref.shape[1]), F32)
    _ffn_tile(h2_ref, ext_ref, acc_ref, [wg], [wu], wd, cw, cb, riders=(h2s_ref[...], sample_gate, accs_ref))
    halo_ref[:, cols] = ext_ref[tm:tm + V7X_SUBLANES]

    @pl.when(last)
    def _():
        x2 = xs_ref[...] + mods_ref[5] * accs_ref[...]
        os_ref[...] = _rms(x2, gf_ref[...]) if final_norm else x2
        x_copy.wait()
        _ffn_finish(o_ref, acc_ref, mod_ref[5:6], gf_ref, final_norm)


def _ffn_head(layer, xs, mods, g2, w_gate, w_up, w_down, cw, cb, p0, p1, g_final, x, h2, mod, final_norm):
    rows = xs.shape[0]
    tokens = x.shape[0]
    nfh = D_FF // TF_HEAD
    full = lambda shape: pl.BlockSpec(shape, lambda f: (0,) * len(shape))
    tile_copy = pl.BlockSpec((None, D_MODEL, TF_HEAD), lambda f: (f, 0, 0))
    return pl.pallas_call(
        functools.partial(_ffn_head_kernel, final_norm=final_norm),
        out_shape=(
            jax.ShapeDtypeStruct((rows, D_MODEL), F32), jax.ShapeDtypeStruct((rows, D_FF), F32),
            jax.ShapeDtypeStruct((nfh, D_MODEL, TF_HEAD), BF16), jax.ShapeDtypeStruct((nfh, D_MODEL, TF_HEAD), BF16),
            jax.ShapeDtypeStruct((D_FF, D_MODEL), BF16),
            jax.ShapeDtypeStruct((TM_FF, D_MODEL), F32),
            jax.ShapeDtypeStruct((V7X_SUBLANES, D_FF), F32),
        ),
        grid=(nfh,),
        in_specs=[
            full((rows, D_MODEL)), full((6, rows, D_MODEL)), full((1, D_MODEL)),
            pl.BlockSpec((None, D_MODEL, TF_HEAD), lambda f: (layer, 0, f)),
            pl.BlockSpec((None, D_MODEL, TF_HEAD), lambda f: (layer, 0, f)),
            pl.BlockSpec((None, TF_HEAD, D_MODEL), lambda f: (layer, f, 0)),
            full((nfh, 3, TF_HEAD)), full((nfh, 1, TF_HEAD)),
            full((rows, D_FF)), full((rows, D_FF)),
            full((1, D_MODEL)),
            pl.BlockSpec(memory_space=pl.ANY),
            pl.BlockSpec((TM_FF, D_MODEL), lambda f: (0, 0), pipeline_mode=pl.Buffered(1)),
            pl.BlockSpec((None, 6, D_MODEL), lambda f: (0, 0, 0)),
        ],
        out_specs=(
            full((rows, D_MODEL)), full((rows, D_FF)),
            tile_copy, tile_copy,
            pl.BlockSpec((TF_HEAD, D_MODEL), lambda f: (f, 0)),
            pl.BlockSpec((TM_FF, D_MODEL), lambda f: (0, 0)),
            full((V7X_SUBLANES, D_FF)),
        ),
        scratch_shapes=[
            pltpu.VMEM((rows, D_MODEL), BF16), pltpu.VMEM((rows, D_MODEL), F32),
            pltpu.VMEM((TM_FF, D_MODEL), F32),
            pltpu.VMEM((TM_FF + V7X_SUBLANES, TF_HEAD), F32),
            pltpu.SemaphoreType.DMA(()),
        ],
        compiler_params=_params("arbitrary"),
        name="conv_ffn_head",
    )(xs, mods, g2, w_gate, w_up, w_down, cw, cb, p0, p1, g_final, x, h2, mod)


def _s_in_proj_kernel(x_ref, mod_ref, g1_ref, w_ref, z_ref, wb_ref, h_ref):
    @pl.when(pl.program_id(0) == 0)
    def _():
        h_ref[...] = (_rms(x_ref[...], g1_ref[...]) * (1.0 + mod_ref[1]) + mod_ref[0]).astype(BF16)

    w = w_ref[...].astype(BF16)
    wb_ref[...] = w
    z_ref[...] = _dot(h_ref[...], w)


def _s_in_proj(layer, x, mod, g1, w_in):
    rows = x.shape[0]
    full = lambda shape: pl.BlockSpec(shape, lambda j: (0,) * len(shape))
    return pl.pallas_call(
        _s_in_proj_kernel,
        out_shape=(jax.ShapeDtypeStruct((rows, IN_COLS), F32), jax.ShapeDtypeStruct((D_MODEL, IN_COLS), BF16)),
        grid=(IN_COLS // TN_IN,),
        in_specs=[
            full((rows, D_MODEL)), full(mod.shape), full((1, D_MODEL)),
            pl.BlockSpec((None, D_MODEL, TN_IN), lambda j: (layer, 0, j)),
        ],
        out_specs=(pl.BlockSpec((rows, TN_IN), lambda j: (0, j)), pl.BlockSpec((D_MODEL, TN_IN), lambda j: (0, j))),
        scratch_shapes=[pltpu.VMEM((rows, D_MODEL), BF16)],
        compiler_params=_params("arbitrary"),
        name="sample_in_proj",
    )(x, mod, g1, w_in)


def _s_attn_kernel(sink_ref, z_ref, cos_ref, sin_ref, gain_ref, ws0_ref, bs0_ref, kt_ref, vt_ref,
                   attn_ref, gm_ref, gvn_ref, ktn_ref, vtn_ref):
    nseq = z_ref.shape[0]
    q_blocks, k_blocks, u, gvn = _post_project(
        z_ref[:, Q0:K0], z_ref[:, K0:V0], z_ref[:, U0:G0], z_ref[:, G0:IN_COLS],
        cos_ref[...], sin_ref[...], gain_ref[...], pair_q=False, q_scale=HEAD_DIM ** -0.5)
    v_new = z_ref[:, V0:U0]
    k_new = jnp.concatenate(k_blocks, axis=1)
    gvn_ref[...] = gvn
    gm_ref[...] = (u * (ws0_ref[...].astype(BF16).astype(F32) * gvn.astype(BF16).astype(F32) + bs0_ref[...])).astype(BF16)

    rows = GQA_GROUP * nseq
    keys = nseq * WINDOW
    seq_of_row = jnp.bitwise_and(_iota((rows, 1), 0), nseq - 1)
    head_of_row = lax.shift_right_logical(_iota((rows, 1), 0), nseq.bit_length() - 1)
    own_keys = lax.shift_right_logical(_iota((rows, keys), 1), WINDOW.bit_length() - 1) == seq_of_row
    own_new = _iota((rows, nseq), 1) == seq_of_row
    pieces = []
    for g in range(N_KV_HEADS):
        gs = slice(g * HEAD_DIM, (g + 1) * HEAD_DIM)
        heads = [g * GQA_GROUP + r for r in range(GQA_GROUP)]
        qg = jnp.concatenate(
            [q_blocks[h // 2][:, (h % 2) * HEAD_DIM:(h % 2 + 1) * HEAD_DIM] for h in heads], axis=0).astype(BF16)
        ktg = jnp.concatenate([kt_ref[b, gs, :] for b in range(nseq)], axis=1).astype(BF16)
        vtg = jnp.concatenate([vt_ref[b, gs, :] for b in range(nseq)], axis=1).astype(BF16)
        s = jnp.where(own_keys, _dot(qg, ktg), NEG)
        sn = jnp.where(own_new, _dot_nt(qg, k_new[:, gs].astype(BF16)), NEG)
        sk = jnp.full((rows, 1), sink_ref[heads[0]], F32)
        for r in range(1, GQA_GROUP):
            sk = jnp.where(head_of_row == r, sink_ref[heads[r]], sk)
        m = jnp.maximum(jnp.maximum(jnp.max(s, axis=-1, keepdims=True), jnp.max(sn, axis=-1, keepdims=True)), sk)
        p = jnp.exp(s - m)
        pn = jnp.exp(sn - m)
        denom = jnp.sum(p, axis=-1, keepdims=True) + jnp.sum(pn, axis=-1, keepdims=True) + jnp.exp(sk - m)
        o = (_dot_nt(p.astype(BF16), vtg) + _dot(pn.astype(BF16), v_new[:, gs].astype(BF16))) / denom
        pieces += [o[r * nseq:(r + 1) * nseq] for r in range(GQA_GROUP)]
    attn_ref[...] = jnp.concatenate(pieces, axis=1).astype(BF16)

    pad = jnp.zeros((V7X_LANES - nseq, KV_WIDTH), F32)
    knt = jnp.concatenate([k_new, pad], axis=0).T
    vnt = jnp.concatenate([v_new, pad], axis=0).T
    newest = _iota((1, WINDOW), 1) == WINDOW - 1
    for b in range(nseq):
        ktn_ref[b] = jnp.where(newest, knt[:, b:b + 1], pltpu.roll(kt_ref[b], WINDOW - 1, axis=1))
        vtn_ref[b] = jnp.where(newest, vnt[:, b:b + 1], pltpu.roll(vt_ref[b], WINDOW - 1, axis=1))


def _s_attention(layer, sinks, z, cos, sin_signed, gain, ws0, bs0, kt_all, vt_all):
    rows = z.shape[0]
    grp = rows // S_ATTN_STEPS
    full = lambda shape: pl.BlockSpec(shape, lambda i: (0,) * len(shape))
    by_seq = lambda *tail: pl.BlockSpec((grp,) + tail, lambda i: (i,) + (0,) * len(tail))
    cache = pl.BlockSpec((None, grp, KV_WIDTH, WINDOW), lambda i: (layer, i, 0, 0))
    return pl.pallas_call(
        _s_attn_kernel,
        out_shape=(
            jax.ShapeDtypeStruct((rows, ATTN_WIDTH), BF16),
            jax.ShapeDtypeStruct((rows, GM_WIDTH), BF16),
            jax.ShapeDtypeStruct((rows, GM_WIDTH), F32),
            jax.ShapeDtypeStruct((rows, KV_WIDTH, WINDOW), F32),
            jax.ShapeDtypeStruct((rows, KV_WIDTH, WINDOW), F32),
        ),
        grid=(S_ATTN_STEPS,),
        in_specs=[
            pl.BlockSpec(memory_space=pltpu.SMEM),
            by_seq(IN_COLS), full((1, V7X_LANES)), full((1, V7X_LANES)),
            full((1, GM_WIDTH)), full((1, GM_WIDTH)), full((1, GM_WIDTH)),
            cache, cache,
        ],
        out_specs=(by_seq(ATTN_WIDTH), by_seq(GM_WIDTH), by_seq(GM_WIDTH),
                   by_seq(KV_WIDTH, WINDOW), by_seq(KV_WIDTH, WINDOW)),
        compiler_params=_params("arbitrary"),
        name="sample_attention",
    )(sinks, z, cos, sin_signed, gain, ws0, bs0, kt_all, vt_all)


def _paired_head(slot):
    p, rest = divmod(slot, 2 * GQA_GROUP)
    j, i = divmod(rest, 2)
    return p * 2 * GQA_GROUP + i * GQA_GROUP + j


def _s_mix_out_kernel(x_ref, gate_ref, a_ref, gm_ref, w_ref, o_ref, wb_ref):
    w = w_ref[...].astype(BF16)
    for slot in range(N_HEADS):
        head = _paired_head(slot)
        wb_ref[slot * HEAD_DIM:(slot + 1) * HEAD_DIM] = w[head * HEAD_DIM:(head + 1) * HEAD_DIM]
    wb_ref[ATTN_WIDTH:] = w[ATTN_WIDTH:]
    mix = _dot(a_ref[...], w[:ATTN_WIDTH]) + _dot(gm_ref[...], w[ATTN_WIDTH:])
    o_ref[...] = x_ref[...] + gate_ref[...] * mix


def _s_mix_out(layer, x, gate, attn, gm, w_out):
    rows = x.shape[0]
    full = lambda shape: pl.BlockSpec(shape, lambda j: (0,) * len(shape))
    col_tile = lambda r: pl.BlockSpec((r, TN_OUT), lambda j: (0, j))
    return pl.pallas_call(
        _s_mix_out_kernel,
        out_shape=(jax.ShapeDtypeStruct((rows, D_MODEL), F32), jax.ShapeDtypeStruct((D_MODEL, D_MODEL), BF16)),
        grid=(D_MODEL // TN_OUT,),
        in_specs=[
            col_tile(rows), col_tile(rows), full((rows, ATTN_WIDTH)), full((rows, GM_WIDTH)),
            pl.BlockSpec((None, D_MODEL, TN_OUT), lambda j: (layer, 0, j)),
        ],
        out_specs=(col_tile(rows), col_tile(D_MODEL)),
        compiler_params=_params("arbitrary"),
        name="sample_mix_out",
    )(x, gate, attn, gm, w_out)


def _rope_tables(pos):
    half = HEAD_DIM // 2
    inv = ROPE_THETA ** (-jnp.arange(half, dtype=jnp.float32) / half)
    ang = pos.astype(jnp.float32)[:, None] * inv[None, :]
    reps = V7X_LANES // half
    sign = jnp.where((jnp.arange(V7X_LANES) % HEAD_DIM) < half, -1.0, 1.0).astype(F32)
    return jnp.tile(jnp.cos(ang), (1, reps)), jnp.tile(jnp.sin(ang), (1, reps)) * sign[None, :]


def _to_planes(window_buf):
    lead = window_buf.shape[:-3]
    return jnp.moveaxis(window_buf, -3, -1).reshape(lead + (KV_WIDTH, WINDOW))


def _from_planes(planes):
    lead = planes.shape[:-2]
    return jnp.moveaxis(planes.reshape(lead + (N_KV_HEADS, HEAD_DIM, WINDOW)), -1, -3)


def kernel(x_prompt, x_sample, cache_k, cache_v, state_conv, c_prompt, c_sample, w_ada, b_ada, g_norm1, g_norm2,
           w_in, gm_gain, gm_ws, gm_bs, sinks, w_out, w_gate, w_up, conv_w, conv_b, w_down, g_final):
    batch, seq, _ = x_prompt.shape
    dec = x_sample.shape[0]
    nf = D_FF // TF
    assert x_sample.shape[1] == 1 and seq % TM == 0 and seq % TQ == 0 and TM % CHUNK == 0 and D_FF % TF == 0
    assert seq // TM_FF >= 2 and seq % TM_FF == 0 and TM_FF % FFN_SUBBLOCKS == 0 and TF % TF_HEAD == 0
    assert dec & (dec - 1) == 0 and dec <= V7X_LANES and dec % (2 * V7X_SUBLANES * S_ATTN_STEPS) == 0

    bst = jnp.swapaxes(gm_bs, 1, 2)
    ws0 = jnp.repeat(gm_ws[:, :, 0, 0], GM_WIDTH // GM_HEADS, axis=1)[:, None, :]
    bs0 = jnp.repeat(gm_bs[:, :, 0], GM_WIDTH // GM_HEADS, axis=1)[:, None, :]
    cw_tiles = jnp.swapaxes(conv_w.reshape(DEPTH, 3, nf, TF), 1, 2)
    cb_tiles = conv_b.reshape(DEPTH, nf, 1, TF)
    cw_head = jnp.swapaxes(conv_w.reshape(DEPTH, 3, D_FF // TF_HEAD, TF_HEAD), 1, 2)
    cb_head = conv_b.reshape(DEPTH, D_FF // TF_HEAD, 1, TF_HEAD)

    cos_p, sin_p = _rope_tables(jnp.arange(seq, dtype=jnp.int32))
    cos_s, sin_s = _rope_tables(PAST_LEN + jnp.arange(1, dtype=jnp.int32))

    c_all = jnp.concatenate([c_prompt, c_sample, jnp.zeros((MOD_ROWS - batch - dec, D_MODEL), F32)], axis=0)
    b_ada3 = b_ada.reshape(DEPTH, 1, 6 * D_MODEL)
    prompt_mod = lambda slab: slab[:batch].reshape(batch, -1, D_MODEL)
    sample_mod = lambda slab: jnp.swapaxes(slab[batch:batch + dec].reshape(dec, -1, D_MODEL), 0, 1)
    mod_first = _modulation_head(c_all, w_ada, b_ada3, 2 * D_MODEL)
    mod_next = None

    kt_all, vt_all = _to_planes(cache_k), _to_planes(cache_v)

    hp = x_prompt.reshape(batch * seq, D_MODEL)
    hs = x_sample.reshape(dec, D_MODEL)
    g_fin = g_final.reshape(1, D_MODEL)
    outs = [[] for _ in range(8)]
    for l in range(DEPTH):
        last = l == DEPTH - 1
        g1 = g_norm1[l].reshape(1, D_MODEL)
        g2 = g_norm2[l].reshape(1, D_MODEL)
        gain = gm_gain[l].reshape(1, GM_WIDTH)
        if l == 0:
            mod_p, mod_s = prompt_mod(mod_first), sample_mod(mod_first)
            in_rider = (c_all, w_ada, b_ada3, 0, 2 * D_MODEL, 4 * D_MODEL)
        else:
            mod_p, mod_s = prompt_mod(mod_next), sample_mod(mod_next)
            in_rider = None
        attn_rider = (c_all, w_ada, b_ada3, l + 1, 0, 6 * D_MODEL) if l + 1 < DEPTH else None

        z, w_in_b = _s_in_proj(l, hs, mod_s, g1, w_in)
        attn_s, gm_s, gvn_s, kt_new, vt_new = _s_attention(
            l, sinks[l], z, cos_s, sin_s, gain, ws0[l], bs0[l], kt_all, vt_all)
        mixed = _mix_in(hp, mod_p, g1, w_in_b, cos_p, sin_p, gain, gm_ws[l], bst[l], seq, mod_rider=in_rider)
        q, k, v, gm, gvl = mixed[:5]
        if in_rider is not None:
            slab = jnp.concatenate([mod_first, mixed[5]], axis=1)
            mod_p, mod_s = prompt_mod(slab), sample_mod(slab)
        outs[4].append(kt_new)
        outs[5].append(vt_new)
        outs[6].append(gvn_s[:, None, :])

        hs, w_out_b = _s_mix_out(l, hs, mod_s[2], attn_s, gm_s, w_out)
        attended = _attention(sinks[l], q, k, v, seq, mod_rider=attn_rider)
        attn, kt_p, vt_p = attended[:3]
        if attn_rider is not None:
            mod_next = attended[3]
        hp, h2 = _mix_out(hp, mod_p, g2, attn, gm, w_out_b, seq)
        outs[0].append(kt_p)
        outs[1].append(vt_p)
        outs[2].append(gvl)

        hs, a_new, wg_b, wu_b, wd_b, head_rows, halo = _ffn_head(
            l, hs, mod_s, g2, w_gate, w_up, w_down, cw_head[l], cb_head[l], state_conv[l, :, 0], state_conv[l, :, 1],
            g_fin, hp, h2, mod_p, last)
        hp, tail = _ffn(hp, head_rows, halo, h2, mod_p, wg_b, wu_b, wd_b, cw_tiles[l], cb_tiles[l], g_fin, seq, last)
        outs[7].append(jnp.stack([state_conv[l, :, 1], a_new], axis=1))
        tails = tail[seq // TM_FF - 2::seq // TM_FF, :, V7X_SUBLANES - 2:, :]
        outs[3].append(jnp.swapaxes(tails, 1, 2).reshape(batch, 2, D_FF))

    st = [jnp.stack(o) for o in outs]
    return (hp.reshape(batch, seq, D_MODEL), hs.reshape(dec, 1, D_MODEL),
            _from_planes(st[0]), _from_planes(st[1]), st[2], st[3],
            _from_planes(st[4]), _from_planes(st[5]), st[6], st[7])
```
